```python
import math
import jax, jax.numpy as jnp
from jax import lax
import numpy as np

D_MODEL = 1024
BATCH = 8
SEQ = 4096
DEPTH = 4

N_POOL_LAYERS = DEPTH // 2
N_ATTN_LAYERS = DEPTH - N_POOL_LAYERS
POOL_WINDOWS = (2, 4, 8, 16)
N_POOL_GROUPS = len(POOL_WINDOWS)
POOL_GROUP_DIM = D_MODEL // N_POOL_GROUPS
BRANCHES = ((128, 1), (512, 4), (2048, 16))
N_BRANCHES = len(BRANCHES)
HEAD_DIM = 64
N_HEADS = D_MODEL // HEAD_DIM
D_ATTN = N_HEADS * HEAD_DIM
ATTN_BLOCK = 128
D_FF = 2816
CONV_WIDTH = 3
EPS = 1e-6
ADA_SCALE = 0.5

kernel_name = "yoco_pool_dilated_alibi_hybrid"


def _rmsnorm(x, g):
    x32 = x.astype(jnp.float32)
    y = x32 * lax.rsqrt(jnp.mean(x32 * x32, axis=-1, keepdims=True) + EPS)
    return (y * g.astype(jnp.float32)).astype(x.dtype)


def _modulate(h, shift, scale):
    return h * (1 + scale[:, None, :]) + shift[:, None, :]


def _alibi_slopes(n):
    def pow2(m):
        start = 2.0 ** (-(2.0 ** -(math.log2(m) - 3)))
        return [start ** (i + 1) for i in range(m)]
    if math.log2(n).is_integer():
        s = pow2(n)
    else:
        c = 2 ** math.floor(math.log2(n))
        s = pow2(c) + pow2(2 * c)[0::2][: n - c]
    s = np.asarray(s, dtype=np.float32)
    return -np.sort(-s)


def _pool_mixer(h, w_in, w_grp, scale, w_out):
    b, s, _ = h.shape
    u = (h @ w_in).reshape(b, s, N_POOL_GROUPS, POOL_GROUP_DIM)
    u32 = u.astype(jnp.float32)
    csum = jnp.cumsum(u32, axis=1)
    t = jnp.arange(s)
    outs = []
    for g, w in enumerate(POOL_WINDOWS):
        cs = csum[:, :, g]
        lag = jnp.pad(cs, ((0, 0), (w, 0), (0, 0)))[:, :s]
        count = jnp.minimum(t + 1, w).astype(jnp.float32)[None, :, None]
        pooled = (cs - lag) / count - u32[:, :, g]
        outs.append(jnp.einsum('bsc,cd->bsd', pooled.astype(h.dtype), w_grp[g]))
    y = jnp.concatenate(outs, axis=-1) * scale
    return y @ w_out


def _causal_dwconv(a, w, bias):
    s = a.shape[1]
    ap = jnp.pad(a, ((0, 0), (CONV_WIDTH - 1, 0), (0, 0)))
    y = bias
    for k in range(CONV_WIDTH):
        y = y + ap[:, k:k + s] * w[k]
    return y


def _conv_ffn(h, w_up, conv_w, conv_b, w_down):
    a, v = jnp.split(h @ w_up, 2, axis=-1)
    a = _causal_dwconv(a, conv_w, conv_b)
    return (jax.nn.silu(a) * v) @ w_down


def _dilated_branch(q, k, v, window, dilation, slopes):
    b, s, h, dh = q.shape
    n_steps = window // dilation
    blk = max(ATTN_BLOCK, n_steps)
    sub_len = s // dilation
    nb = -(-sub_len // blk)
    sub_pad = nb * blk

    def to_sub(t):
        t = t.reshape(b, sub_len, dilation, h, dh).transpose(0, 2, 1, 3, 4).reshape(b * dilation, sub_len, h, dh)
        return jnp.pad(t, ((0, 0), (0, sub_pad - sub_len), (0, 0), (0, 0)))

    def with_prev(t):
        tb = t.reshape(-1, nb, blk, h, dh)
        prev = jnp.pad(tb, ((0, 0), (1, 0), (0, 0), (0, 0), (0, 0)))[:, :nb]
        return jnp.concatenate([prev, tb], axis=2)

    qb = to_sub(q).reshape(-1, nb, blk, h, dh)
    kb = with_prev(to_sub(k))
    vb = with_prev(to_sub(v))

    scores = jnp.einsum('bnqhd,bnkhd->bnhqk', qb, kb).astype(jnp.float32) * (dh ** -0.5)
    qi = jnp.arange(blk)[:, None] + blk
    ki = jnp.arange(2 * blk)[None, :]
    delta = qi - ki
    key_idx = jnp.arange(nb)[:, None] * blk + jnp.arange(2 * blk)[None, :] - blk
    valid = ((delta >= 0) & (delta <= n_steps))[None] & (key_idx >= 0)[:, None, :]
    bias = -slopes[:, None, None] * (delta * dilation).astype(jnp.float32)[None]
    scores = jnp.where(valid[None, :, None], scores + bias[None, None], -jnp.inf)
    lse = jax.nn.logsumexp(scores, axis=-1)
    p = jnp.exp(scores - lse[..., None])
    out = jnp.einsum('bnhqk,bnkhd->bnqhd', p.astype(v.dtype), vb)

    out = out.reshape(b, dilation, sub_pad, h, dh)[:, :, :sub_len]
    out = out.transpose(0, 2, 1, 3, 4).reshape(b, s, h, dh)
    lse = lse.transpose(0, 1, 3, 2).reshape(b, dilation, sub_pad, h)[:, :, :sub_len]
    lse = lse.transpose(0, 2, 1, 3).reshape(b, s, h)
    return out, lse


def _dilated_attention(h, kv, w_q, w_o, slopes):
    b, s, _ = h.shape
    q = (h @ w_q).reshape(b, s, N_BRANCHES, N_HEADS, HEAD_DIM)
    outs, lses = [], []
    for g, (window, dil) in enumerate(BRANCHES):
        o, l = _dilated_branch(q[:, :, g], kv[:, :, 0, g], kv[:, :, 1, g], window, dil, slopes[g])
        outs.append(o)
        lses.append(l)
    wts = jax.nn.softmax(jnp.stack(lses, axis=0), axis=0)
    o = jnp.sum(wts[..., None] * jnp.stack(outs, axis=0).astype(jnp.float32), axis=0)
    return o.reshape(b, s, D_ATTN).astype(h.dtype) @ w_o


def setup_inputs(seed: int = 0) -> dict:
    key = jax.random.key(seed)
    ks = jax.random.split(key, 24)

    def nrm(k, shape, scale):
        return jax.random.normal(k, shape, jnp.float32) * scale

    D, F, G = D_MODEL, D_FF, N_BRANCHES
    return {
        "x": nrm(ks[0], (BATCH, SEQ, D), 1.0),
        "c": nrm(ks[1], (BATCH, D), 1.0),
        "ada_w": nrm(ks[2], (DEPTH, D, 6 * D), ADA_SCALE * D ** -0.5),
        "ada_b": nrm(ks[3], (DEPTH, 6 * D), 0.02),
        "norm1_g": 1.0 + nrm(ks[4], (DEPTH, D), 0.05),
        "norm2_g": 1.0 + nrm(ks[5], (DEPTH, D), 0.05),
        "pool_w_in": nrm(ks[6], (N_POOL_LAYERS, D, D), D ** -0.5),
        "pool_w_grp": nrm(ks[7], (N_POOL_LAYERS, N_POOL_GROUPS, POOL_GROUP_DIM, POOL_GROUP_DIM), POOL_GROUP_DIM ** -0.5),
        "pool_scale": 1.0 + nrm(ks[8], (N_POOL_LAYERS, D), 0.1),
        "pool_w_out": nrm(ks[9], (N_POOL_LAYERS, D, D), D ** -0.5),
        "kv_norm_g": 1.0 + nrm(ks[10], (D,), 0.05),
        "kv_ada_w": nrm(ks[11], (D, 2 * D), ADA_SCALE * D ** -0.5),
        "kv_ada_b": nrm(ks[12], (2 * D,), 0.02),
        "w_kv": nrm(ks[13], (D, 2 * G * D_ATTN), D ** -0.5),
        "attn_w_q": nrm(ks[14], (N_ATTN_LAYERS, D, G * D_ATTN), D ** -0.5),
        "attn_w_o": nrm(ks[15], (N_ATTN_LAYERS, D_ATTN, D), D_ATTN ** -0.5),
        "ffn_w_up": nrm(ks[16], (DEPTH, D, 2 * F), D ** -0.5),
        "ffn_conv_w": nrm(ks[17], (DEPTH, CONV_WIDTH, F), CONV_WIDTH ** -0.5),
        "ffn_conv_b": nrm(ks[18], (DEPTH, F), 0.02),
        "ffn_w_down": nrm(ks[19], (DEPTH, F, D), F ** -0.5),
        "final_g": 1.0 + nrm(ks[20], (D,), 0.05),
    }


def reference(x, c, ada_w, ada_b, norm1_g, norm2_g, pool_w_in, pool_w_grp, pool_scale, pool_w_out,
              kv_norm_g, kv_ada_w, kv_ada_b, w_kv, attn_w_q, attn_w_o,
              ffn_w_up, ffn_conv_w, ffn_conv_b, ffn_w_down, final_g):
    b, s, _ = x.shape
    cond = jax.nn.silu(c)
    slopes = jnp.asarray(_alibi_slopes(N_BRANCHES * N_HEADS)).reshape(N_BRANCHES, N_HEADS)
    kv = None
    for layer in range(DEPTH):
        mod = cond @ ada_w[layer] + ada_b[layer]
        sh1, sc1, g1, sh2, sc2, g2 = jnp.split(mod, 6, axis=-1)
        h = _modulate(_rmsnorm(x, norm1_g[layer]), sh1, sc1)
        if layer < N_POOL_LAYERS:
            y = _pool_mixer(h, pool_w_in[layer], pool_w_grp[layer], pool_scale[layer], pool_w_out[layer])
        else:
            if layer == N_POOL_LAYERS:
                kv_shift, kv_scale = jnp.split(cond @ kv_ada_w + kv_ada_b, 2, axis=-1)
                hkv = _modulate(_rmsnorm(x, kv_norm_g), kv_shift, kv_scale)
                kv = (hkv @ w_kv).reshape(b, s, 2, N_BRANCHES, N_HEADS, HEAD_DIM)
            j = layer - N_POOL_LAYERS
            y = _dilated_attention(h, kv, attn_w_q[j], attn_w_o[j], slopes)
        x = x + g1[:, None, :] * y
        h = _modulate(_rmsnorm(x, norm2_g[layer]), sh2, sc2)
        x = x + g2[:, None, :] * _conv_ffn(h, ffn_w_up[layer], ffn_conv_w[layer], ffn_conv_b[layer], ffn_w_down[layer])
    return _rmsnorm(x, final_g)
```

```python
import functools
import math

import jax
import jax.numpy as jnp
import numpy as np
from jax import lax
from jax.experimental import pallas as pl
from jax.experimental.pallas import tpu as pltpu

D_MODEL = 1024
POOL_WINDOWS = (2, 4, 8, 16)
POOL_GROUP_DIM = D_MODEL // len(POOL_WINDOWS)
BRANCHES = ((128, 1), (512, 4), (2048, 16))
N_BRANCHES = len(BRANCHES)
HEAD_DIM = 64
N_HEADS = D_MODEL // HEAD_DIM
D_ATTN = N_HEADS * HEAD_DIM
ATTN_BLOCK = 128
D_FF = 2816
CONV_WIDTH = 3
EPS = 1e-6

LANES = 128
CARRY_ROWS = 8
VMEM_LIMIT = 56 * 1024 * 1024

BF16 = jnp.bfloat16
F32 = jnp.float32


def _alibi_slopes(n):
    def pow2(m):
        start = 2.0 ** (-(2.0 ** -(math.log2(m) - 3)))
        return [start ** (i + 1) for i in range(m)]
    if math.log2(n).is_integer():
        s = pow2(n)
    else:
        c = 2 ** math.floor(math.log2(n))
        s = pow2(c) + pow2(2 * c)[0::2][: n - c]
    s = np.asarray(s, dtype=np.float32)
    return -np.sort(-s)


def _params(*sem):
    return pltpu.CompilerParams(dimension_semantics=sem, vmem_limit_bytes=VMEM_LIMIT)


def _resident(shape):
    nd = len(shape)
    return pl.BlockSpec(shape, lambda *_: (0,) * nd, pipeline_mode=pl.Buffered(1))


def _norm_mod(x, g, shift, scale):
    y = x * lax.rsqrt(jnp.mean(x * x, axis=-1, keepdims=True) + EPS)
    return (y * g) * (1.0 + scale) + shift


def _dot(a, b):
    return jnp.dot(a, b, preferred_element_type=F32)


def _ada_kernel(c_ref, w_ref, b_ref, o_ref):
    c = c_ref[...]
    cond = c * (1.0 / (1.0 + jnp.exp(-c)))
    o_ref[0] = _dot(cond.astype(BF16), w_ref[0].astype(BF16)) + b_ref[0]


def _ada(c, w, b, tn=1024):
    n_layers, d, n = w.shape
    bsz = c.shape[0]
    return pl.pallas_call(
        _ada_kernel,
        grid=(n_layers, n // tn),
        in_specs=[
            pl.BlockSpec((bsz, d), lambda l, j: (0, 0)),
            pl.BlockSpec((1, d, tn), lambda l, j: (l, 0, j)),
            pl.BlockSpec((1, 1, tn), lambda l, j: (l, 0, j)),
        ],
        out_specs=pl.BlockSpec((1, bsz, tn), lambda l, j: (l, 0, j)),
        out_shape=jax.ShapeDtypeStruct((n_layers, bsz, n), F32),
        compiler_params=_params("arbitrary", "arbitrary"),
        name="ada_mod",
    )(c, w, b.reshape(n_layers, 1, n))


def _pool_kernel(x_ref, mod_ref, g_ref, win_ref, wgrp_ref, scale_ref, wout_ref, o_ref,
                 s1_ref, s2_ref, s4_ref, s8_ref, *, tm):
    si = pl.program_id(1)
    c0 = CARRY_ROWS
    gd = POOL_GROUP_DIM
    stages = (s1_ref, s2_ref, s4_ref, s8_ref)

    @pl.when(si == 0)
    def _():
        for ref in stages:
            ref[0:c0, :] = jnp.zeros((c0, ref.shape[1]), F32)

    x = x_ref[0]
    mod = mod_ref[0]
    h = _norm_mod(x, g_ref[...], mod[0:1], mod[1:2])
    u = _dot(h.astype(BF16), win_ref[...])
    s1_ref[c0:c0 + tm, :] = u
    for k in range(1, len(stages)):
        prev, cur = stages[k - 1], stages[k]
        sh = 1 << (k - 1)
        cur[c0:c0 + tm, :] = prev[c0:c0 + tm, gd:] + prev[c0 - sh:c0 - sh + tm, gd:]

    t = si * tm + lax.broadcasted_iota(jnp.int32, (tm, 1), 0)
    ys = []
    for g, w in enumerate(POOL_WINDOWS):
        ref = stages[g]
        sh = w // 2
        wsum = ref[c0:c0 + tm, 0:gd] + ref[c0 - sh:c0 - sh + tm, 0:gd]
        count = jnp.minimum(t + 1, w).astype(F32)
        pooled = wsum / count - u[:, g * gd:(g + 1) * gd]
        ys.append(_dot(pooled.astype(BF16), wgrp_ref[g]))
    y = jnp.concatenate(ys, axis=-1) * scale_ref[...]
    y = _dot(y.astype(BF16), wout_ref[...])
    o_ref[0] = x + mod[2:3] * y

    for ref in stages:
        ref[0:c0, :] = ref[tm:tm + c0, :]


def _pool_layer(x, mod, g, w_in, w_grp, scale, w_out, tm=512):
    bsz, s, d = x.shape
    gd = POOL_GROUP_DIM
    return pl.pallas_call(
        functools.partial(_pool_kernel, tm=tm),
        grid=(bsz, s // tm),
        in_specs=[
            pl.BlockSpec((1, tm, d), lambda b, i: (b, i, 0)),
            pl.BlockSpec((1, 6, d), lambda b, i: (b, 0, 0)),
            _resident((1, d)),
            _resident((d, d)),
            _resident((len(POOL_WINDOWS), gd, gd)),
            _resident((1, d)),
            _resident((d, d)),
        ],
        out_specs=pl.BlockSpec((1, tm, d), lambda b, i: (b, i, 0)),
        out_shape=jax.ShapeDtypeStruct(x.shape, F32),
        scratch_shapes=[pltpu.VMEM((tm + CARRY_ROWS, d - k * gd), F32) for k in range(4)],
        compiler_params=_params("arbitrary", "arbitrary"),
        name="pool_mixer",
    )(x, mod, g.reshape(1, d), w_in, w_grp, scale.reshape(1, d), w_out)


def _ffn_kernel(x_ref, mod_ref, g_ref, wup_ref, cw_ref, cb_ref, wdown_ref, fg_ref, o_ref,
                a_ref, *, tm, n_chunks, final_norm):
    si = pl.program_id(1)
    c0 = CARRY_ROWS
    f = D_FF
    tf = f // n_chunks

    @pl.when(si == 0)
    def _():
        a_ref[0:c0, :] = jnp.zeros((c0, f), F32)

    x = x_ref[0]
    mod = mod_ref[0]
    h = _norm_mod(x, g_ref[...], mod[3:4], mod[4:5]).astype(BF16)
    acc = None
    for j in range(n_chunks):
        lo = j * tf
        a = _dot(h, wup_ref[:, lo:lo + tf])
        v = _dot(h, wup_ref[:, f + lo:f + lo + tf])
        a_ref[c0:c0 + tm, lo:lo + tf] = a
        y = cb_ref[:, lo:lo + tf]
        y = y + a_ref[c0 - 2:c0 - 2 + tm, lo:lo + tf] * cw_ref[0:1, lo:lo + tf]
        y = y + a_ref[c0 - 1:c0 - 1 + tm, lo:lo + tf] * cw_ref[1:2, lo:lo + tf]
        y = y + a * cw_ref[2:3, lo:lo + tf]
        gated = y * (1.0 / (1.0 + jnp.exp(-y))) * v
        part = _dot(gated.astype(BF16), wdown_ref[lo:lo + tf, :])
        acc = part if acc is None else acc + part
    out = x + mod[5:6] * acc
    if final_norm:
        out = (out * lax.rsqrt(jnp.mean(out * out, axis=-1, keepdims=True) + EPS)) * fg_ref[...]
    o_ref[0] = out
    a_ref[0:c0, :] = a_ref[tm:tm + c0, :]


def _ffn_layer(x, mod, g, w_up, conv_w, conv_b, w_down, final_g, final_norm, tm=512, n_chunks=2):
    bsz, s, d = x.shape
    f = D_FF
    return pl.pallas_call(
        functools.partial(_ffn_kernel, tm=tm, n_chunks=n_chunks, final_norm=final_norm),
        grid=(bsz, s // tm),
        in_specs=[
            pl.BlockSpec((1, tm, d), lambda b, i: (b, i, 0)),
            pl.BlockSpec((1, 6, d), lambda b, i: (b, 0, 0)),
            _resident((1, d)),
            _resident((d, 2 * f)),
            _resident((CONV_WIDTH, f)),
            _resident((1, f)),
            _resident((f, d)),
            _resident((1, d)),
        ],
        out_specs=pl.BlockSpec((1, tm, d), lambda b, i: (b, i, 0)),
        out_shape=jax.ShapeDtypeStruct(x.shape, F32),
        scratch_shapes=[pltpu.VMEM((tm + CARRY_ROWS, f), F32)],
        compiler_params=_params("arbitrary", "arbitrary"),
        name="conv_ffn",
    )(x, mod, g.reshape(1, d), w_up, conv_w, conv_b.reshape(1, f), w_down, final_g.reshape(1, d))


def _proj_kernel(x_ref, shift_ref, scale_ref, g_ref, w_ref, o_ref, *, tn, out_scale):
    h = _norm_mod(x_ref[0], g_ref[...], shift_ref[0], scale_ref[0]).astype(BF16)
    n = w_ref.shape[1]
    for j in range(n // tn):
        r = _dot(h, w_ref[:, j * tn:(j + 1) * tn])
        if out_scale != 1.0:
            r = r * out_scale
        o_ref[0, :, j * tn:(j + 1) * tn] = r.astype(BF16)


def _proj(x, shift, scale, g, w, out_scale=1.0, tm=512, tn=1024):
    bsz, s, d = x.shape
    n = w.shape[1]
    return pl.pallas_call(
        functools.partial(_proj_kernel, tn=tn, out_scale=out_scale),
        grid=(bsz, s // tm),
        in_specs=[
            pl.BlockSpec((1, tm, d), lambda b, i: (b, i, 0)),
            pl.BlockSpec((1, 1, d), lambda b, i: (b, 0, 0)),
            pl.BlockSpec((1, 1, d), lambda b, i: (b, 0, 0)),
            _resident((1, d)),
            _resident((d, n)),
        ],
        out_specs=pl.BlockSpec((1, tm, n), lambda b, i: (b, i, 0)),
        out_shape=jax.ShapeDtypeStruct((bsz, s, n), BF16),
        compiler_params=_params("arbitrary", "arbitrary"),
        name="norm_proj",
    )(x, shift, scale, g.reshape(1, d), w)


def _attn_kernel(q_ref, kp_ref, kc_ref, vp_ref, vc_ref, o_ref, lse_ref, *, slopes, dilation, n_steps):
    blk = ATTN_BLOCK
    j = pl.program_id(2)
    row = lax.broadcasted_iota(jnp.int32, (blk, 2 * blk), 0)
    col = lax.broadcasted_iota(jnp.int32, (blk, 2 * blk), 1)
    delta = row + blk - col
    first_key_col = jnp.where(j > 0, 0, blk)
    valid = (delta >= 0) & (delta <= n_steps) & (col >= first_key_col)
    dist = (delta * dilation).astype(F32)
    lane = lax.broadcasted_iota(jnp.int32, (blk, LANES), 1)
    low_half = lane < HEAD_DIM
    lse_mat = jnp.zeros((blk, LANES), F32)

    for pair in range(N_HEADS // 2):
        cs = slice(pair * LANES, (pair + 1) * LANES)
        q = q_ref[0, :, cs]
        k = jnp.concatenate([kp_ref[0, :, cs], kc_ref[0, :, cs]], axis=0)
        v = jnp.concatenate([vp_ref[0, :, cs], vc_ref[0, :, cs]], axis=0)
        zero = jnp.zeros_like(q)
        probs = []
        for half in range(2):
            head = 2 * pair + half
            keep = low_half if half == 0 else ~low_half
            s = lax.dot_general(jnp.where(keep, q, zero), k, (((1,), (1,)), ((), ())),
                                preferred_element_type=F32)
            s = jnp.where(valid, s - slopes[head] * dist, -jnp.inf)
            m = jnp.max(s, axis=-1, keepdims=True)
            e = jnp.exp(s - m)
            l = jnp.sum(e, axis=-1, keepdims=True)
            probs.append((e * (1.0 / l)).astype(BF16))
            lse_mat = jnp.where(lane == head, m + jnp.log(l), lse_mat)
        low_rows = lax.broadcasted_iota(jnp.int32, (2 * blk, LANES), 1) < HEAD_DIM
        zv = jnp.zeros_like(v)
        v2 = jnp.concatenate([jnp.where(low_rows, v, zv), jnp.where(low_rows, zv, v)], axis=0)
        p2 = jnp.concatenate(probs, axis=-1)
        o_ref[0, :, cs] = _dot(p2, v2)
    lse_ref[0] = lse_mat


def _attn_branch(q, kv, branch, slopes):
    window, d = BRANCHES[branch]
    bsz, s, _ = q.shape
    n_steps = window // d
    blk = ATTN_BLOCK
    assert n_steps <= blk and (s // d) % blk == 0
    sub = s // d
    nb = sub // blk
    qw, kvw = N_BRANCHES, 2 * N_BRANCHES
    qv = q.reshape(bsz, sub, d * qw * D_ATTN)
    kvv = kv.reshape(bsz, sub, d * kvw * D_ATTN)
    g = branch
    blockspec = lambda fn: pl.BlockSpec((1, blk, D_ATTN), fn)
    out, lse = pl.pallas_call(
        functools.partial(_attn_kernel, slopes=tuple(float(v) for v in slopes),
                          dilation=d, n_steps=n_steps),
        grid=(bsz, d, nb),
        in_specs=[
            blockspec(lambda b, r, j: (b, j, r * qw + g)),
            blockspec(lambda b, r, j: (b, jnp.maximum(j - 1, 0), r * kvw + g)),
            blockspec(lambda b, r, j: (b, j, r * kvw + g)),
            blockspec(lambda b, r, j: (b, jnp.maximum(j - 1, 0), r * kvw + N_BRANCHES + g)),
            blockspec(lambda b, r, j: (b, j, r * kvw + N_BRANCHES + g)),
        ],
        out_specs=[
            blockspec(lambda b, r, j: (b, j, r)),
            pl.BlockSpec((1, blk, LANES), lambda b, r, j: (b, j, r)),
        ],
        out_shape=[
            jax.ShapeDtypeStruct((bsz, sub, d * D_ATTN), F32),
            jax.ShapeDtypeStruct((bsz, sub, d * LANES), F32),
        ],
        compiler_params=_params("arbitrary", "arbitrary", "arbitrary"),
        name=f"dilated_attn_{branch}",
    )(qv, kvv, kvv, kvv, kvv)
    return out.reshape(bsz, s, D_ATTN), lse.reshape(bsz, s, LANES)


def _combine_kernel(o0_ref, o1_ref, o2_ref, l0_ref, l1_ref, l2_ref, x_ref, mod_ref, wo_ref, out_ref, *, tm):
    outs = (o0_ref, o1_ref, o2_ref)
    lses = [r[0] for r in (l0_ref, l1_ref, l2_ref)]
    m = jnp.maximum(jnp.maximum(lses[0], lses[1]), lses[2])
    es = [jnp.exp(l - m) for l in lses]
    inv = 1.0 / (es[0] + es[1] + es[2])
    wts = [e * inv for e in es]
    low_half = lax.broadcasted_iota(jnp.int32, (tm, LANES), 1) < HEAD_DIM
    cols = []
    for pair in range(N_HEADS // 2):
        cs = slice(pair * LANES, (pair + 1) * LANES)
        acc = None
        for g in range(N_BRANCHES):
            wa = jnp.broadcast_to(wts[g][:, 2 * pair:2 * pair + 1], (tm, LANES))
            wb = jnp.broadcast_to(wts[g][:, 2 * pair + 1:2 * pair + 2], (tm, LANES))
            term = jnp.where(low_half, wa, wb) * outs[g][0, :, cs]
            acc = term if acc is None else acc + term
        cols.append(acc.astype(BF16))
    o = jnp.concatenate(cols, axis=-1)
    mod = mod_ref[0]
    out_ref[0] = x_ref[0] + mod[2:3] * _dot(o, wo_ref[...])


def _combine(outs, lses, x, mod, w_o, tm=512):
    bsz, s, d = x.shape
    tile = lambda w: pl.BlockSpec((1, tm, w), lambda b, i: (b, i, 0))
    return pl.pallas_call(
        functools.partial(_combine_kernel, tm=tm),
        grid=(bsz, s // tm),
        in_specs=[tile(D_ATTN)] * 3 + [tile(LANES)] * 3 + [
            tile(d),
            pl.BlockSpec((1, 6, d), lambda b, i: (b, 0, 0)),
            _resident((D_ATTN, d)),
        ],
        out_specs=tile(d),
        out_shape=jax.ShapeDtypeStruct(x.shape, F32),
        compiler_params=_params("arbitrary", "arbitrary"),
        name="branch_mix_out_proj",
    )(*outs, *lses, x, mod, w_o)


def kernel(x, c, ada_w, ada_b, norm1_g, norm2_g, pool_w_in, pool_w_grp, pool_scale, pool_w_out,
           kv_norm_g, kv_ada_w, kv_ada_b, w_kv, attn_w_q, attn_w_o,
           ffn_w_up, ffn_conv_w, ffn_conv_b, ffn_w_down, final_g):
    bsz, s, d = x.shape
    depth = ada_w.shape[0]
    n_pool = pool_w_in.shape[0]
    slopes = _alibi_slopes(N_BRANCHES * N_HEADS).reshape(N_BRANCHES, N_HEADS)

    mods = _ada(c, ada_w, ada_b).reshape(depth, bsz, 6, d)
    kv_mod = _ada(c, kv_ada_w[None], kv_ada_b[None]).reshape(bsz, 2, 1, d)

    kv = None
    for layer in range(depth):
        mod = mods[layer]
        if layer < n_pool:
            x = _pool_layer(x, mod, norm1_g[layer], pool_w_in[layer].astype(BF16),
                            pool_w_grp[layer].astype(BF16), pool_scale[layer],
                            pool_w_out[layer].astype(BF16))
        else:
            if layer == n_pool:
                kv = _proj(x, kv_mod[:, 0], kv_mod[:, 1], kv_norm_g, w_kv.astype(BF16))
            jl = layer - n_pool
            q = _proj(x, mod[:, 0:1], mod[:, 1:2], norm1_g[layer], attn_w_q[jl].astype(BF16),
                      out_scale=HEAD_DIM ** -0.5)
            outs, lses = zip(*[_attn_branch(q, kv, g, slopes[g]) for g in range(N_BRANCHES)])
            x = _combine(outs, lses, x, mod, attn_w_o[jl].astype(BF16))
        x = _ffn_layer(x, mod, norm2_g[layer], ffn_w_up[layer].astype(BF16), ffn_conv_w[layer],
                       ffn_conv_b[layer], ffn_w_down[layer].astype(BF16), final_g,
                       final_norm=(layer == depth - 1))
    return x
```

```python
import functools
import math

import jax
import jax.numpy as jnp
import numpy as np
from jax import lax
from jax.experimental import pallas as pl
from jax.experimental.pallas import tpu as pltpu

D_MODEL = 1024
POOL_WINDOWS = (2, 4, 8, 16)
POOL_GROUP_DIM = D_MODEL // len(POOL_WINDOWS)
BRANCHES = ((128, 1), (512, 4), (2048, 16))
N_BRANCHES = len(BRANCHES)
HEAD_DIM = 64
N_HEADS = D_MODEL // HEAD_DIM
D_ATTN = N_HEADS * HEAD_DIM
ATTN_BLOCK = 128
D_FF = 2816
CONV_WIDTH = 3
EPS = 1e-6

LANES = 128
CARRY_ROWS = 8
VMEM_LIMIT = 56 * 1024 * 1024

BF16 = jnp.bfloat16
F32 = jnp.float32


def _alibi_slopes(n):
    def pow2(m):
        start = 2.0 ** (-(2.0 ** -(math.log2(m) - 3)))
        return [start ** (i + 1) for i in range(m)]
    if math.log2(n).is_integer():
        s = pow2(n)
    else:
        c = 2 ** math.floor(math.log2(n))
        s = pow2(c) + pow2(2 * c)[0::2][: n - c]
    s = np.asarray(s, dtype=np.float32)
    return -np.sort(-s)


def _params(*sem):
    return pltpu.CompilerParams(dimension_semantics=sem, vmem_limit_bytes=VMEM_LIMIT)


def _resident(shape):
    nd = len(shape)
    return pl.BlockSpec(shape, lambda *_: (0,) * nd, pipeline_mode=pl.Buffered(1))


def _norm_mod(x, g, shift, scale):
    y = x * lax.rsqrt(jnp.mean(x * x, axis=-1, keepdims=True) + EPS)
    return (y * g) * (1.0 + scale) + shift


def _dot(a, b):
    return jnp.dot(a, b, preferred_element_type=F32)


def _ada_kernel(c_ref, w_ref, b_ref, o_ref):
    c = c_ref[...]
    cond = c * (1.0 / (1.0 + jnp.exp(-c)))
    o_ref[0] = _dot(cond.astype(BF16), w_ref[0].astype(BF16)) + b_ref[0]


def _ada(c, w, b, tn=1024):
    n_layers, d, n = w.shape
    bsz = c.shape[0]
    return pl.pallas_call(
        _ada_kernel,
        grid=(n_layers, n // tn),
        in_specs=[
            pl.BlockSpec((bsz, d), lambda l, j: (0, 0)),
            pl.BlockSpec((1, d, tn), lambda l, j: (l, 0, j)),
            pl.BlockSpec((1, 1, tn), lambda l, j: (l, 0, j)),
        ],
        out_specs=pl.BlockSpec((1, bsz, tn), lambda l, j: (l, 0, j)),
        out_shape=jax.ShapeDtypeStruct((n_layers, bsz, n), F32),
        compiler_params=_params("arbitrary", "arbitrary"),
        name="ada_mod",
    )(c, w, b.reshape(n_layers, 1, n))


def _pool_kernel(x_ref, mod_ref, g_ref, win_ref, wgrp_ref, scale_ref, wout_ref, o_ref,
                 s1_ref, s2_ref, s4_ref, s8_ref, *, tm):
    si = pl.program_id(1)
    c0 = CARRY_ROWS
    gd = POOL_GROUP_DIM
    stages = (s1_ref, s2_ref, s4_ref, s8_ref)

    @pl.when(si == 0)
    def _():
        for ref in stages:
            ref[0:c0, :] = jnp.zeros((c0, ref.shape[1]), F32)

    x = x_ref[0]
    mod = mod_ref[0]
    h = _norm_mod(x, g_ref[...], mod[0:1], mod[1:2])
    u = _dot(h.astype(BF16), win_ref[...])
    s1_ref[c0:c0 + tm, :] = u
    for k in range(1, len(stages)):
        prev, cur = stages[k - 1], stages[k]
        sh = 1 << (k - 1)
        cur[c0:c0 + tm, :] = prev[c0:c0 + tm, gd:] + prev[c0 - sh:c0 - sh + tm, gd:]

    t = si * tm + lax.broadcasted_iota(jnp.int32, (tm, 1), 0)
    ys = []
    for g, w in enumerate(POOL_WINDOWS):
        ref = stages[g]
        sh = w // 2
        wsum = ref[c0:c0 + tm, 0:gd] + ref[c0 - sh:c0 - sh + tm, 0:gd]
        count = jnp.minimum(t + 1, w).astype(F32)
        pooled = wsum / count - u[:, g * gd:(g + 1) * gd]
        ys.append(_dot(pooled.astype(BF16), wgrp_ref[g]))
    y = jnp.concatenate(ys, axis=-1) * scale_ref[...]
    y = _dot(y.astype(BF16), wout_ref[...])
    o_ref[0] = x + mod[2:3] * y

    for ref in stages:
        ref[0:c0, :] = ref[tm:tm + c0, :]


def _pool_layer(x, mod, g, w_in, w_grp, scale, w_out, tm=512):
    bsz, s, d = x.shape
    gd = POOL_GROUP_DIM
    return pl.pallas_call(
        functools.partial(_pool_kernel, tm=tm),
        grid=(bsz, s // tm),
        in_specs=[
            pl.BlockSpec((1, tm, d), lambda b, i: (b, i, 0)),
            pl.BlockSpec((1, 6, d), lambda b, i: (b, 0, 0)),
            _resident((1, d)),
            _resident((d, d)),
            _resident((len(POOL_WINDOWS), gd, gd)),
            _resident((1, d)),
            _resident((d, d)),
        ],
        out_specs=pl.BlockSpec((1, tm, d), lambda b, i: (b, i, 0)),
        out_shape=jax.ShapeDtypeStruct(x.shape, F32),
        scratch_shapes=[pltpu.VMEM((tm + CARRY_ROWS, d - k * gd), F32) for k in range(4)],
        compiler_params=_params("arbitrary", "arbitrary"),
        name="pool_mixer",
    )(x, mod, g.reshape(1, d), w_in, w_grp, scale.reshape(1, d), w_out)


def _ffn_kernel(x_ref, mod_ref, g_ref, wup_ref, cw_ref, cb_ref, wdown_ref, fg_ref, o_ref,
                a_ref, *, tm, n_chunks, final_norm):
    si = pl.program_id(1)
    c0 = CARRY_ROWS
    f = D_FF
    tf = f // n_chunks

    @pl.when(si == 0)
    def _():
        a_ref[0:c0, :] = jnp.zeros((c0, f), F32)

    x = x_ref[0]
    mod = mod_ref[0]
    h = _norm_mod(x, g_ref[...], mod[3:4], mod[4:5]).astype(BF16)
    acc = None
    for j in range(n_chunks):
        lo = j * tf
        a = _dot(h, wup_ref[:, lo:lo + tf])
        v = _dot(h, wup_ref[:, f + lo:f + lo + tf])
        a_ref[c0:c0 + tm, lo:lo + tf] = a
        y = cb_ref[:, lo:lo + tf]
        y = y + a_ref[c0 - 2:c0 - 2 + tm, lo:lo + tf] * cw_ref[0:1, lo:lo + tf]
        y = y + a_ref[c0 - 1:c0 - 1 + tm, lo:lo + tf] * cw_ref[1:2, lo:lo + tf]
        y = y + a * cw_ref[2:3, lo:lo + tf]
        gated = y * (1.0 / (1.0 + jnp.exp(-y))) * v
        part = _dot(gated.astype(BF16), wdown_ref[lo:lo + tf, :])
        acc = part if acc is None else acc + part
    out = x + mod[5:6] * acc
    if final_norm:
        out = (out * lax.rsqrt(jnp.mean(out * out, axis=-1, keepdims=True) + EPS)) * fg_ref[...]
    o_ref[0] = out
    a_ref[0:c0, :] = a_ref[tm:tm + c0, :]


def _ffn_layer(x, mod, g, w_up, conv_w, conv_b, w_down, final_g, final_norm, tm=512, n_chunks=2):
    bsz, s, d = x.shape
    f = D_FF
    return pl.pallas_call(
        functools.partial(_ffn_kernel, tm=tm, n_chunks=n_chunks, final_norm=final_norm),
        grid=(bsz, s // tm),
        in_specs=[
            pl.BlockSpec((1, tm, d), lambda b, i: (b, i, 0)),
            pl.BlockSpec((1, 6, d), lambda b, i: (b, 0, 0)),
            _resident((1, d)),
            _resident((d, 2 * f)),
            _resident((CONV_WIDTH, f)),
            _resident((1, f)),
            _resident((f, d)),
            _resident((1, d)),
        ],
        out_specs=pl.BlockSpec((1, tm, d), lambda b, i: (b, i, 0)),
        out_shape=jax.ShapeDtypeStruct(x.shape, F32),
        scratch_shapes=[pltpu.VMEM((tm + CARRY_ROWS, f), F32)],
        compiler_params=_params("arbitrary", "arbitrary"),
        name="conv_ffn",
    )(x, mod, g.reshape(1, d), w_up, conv_w, conv_b.reshape(1, f), w_down, final_g.reshape(1, d))


def _proj_kernel(x_ref, shift_ref, scale_ref, g_ref, w_ref, *rest, tm, n_tensors, out_scale):
    o_refs, h_ref = rest[:-1], rest[-1]
    h = _norm_mod(x_ref[0], g_ref[...], shift_ref[0], scale_ref[0])
    n_slabs = D_MODEL // LANES
    for c in range(n_slabs):
        h_ref[c] = h[:, c * LANES:(c + 1) * LANES]
    for g, (_, d) in enumerate(BRANCHES):
        n = tm // d
        if d == 1:
            hp = h.astype(BF16)
        else:
            hp = jnp.concatenate(
                [jnp.concatenate([h_ref[c, pl.ds(r, n, stride=d), :] for c in range(n_slabs)], axis=1)
                 for r in range(d)], axis=0).astype(BF16)
        for t in range(n_tensors):
            col = (t * N_BRANCHES + g) * D_ATTN
            res = _dot(hp, w_ref[:, col:col + D_ATTN])
            if out_scale != 1.0:
                res = res * out_scale
            o_ref = o_refs[t * N_BRANCHES + g]
            for r in range(d):
                o_ref[0, r] = res[r * n:(r + 1) * n].astype(BF16)


def _proj(x, shift, scale, g, w, out_scale=1.0, tm=512):
    bsz, s, d_model = x.shape
    n_tensors = w.shape[1] // (N_BRANCHES * D_ATTN)
    dils = [d for _, d in BRANCHES] * n_tensors
    return pl.pallas_call(
        functools.partial(_proj_kernel, tm=tm, n_tensors=n_tensors, out_scale=out_scale),
        grid=(bsz, s // tm),
        in_specs=[
            pl.BlockSpec((1, tm, d_model), lambda b, i: (b, i, 0)),
            pl.BlockSpec((1, 1, d_model), lambda b, i: (b, 0, 0)),
            pl.BlockSpec((1, 1, d_model), lambda b, i: (b, 0, 0)),
            _resident((1, d_model)),
            _resident(w.shape),
        ],
        out_specs=[pl.BlockSpec((1, d, tm // d, D_ATTN), lambda b, i: (b, 0, i, 0)) for d in dils],
        out_shape=[jax.ShapeDtypeStruct((bsz, d, s // d, D_ATTN), BF16) for d in dils],
        scratch_shapes=[pltpu.VMEM((d_model // LANES, tm, LANES), F32)],
        compiler_params=_params("arbitrary", "arbitrary"),
        name="norm_proj",
    )(x, shift, scale, g.reshape(1, d_model), w)


def _attn_kernel(q_ref, kp_ref, kc_ref, vp_ref, vc_ref, o_ref, lse_ref, *, slopes, dilation, n_steps):
    blk = ATTN_BLOCK
    j = pl.program_id(2)
    row = lax.broadcasted_iota(jnp.int32, (blk, 2 * blk), 0)
    col = lax.broadcasted_iota(jnp.int32, (blk, 2 * blk), 1)
    delta = row + blk - col
    first_key_col = jnp.where(j > 0, 0, blk)
    valid = (delta >= 0) & (delta <= n_steps) & (col >= first_key_col)
    dist = (delta * dilation).astype(F32)
    lane = lax.broadcasted_iota(jnp.int32, (blk, LANES), 1)
    low_half = lane < HEAD_DIM
    lse_mat = jnp.zeros((blk, LANES), F32)

    for pair in range(N_HEADS // 2):
        cs = slice(pair * LANES, (pair + 1) * LANES)
        q = q_ref[:, cs]
        k = jnp.concatenate([kp_ref[:, cs], kc_ref[:, cs]], axis=0)
        v = jnp.concatenate([vp_ref[:, cs], vc_ref[:, cs]], axis=0)
        zero = jnp.zeros_like(q)
        probs = []
        for half in range(2):
            head = 2 * pair + half
            keep = low_half if half == 0 else ~low_half
            s = lax.dot_general(jnp.where(keep, q, zero), k, (((1,), (1,)), ((), ())),
                                preferred_element_type=F32)
            s = jnp.where(valid, s - slopes[head] * dist, -jnp.inf)
            m = jnp.max(s, axis=-1, keepdims=True)
            e = jnp.exp(s - m)
            l = jnp.sum(e, axis=-1, keepdims=True)
            probs.append((e * (1.0 / l)).astype(BF16))
            lse_mat = jnp.where(lane == head, m + jnp.log(l), lse_mat)
        low_rows = lax.broadcasted_iota(jnp.int32, (2 * blk, LANES), 1) < HEAD_DIM
        zv = jnp.zeros_like(v)
        v2 = jnp.concatenate([jnp.where(low_rows, v, zv), jnp.where(low_rows, zv, v)], axis=0)
        p2 = jnp.concatenate(probs, axis=-1)
        o_ref[:, cs] = _dot(p2, v2)
    lse_ref[...] = lse_mat


def _attn_branch(q, k, v, branch, slopes):
    window, d = BRANCHES[branch]
    bsz, _, sub, _ = q.shape
    n_steps = window // d
    blk = ATTN_BLOCK
    assert n_steps <= blk and sub % blk == 0
    cur = pl.BlockSpec((None, None, blk, D_ATTN), lambda b, r, j: (b, r, j, 0))
    prev = pl.BlockSpec((None, None, blk, D_ATTN), lambda b, r, j: (b, r, jnp.maximum(j - 1, 0), 0))
    return pl.pallas_call(
        functools.partial(_attn_kernel, slopes=tuple(float(v) for v in slopes),
                          dilation=d, n_steps=n_steps),
        grid=(bsz, d, sub // blk),
        in_specs=[cur, prev, cur, prev, cur],
        out_specs=[cur, pl.BlockSpec((None, None, blk, LANES), lambda b, r, j: (b, r, j, 0))],
        out_shape=[
            jax.ShapeDtypeStruct((bsz, d, sub, D_ATTN), F32),
            jax.ShapeDtypeStruct((bsz, d, sub, LANES), F32),
        ],
        compiler_params=_params("arbitrary", "arbitrary", "arbitrary"),
        name=f"dilated_attn_{branch}",
    )(q, k, k, v, v)


def _combine_kernel(o0_ref, o1_ref, o2_ref, l0_ref, l1_ref, l2_ref, x_ref, mod_ref, wo_ref, out_ref,
                    nat_o_ref, nat_l_ref, *, tm):
    o_refs = (o0_ref, o1_ref, o2_ref)
    l_refs = (l0_ref, l1_ref, l2_ref)
    n_slabs = D_ATTN // LANES
    for g, (_, d) in enumerate(BRANCHES):
        if d == 1:
            continue
        n = tm // d
        for r in range(d):
            nat_l_ref[g - 1, pl.ds(r, n, stride=d), :] = l_refs[g][r]
            for c in range(n_slabs):
                nat_o_ref[g - 1, c, pl.ds(r, n, stride=d), :] = o_refs[g][r, :, c * LANES:(c + 1) * LANES]
    lses = [l0_ref[0], nat_l_ref[0], nat_l_ref[1]]
    m = jnp.maximum(jnp.maximum(lses[0], lses[1]), lses[2])
    es = [jnp.exp(l - m) for l in lses]
    inv = 1.0 / (es[0] + es[1] + es[2])
    wts = [e * inv for e in es]
    low_half = lax.broadcasted_iota(jnp.int32, (tm, LANES), 1) < HEAD_DIM
    cols = []
    for pair in range(n_slabs):
        cs = slice(pair * LANES, (pair + 1) * LANES)
        outs = [o0_ref[0, :, cs], nat_o_ref[0, pair], nat_o_ref[1, pair]]
        acc = None
        for g in range(N_BRANCHES):
            wa = jnp.broadcast_to(wts[g][:, 2 * pair:2 * pair + 1], (tm, LANES))
            wb = jnp.broadcast_to(wts[g][:, 2 * pair + 1:2 * pair + 2], (tm, LANES))
            term = jnp.where(low_half, wa, wb) * outs[g]
            acc = term if acc is None else acc + term
        cols.append(acc.astype(BF16))
    o = jnp.concatenate(cols, axis=-1)
    mod = mod_ref[0]
    out_ref[0] = x_ref[0] + mod[2:3] * _dot(o, wo_ref[...])


def _combine(outs, lses, x, mod, w_o, tm=512):
    bsz, s, d_model = x.shape
    dils = [d for _, d in BRANCHES]
    assert dils[0] == 1
    res_major = lambda w: [pl.BlockSpec((None, d, tm // d, w), lambda b, i: (b, 0, i, 0)) for d in dils]
    return pl.pallas_call(
        functools.partial(_combine_kernel, tm=tm),
        grid=(bsz, s // tm),
        in_specs=res_major(D_ATTN) + res_major(LANES) + [
            pl.BlockSpec((1, tm, d_model), lambda b, i: (b, i, 0)),
            pl.BlockSpec((1, 6, d_model), lambda b, i: (b, 0, 0)),
            _resident((D_ATTN, d_model)),
        ],
        out_specs=pl.BlockSpec((1, tm, d_model), lambda b, i: (b, i, 0)),
        out_shape=jax.ShapeDtypeStruct(x.shape, F32),
        scratch_shapes=[pltpu.VMEM((N_BRANCHES - 1, D_ATTN // LANES, tm, LANES), F32),
                        pltpu.VMEM((N_BRANCHES - 1, tm, LANES), F32)],
        compiler_params=_params("arbitrary", "arbitrary"),
        name="branch_mix_out_proj",
    )(*outs, *lses, x, mod, w_o)


def kernel(x, c, ada_w, ada_b, norm1_g, norm2_g, pool_w_in, pool_w_grp, pool_scale, pool_w_out,
           kv_norm_g, kv_ada_w, kv_ada_b, w_kv, attn_w_q, attn_w_o,
           ffn_w_up, ffn_conv_w, ffn_conv_b, ffn_w_down, final_g):
    bsz, s, d = x.shape
    depth = ada_w.shape[0]
    n_pool = pool_w_in.shape[0]
    slopes = _alibi_slopes(N_BRANCHES * N_HEADS).reshape(N_BRANCHES, N_HEADS)

    mods = _ada(c, ada_w, ada_b).reshape(depth, bsz, 6, d)
    kv_mod = _ada(c, kv_ada_w[None], kv_ada_b[None]).reshape(bsz, 2, 1, d)

    kv = None
    for layer in range(depth):
        mod = mods[layer]
        if layer < n_pool:
            x = _pool_layer(x, mod, norm1_g[layer], pool_w_in[layer].astype(BF16),
                            pool_w_grp[layer].astype(BF16), pool_scale[layer],
                            pool_w_out[layer].astype(BF16))
        else:
            if layer == n_pool:
                kv = _proj(x, kv_mod[:, 0], kv_mod[:, 1], kv_norm_g, w_kv.astype(BF16))
                ks, vs = kv[:N_BRANCHES], kv[N_BRANCHES:]
            jl = layer - n_pool
            q = _proj(x, mod[:, 0:1], mod[:, 1:2], norm1_g[layer], attn_w_q[jl].astype(BF16),
                      out_scale=HEAD_DIM ** -0.5)
            outs, lses = zip(*[_attn_branch(q[g], ks[g], vs[g], g, slopes[g]) for g in range(N_BRANCHES)])
            x = _combine(outs, lses, x, mod, attn_w_o[jl].astype(BF16))
        x = _ffn_layer(x, mod, norm2_g[layer], ffn_w_up[layer].astype(BF16), ffn_conv_w[layer],
                       ffn_conv_b[layer], ffn_w_down[layer].astype(BF16), final_g,
                       final_norm=(layer == depth - 1))
    return x
```

```python
import functools
import math

import jax
import jax.numpy as jnp
import numpy as np
from jax import lax
from jax.experimental import pallas as pl
from jax.experimental.pallas import tpu as pltpu

D_MODEL = 1024
POOL_WINDOWS = (2, 4, 8, 16)
POOL_GROUP_DIM = D_MODEL // len(POOL_WINDOWS)
BRANCHES = ((128, 1), (512, 4), (2048, 16))
N_BRANCHES = len(BRANCHES)
HEAD_DIM = 64
N_HEADS = D_MODEL // HEAD_DIM
D_ATTN = N_HEADS * HEAD_DIM
ATTN_BLOCK = 128
D_FF = 2816
CONV_WIDTH = 3
EPS = 1e-6

LANES = 128
CARRY_ROWS = 8
VMEM_LIMIT = 56 * 1024 * 1024

BF16 = jnp.bfloat16
F32 = jnp.float32


def _alibi_slopes(n):
    def pow2(m):
        start = 2.0 ** (-(2.0 ** -(math.log2(m) - 3)))
        return [start ** (i + 1) for i in range(m)]
    if math.log2(n).is_integer():
        s = pow2(n)
    else:
        c = 2 ** math.floor(math.log2(n))
        s = pow2(c) + pow2(2 * c)[0::2][: n - c]
    s = np.asarray(s, dtype=np.float32)
    return -np.sort(-s)


def _params(*sem):
    return pltpu.CompilerParams(dimension_semantics=sem, vmem_limit_bytes=VMEM_LIMIT)


def _resident(shape):
    nd = len(shape)
    return pl.BlockSpec(shape, lambda *_: (0,) * nd, pipeline_mode=pl.Buffered(1))


def _norm_mod(x, g, shift, scale):
    y = x * lax.rsqrt(jnp.mean(x * x, axis=-1, keepdims=True) + EPS)
    return (y * g) * (1.0 + scale) + shift


def _dot(a, b):
    return jnp.dot(a, b, preferred_element_type=F32)


def _ada_kernel(c_ref, w_ref, b_ref, o_ref):
    c = c_ref[...]
    cond = c * (1.0 / (1.0 + jnp.exp(-c)))
    o_ref[0] = _dot(cond.astype(BF16), w_ref[0].astype(BF16)) + b_ref[0]


def _ada(c, w, b, tn=1024):
    n_layers, d, n = w.shape
    bsz = c.shape[0]
    return pl.pallas_call(
        _ada_kernel,
        grid=(n_layers, n // tn),
        in_specs=[
            pl.BlockSpec((bsz, d), lambda l, j: (0, 0)),
            pl.BlockSpec((1, d, tn), lambda l, j: (l, 0, j)),
            pl.BlockSpec((1, 1, tn), lambda l, j: (l, 0, j)),
        ],
        out_specs=pl.BlockSpec((1, bsz, tn), lambda l, j: (l, 0, j)),
        out_shape=jax.ShapeDtypeStruct((n_layers, bsz, n), F32),
        compiler_params=_params("arbitrary", "arbitrary"),
        name="ada_mod",
    )(c, w, b.reshape(n_layers, 1, n))


def _pool_kernel(x_ref, mod_ref, g_ref, win_ref, wgrp_ref, scale_ref, wout_ref, o_ref,
                 s1_ref, s2_ref, s4_ref, s8_ref, *, tm):
    si = pl.program_id(1)
    c0 = CARRY_ROWS
    gd = POOL_GROUP_DIM
    stages = (s1_ref, s2_ref, s4_ref, s8_ref)

    @pl.when(si == 0)
    def _():
        for ref in stages:
            ref[0:c0, :] = jnp.zeros((c0, ref.shape[1]), F32)

    x = x_ref[0]
    mod = mod_ref[0]
    h = _norm_mod(x, g_ref[...], mod[0:1], mod[1:2])
    u = _dot(h.astype(BF16), win_ref[...])
    s1_ref[c0:c0 + tm, :] = u
    for k in range(1, len(stages)):
        prev, cur = stages[k - 1], stages[k]
        sh = 1 << (k - 1)
        cur[c0:c0 + tm, :] = prev[c0:c0 + tm, gd:] + prev[c0 - sh:c0 - sh + tm, gd:]

    t = si * tm + lax.broadcasted_iota(jnp.int32, (tm, 1), 0)
    ys = []
    for g, w in enumerate(POOL_WINDOWS):
        ref = stages[g]
        sh = w // 2
        wsum = ref[c0:c0 + tm, 0:gd] + ref[c0 - sh:c0 - sh + tm, 0:gd]
        count = jnp.minimum(t + 1, w).astype(F32)
        pooled = wsum / count - u[:, g * gd:(g + 1) * gd]
        ys.append(_dot(pooled.astype(BF16), wgrp_ref[g]))
    y = jnp.concatenate(ys, axis=-1) * scale_ref[...]
    y = _dot(y.astype(BF16), wout_ref[...])
    o_ref[0] = x + mod[2:3] * y

    for ref in stages:
        ref[0:c0, :] = ref[tm:tm + c0, :]


def _pool_layer(x, mod, g, w_in, w_grp, scale, w_out, tm=512):
    bsz, s, d = x.shape
    gd = POOL_GROUP_DIM
    return pl.pallas_call(
        functools.partial(_pool_kernel, tm=tm),
        grid=(bsz, s // tm),
        in_specs=[
            pl.BlockSpec((1, tm, d), lambda b, i: (b, i, 0)),
            pl.BlockSpec((1, 6, d), lambda b, i: (b, 0, 0)),
            _resident((1, d)),
            _resident((d, d)),
            _resident((len(POOL_WINDOWS), gd, gd)),
            _resident((1, d)),
            _resident((d, d)),
        ],
        out_specs=pl.BlockSpec((1, tm, d), lambda b, i: (b, i, 0)),
        out_shape=jax.ShapeDtypeStruct(x.shape, F32),
        scratch_shapes=[pltpu.VMEM((tm + CARRY_ROWS, d - k * gd), F32) for k in range(4)],
        compiler_params=_params("arbitrary", "arbitrary"),
        name="pool_mixer",
    )(x, mod, g.reshape(1, d), w_in, w_grp, scale.reshape(1, d), w_out)


def _ffn_kernel(x_ref, mod_ref, g_ref, wup_ref, cw_ref, cb_ref, wdown_ref, fg_ref, o_ref,
                a_ref, *, tm, n_chunks, final_norm):
    si = pl.program_id(1)
    c0 = CARRY_ROWS
    f = D_FF
    tf = f // n_chunks

    @pl.when(si == 0)
    def _():
        a_ref[0:c0, :] = jnp.zeros((c0, f), F32)

    x = x_ref[0]
    mod = mod_ref[0]
    h = _norm_mod(x, g_ref[...], mod[3:4], mod[4:5]).astype(BF16)
    acc = None
    for j in range(n_chunks):
        lo = j * tf
        a = _dot(h, wup_ref[:, lo:lo + tf])
        v = _dot(h, wup_ref[:, f + lo:f + lo + tf])
        a_ref[c0:c0 + tm, lo:lo + tf] = a
        y = cb_ref[:, lo:lo + tf]
        y = y + a_ref[c0 - 2:c0 - 2 + tm, lo:lo + tf] * cw_ref[0:1, lo:lo + tf]
        y = y + a_ref[c0 - 1:c0 - 1 + tm, lo:lo + tf] * cw_ref[1:2, lo:lo + tf]
        y = y + a * cw_ref[2:3, lo:lo + tf]
        gated = y * (1.0 / (1.0 + jnp.exp(-y))) * v
        part = _dot(gated.astype(BF16), wdown_ref[lo:lo + tf, :])
        acc = part if acc is None else acc + part
    out = x + mod[5:6] * acc
    if final_norm:
        out = (out * lax.rsqrt(jnp.mean(out * out, axis=-1, keepdims=True) + EPS)) * fg_ref[...]
    o_ref[0] = out
    a_ref[0:c0, :] = a_ref[tm:tm + c0, :]


def _ffn_layer(x, mod, g, w_up, conv_w, conv_b, w_down, final_g, final_norm, tm=512, n_chunks=2):
    bsz, s, d = x.shape
    f = D_FF
    return pl.pallas_call(
        functools.partial(_ffn_kernel, tm=tm, n_chunks=n_chunks, final_norm=final_norm),
        grid=(bsz, s // tm),
        in_specs=[
            pl.BlockSpec((1, tm, d), lambda b, i: (b, i, 0)),
            pl.BlockSpec((1, 6, d), lambda b, i: (b, 0, 0)),
            _resident((1, d)),
            _resident((d, 2 * f)),
            _resident((CONV_WIDTH, f)),
            _resident((1, f)),
            _resident((f, d)),
            _resident((1, d)),
        ],
        out_specs=pl.BlockSpec((1, tm, d), lambda b, i: (b, i, 0)),
        out_shape=jax.ShapeDtypeStruct(x.shape, F32),
        scratch_shapes=[pltpu.VMEM((tm + CARRY_ROWS, f), F32)],
        compiler_params=_params("arbitrary", "arbitrary"),
        name="conv_ffn",
    )(x, mod, g.reshape(1, d), w_up, conv_w, conv_b.reshape(1, f), w_down, final_g.reshape(1, d))


def _proj_kernel(x_ref, shift_ref, scale_ref, g_ref, w_ref, *rest, tm, n_tensors, out_scale):
    o_refs, h_ref = rest[:-1], rest[-1]
    h = _norm_mod(x_ref[0], g_ref[...], shift_ref[0], scale_ref[0])
    n_slabs = D_MODEL // LANES
    for c in range(n_slabs):
        h_ref[c] = h[:, c * LANES:(c + 1) * LANES]
    for g, (_, d) in enumerate(BRANCHES):
        n = tm // d
        if d == 1:
            hp = h.astype(BF16)
        else:
            hp = jnp.concatenate(
                [jnp.concatenate([h_ref[c, pl.ds(r, n, stride=d), :] for c in range(n_slabs)], axis=1)
                 for r in range(d)], axis=0).astype(BF16)
        for t in range(n_tensors):
            col = (t * N_BRANCHES + g) * D_ATTN
            res = _dot(hp, w_ref[:, col:col + D_ATTN])
            if out_scale != 1.0:
                res = res * out_scale
            o_ref = o_refs[t * N_BRANCHES + g]
            for r in range(d):
                o_ref[0, r] = res[r * n:(r + 1) * n].astype(BF16)


def _proj(x, shift, scale, g, w, out_scale=1.0, tm=512):
    bsz, s, d_model = x.shape
    n_tensors = w.shape[1] // (N_BRANCHES * D_ATTN)
    dils = [d for _, d in BRANCHES] * n_tensors
    return pl.pallas_call(
        functools.partial(_proj_kernel, tm=tm, n_tensors=n_tensors, out_scale=out_scale),
        grid=(bsz, s // tm),
        in_specs=[
            pl.BlockSpec((1, tm, d_model), lambda b, i: (b, i, 0)),
            pl.BlockSpec((1, 1, d_model), lambda b, i: (b, 0, 0)),
            pl.BlockSpec((1, 1, d_model), lambda b, i: (b, 0, 0)),
            _resident((1, d_model)),
            _resident(w.shape),
        ],
        out_specs=[pl.BlockSpec((1, d, tm // d, D_ATTN), lambda b, i: (b, 0, i, 0)) for d in dils],
        out_shape=[jax.ShapeDtypeStruct((bsz, d, s // d, D_ATTN), BF16) for d in dils],
        scratch_shapes=[pltpu.VMEM((d_model // LANES, tm, LANES), F32)],
        compiler_params=_params("arbitrary", "arbitrary"),
        name="norm_proj",
    )(x, shift, scale, g.reshape(1, d_model), w)


def _bias_table(slopes, dilation, n_steps):
    blk = ATTN_BLOCK
    row = np.arange(blk)[:, None]
    col = np.arange(2 * blk)[None, :]
    delta = row + blk - col
    valid = (delta >= 0) & (delta <= n_steps)
    dist = (delta * dilation).astype(np.float32)
    bias = -np.asarray(slopes, np.float32)[:, None, None] * dist[None]
    later = np.where(valid[None], bias, -np.inf).astype(np.float32)
    first = np.where((valid & (col >= blk))[None], bias, -np.inf).astype(np.float32)
    return np.stack([first, later])


def _ones_table():
    blk = ATTN_BLOCK
    t = np.zeros((N_HEADS // 2, 4 * blk, LANES), np.float32)
    for p in range(N_HEADS // 2):
        t[p, :2 * blk, N_HEADS + 2 * p] = 1.0
        t[p, 2 * blk:, N_HEADS + 2 * p + 1] = 1.0
    return t


def _attn_kernel(q_ref, kp_ref, kc_ref, vp_ref, vc_ref, bias_ref, ones_ref, o_ref, st_ref, s_scr, p_scr):
    blk = ATTN_BLOCK
    n_pairs = N_HEADS // 2
    first = jnp.minimum(pl.program_id(2), 1)
    lane = lax.broadcasted_iota(jnp.int32, (blk, LANES), 1)
    low_half = lane < HEAD_DIM
    low_half2 = lax.broadcasted_iota(jnp.int32, (2 * blk, LANES), 1) < HEAD_DIM

    ms = []
    for pair in range(n_pairs):
        cs = slice(pair * LANES, (pair + 1) * LANES)
        q = q_ref[:, cs]
        zq = jnp.zeros_like(q)
        q2 = jnp.concatenate([jnp.where(low_half, q, zq), jnp.where(low_half, zq, q)], axis=0)
        k = jnp.concatenate([kp_ref[:, cs], kc_ref[:, cs]], axis=0)
        s2 = lax.dot_general(q2, k, (((1,), (1,)), ((), ())), preferred_element_type=F32)
        for half in range(2):
            head = 2 * pair + half
            s = s2[half * blk:(half + 1) * blk] + bias_ref[first, head]
            s_scr[head] = s
            ms.append(jnp.max(s, axis=-1, keepdims=True))
    for head in range(N_HEADS):
        p_scr[head] = jnp.exp(s_scr[head] - ms[head]).astype(BF16)
    m_mat = jnp.zeros((blk, LANES), F32)
    l_mat = jnp.zeros((blk, LANES), F32)
    for pair in range(n_pairs):
        cs = slice(pair * LANES, (pair + 1) * LANES)
        v = jnp.concatenate([vp_ref[:, cs], vc_ref[:, cs]], axis=0)
        zv = jnp.zeros_like(v)
        v2 = jnp.concatenate([jnp.where(low_half2, v, zv), jnp.where(low_half2, zv, v)], axis=0)
        v2 = jnp.concatenate([v2, ones_ref[pair]], axis=1)
        p2 = jnp.concatenate([p_scr[2 * pair], p_scr[2 * pair + 1]], axis=1)
        u = _dot(p2, v2)
        o_ref[:, cs] = u[:, :LANES]
        l_mat = l_mat + u[:, LANES:]
        m_mat = jnp.where(lane == 2 * pair, ms[2 * pair], m_mat)
        m_mat = jnp.where(lane == 2 * pair + 1, ms[2 * pair + 1], m_mat)
    st_ref[...] = m_mat + l_mat


def _attn_branch(q, k, v, branch, slopes):
    window, d = BRANCHES[branch]
    bsz, _, sub, _ = q.shape
    n_steps = window // d
    blk = ATTN_BLOCK
    assert n_steps <= blk and sub % blk == 0 and 2 * N_HEADS <= LANES
    bias = jnp.asarray(_bias_table(slopes, d, n_steps))
    ones = jnp.asarray(_ones_table(), dtype=BF16)
    cur = pl.BlockSpec((None, None, blk, D_ATTN), lambda b, r, j: (b, r, j, 0))
    prev = pl.BlockSpec((None, None, blk, D_ATTN), lambda b, r, j: (b, r, jnp.maximum(j - 1, 0), 0))
    return pl.pallas_call(
        _attn_kernel,
        grid=(bsz, d, sub // blk),
        in_specs=[cur, prev, cur, prev, cur, _resident(bias.shape), _resident(ones.shape)],
        out_specs=[cur, pl.BlockSpec((None, None, blk, LANES), lambda b, r, j: (b, r, j, 0))],
        out_shape=[
            jax.ShapeDtypeStruct((bsz, d, sub, D_ATTN), F32),
            jax.ShapeDtypeStruct((bsz, d, sub, LANES), F32),
        ],
        scratch_shapes=[pltpu.VMEM((N_HEADS, blk, 2 * blk), F32),
                        pltpu.VMEM((N_HEADS, blk, 2 * blk), BF16)],
        compiler_params=_params("arbitrary", "arbitrary", "arbitrary"),
        name=f"dilated_attn_{branch}",
    )(q, k, k, v, v, bias, ones)


def _combine_kernel(o0_ref, o1_ref, o2_ref, l0_ref, l1_ref, l2_ref, x_ref, mod_ref, wo_ref, out_ref,
                    nat_o_ref, nat_l_ref, *, tm):
    o_refs = (o0_ref, o1_ref, o2_ref)
    l_refs = (l0_ref, l1_ref, l2_ref)
    n_slabs = D_ATTN // LANES
    for g, (_, d) in enumerate(BRANCHES):
        if d == 1:
            continue
        n = tm // d
        for r in range(d):
            nat_l_ref[g - 1, pl.ds(r, n, stride=d), :] = l_refs[g][r]
            for c in range(n_slabs):
                nat_o_ref[g - 1, c, pl.ds(r, n, stride=d), :] = o_refs[g][r, :, c * LANES:(c + 1) * LANES]
    stats = [l0_ref[0], nat_l_ref[0], nat_l_ref[1]]
    m = jnp.maximum(jnp.maximum(stats[0], stats[1]), stats[2])
    es = [jnp.exp(st - m) for st in stats]
    sums = [pltpu.roll(st, LANES - N_HEADS, axis=1) for st in stats]
    inv = 1.0 / (es[0] * sums[0] + es[1] * sums[1] + es[2] * sums[2])
    wts = [e * inv for e in es]
    low_half = lax.broadcasted_iota(jnp.int32, (tm, LANES), 1) < HEAD_DIM
    cols = []
    for pair in range(n_slabs):
        cs = slice(pair * LANES, (pair + 1) * LANES)
        outs = [o0_ref[0, :, cs], nat_o_ref[0, pair], nat_o_ref[1, pair]]
        acc = None
        for g in range(N_BRANCHES):
            wa = jnp.broadcast_to(wts[g][:, 2 * pair:2 * pair + 1], (tm, LANES))
            wb = jnp.broadcast_to(wts[g][:, 2 * pair + 1:2 * pair + 2], (tm, LANES))
            term = jnp.where(low_half, wa, wb) * outs[g]
            acc = term if acc is None else acc + term
        cols.append(acc.astype(BF16))
    o = jnp.concatenate(cols, axis=-1)
    mod = mod_ref[0]
    out_ref[0] = x_ref[0] + mod[2:3] * _dot(o, wo_ref[...])


def _combine(outs, lses, x, mod, w_o, tm=512):
    bsz, s, d_model = x.shape
    dils = [d for _, d in BRANCHES]
    assert dils[0] == 1
    res_major = lambda w: [pl.BlockSpec((None, d, tm // d, w), lambda b, i: (b, 0, i, 0)) for d in dils]
    return pl.pallas_call(
        functools.partial(_combine_kernel, tm=tm),
        grid=(bsz, s // tm),
        in_specs=res_major(D_ATTN) + res_major(LANES) + [
            pl.BlockSpec((1, tm, d_model), lambda b, i: (b, i, 0)),
            pl.BlockSpec((1, 6, d_model), lambda b, i: (b, 0, 0)),
            _resident((D_ATTN, d_model)),
        ],
        out_specs=pl.BlockSpec((1, tm, d_model), lambda b, i: (b, i, 0)),
        out_shape=jax.ShapeDtypeStruct(x.shape, F32),
        scratch_shapes=[pltpu.VMEM((N_BRANCHES - 1, D_ATTN // LANES, tm, LANES), F32),
                        pltpu.VMEM((N_BRANCHES - 1, tm, LANES), F32)],
        compiler_params=_params("arbitrary", "arbitrary"),
        name="branch_mix_out_proj",
    )(*outs, *lses, x, mod, w_o)


def kernel(x, c, ada_w, ada_b, norm1_g, norm2_g, pool_w_in, pool_w_grp, pool_scale, pool_w_out,
           kv_norm_g, kv_ada_w, kv_ada_b, w_kv, attn_w_q, attn_w_o,
           ffn_w_up, ffn_conv_w, ffn_conv_b, ffn_w_down, final_g):
    bsz, s, d = x.shape
    depth = ada_w.shape[0]
    n_pool = pool_w_in.shape[0]
    slopes = _alibi_slopes(N_BRANCHES * N_HEADS).reshape(N_BRANCHES, N_HEADS)

    mods = _ada(c, ada_w, ada_b).reshape(depth, bsz, 6, d)
    kv_mod = _ada(c, kv_ada_w[None], kv_ada_b[None]).reshape(bsz, 2, 1, d)

    kv = None
    for layer in range(depth):
        mod = mods[layer]
        if layer < n_pool:
            x = _pool_layer(x, mod, norm1_g[layer], pool_w_in[layer].astype(BF16),
                            pool_w_grp[layer].astype(BF16), pool_scale[layer],
                            pool_w_out[layer].astype(BF16))
        else:
            if layer == n_pool:
                kv = _proj(x, kv_mod[:, 0], kv_mod[:, 1], kv_norm_g, w_kv.astype(BF16))
                ks, vs = kv[:N_BRANCHES], kv[N_BRANCHES:]
            jl = layer - n_pool
            q = _proj(x, mod[:, 0:1], mod[:, 1:2], norm1_g[layer], attn_w_q[jl].astype(BF16),
                      out_scale=HEAD_DIM ** -0.5)
            outs, lses = zip(*[_attn_branch(q[g], ks[g], vs[g], g, slopes[g]) for g in range(N_BRANCHES)])
            x = _combine(outs, lses, x, mod, attn_w_o[jl].astype(BF16))
        x = _ffn_layer(x, mod, norm2_g[layer], ffn_w_up[layer].astype(BF16), ffn_conv_w[layer],
                       ffn_conv_b[layer], ffn_w_down[layer].astype(BF16), final_g,
                       final_norm=(layer == depth - 1))
    return x
```

```python
import functools
import math

import jax
import jax.numpy as jnp
import numpy as np
from jax import lax
from jax.experimental import pallas as pl
from jax.experimental.pallas import tpu as pltpu

D_MODEL = 1024
POOL_WINDOWS = (2, 4, 8, 16)
POOL_GROUP_DIM = D_MODEL // len(POOL_WINDOWS)
BRANCHES = ((128, 1), (512, 4), (2048, 16))
N_BRANCHES = len(BRANCHES)
HEAD_DIM = 64
N_HEADS = D_MODEL // HEAD_DIM
D_ATTN = N_HEADS * HEAD_DIM
ATTN_BLOCK = 128
D_FF = 2816
CONV_WIDTH = 3
EPS = 1e-6

LANES = 128
CARRY_ROWS = 8
VMEM_LIMIT = 56 * 1024 * 1024

BF16 = jnp.bfloat16
F32 = jnp.float32


def _alibi_slopes(n):
    def pow2(m):
        start = 2.0 ** (-(2.0 ** -(math.log2(m) - 3)))
        return [start ** (i + 1) for i in range(m)]
    if math.log2(n).is_integer():
        s = pow2(n)
    else:
        c = 2 ** math.floor(math.log2(n))
        s = pow2(c) + pow2(2 * c)[0::2][: n - c]
    s = np.asarray(s, dtype=np.float32)
    return -np.sort(-s)


def _params(*sem):
    return pltpu.CompilerParams(dimension_semantics=sem, vmem_limit_bytes=VMEM_LIMIT)


def _resident(shape):
    nd = len(shape)
    return pl.BlockSpec(shape, lambda *_: (0,) * nd, pipeline_mode=pl.Buffered(1))


def _norm_mod(x, g, shift, scale):
    y = x * lax.rsqrt(jnp.mean(x * x, axis=-1, keepdims=True) + EPS)
    return (y * g) * (1.0 + scale) + shift


def _dot(a, b):
    return jnp.dot(a, b, preferred_element_type=F32)


def _ada_kernel(c_ref, w_ref, b_ref, o_ref):
    c = c_ref[...]
    cond = c * (1.0 / (1.0 + jnp.exp(-c)))
    o_ref[0] = _dot(cond.astype(BF16), w_ref[0].astype(BF16)) + b_ref[0]


def _ada(c, w, b, tn=1024):
    n_layers, d, n = w.shape
    bsz = c.shape[0]
    return pl.pallas_call(
        _ada_kernel,
        grid=(n_layers, n // tn),
        in_specs=[
            pl.BlockSpec((bsz, d), lambda l, j: (0, 0)),
            pl.BlockSpec((1, d, tn), lambda l, j: (l, 0, j)),
            pl.BlockSpec((1, 1, tn), lambda l, j: (l, 0, j)),
        ],
        out_specs=pl.BlockSpec((1, bsz, tn), lambda l, j: (l, 0, j)),
        out_shape=jax.ShapeDtypeStruct((n_layers, bsz, n), F32),
        compiler_params=_params("arbitrary", "arbitrary"),
        name="ada_mod",
    )(c, w, b.reshape(n_layers, 1, n))


def _pool_kernel(x_ref, mod_ref, g_ref, win_ref, wgrp_ref, scale_ref, wout_ref, o_ref,
                 s1_ref, s2_ref, s4_ref, s8_ref, *, tm):
    si = pl.program_id(1)
    c0 = CARRY_ROWS
    gd = POOL_GROUP_DIM
    stages = (s1_ref, s2_ref, s4_ref, s8_ref)

    @pl.when(si == 0)
    def _():
        for ref in stages:
            ref[0:c0, :] = jnp.zeros((c0, ref.shape[1]), F32)

    x = x_ref[0]
    mod = mod_ref[0]
    h = _norm_mod(x, g_ref[...], mod[0:1], mod[1:2])
    u = _dot(h.astype(BF16), win_ref[...])
    s1_ref[c0:c0 + tm, :] = u
    for k in range(1, len(stages)):
        prev, cur = stages[k - 1], stages[k]
        sh = 1 << (k - 1)
        cur[c0:c0 + tm, :] = prev[c0:c0 + tm, gd:] + prev[c0 - sh:c0 - sh + tm, gd:]

    t = si * tm + lax.broadcasted_iota(jnp.int32, (tm, 1), 0)
    ys = []
    for g, w in enumerate(POOL_WINDOWS):
        ref = stages[g]
        sh = w // 2
        wsum = ref[c0:c0 + tm, 0:gd] + ref[c0 - sh:c0 - sh + tm, 0:gd]
        count = jnp.minimum(t + 1, w).astype(F32)
        pooled = wsum / count - u[:, g * gd:(g + 1) * gd]
        ys.append(_dot(pooled.astype(BF16), wgrp_ref[g]))
    y = jnp.concatenate(ys, axis=-1) * scale_ref[...]
    y = _dot(y.astype(BF16), wout_ref[...])
    o_ref[0] = x + mod[2:3] * y

    for ref in stages:
        ref[0:c0, :] = ref[tm:tm + c0, :]


def _pool_layer(x, mod, g, w_in, w_grp, scale, w_out, tm=512):
    bsz, s, d = x.shape
    gd = POOL_GROUP_DIM
    return pl.pallas_call(
        functools.partial(_pool_kernel, tm=tm),
        grid=(bsz, s // tm),
        in_specs=[
            pl.BlockSpec((1, tm, d), lambda b, i: (b, i, 0)),
            pl.BlockSpec((1, 6, d), lambda b, i: (b, 0, 0)),
            _resident((1, d)),
            _resident((d, d)),
            _resident((len(POOL_WINDOWS), gd, gd)),
            _resident((1, d)),
            _resident((d, d)),
        ],
        out_specs=pl.BlockSpec((1, tm, d), lambda b, i: (b, i, 0)),
        out_shape=jax.ShapeDtypeStruct(x.shape, F32),
        scratch_shapes=[pltpu.VMEM((tm + CARRY_ROWS, d - k * gd), F32) for k in range(4)],
        compiler_params=_params("arbitrary", "arbitrary"),
        name="pool_mixer",
    )(x, mod, g.reshape(1, d), w_in, w_grp, scale.reshape(1, d), w_out)


def _ffn_kernel(x_ref, mod_ref, g_ref, wup_ref, cw_ref, cb_ref, wdown_ref, fg_ref, o_ref,
                a_ref, *, tm, final_norm):
    si = pl.program_id(1)
    c0 = CARRY_ROWS
    f = D_FF

    @pl.when(si == 0)
    def _():
        a_ref[0:c0, :] = jnp.zeros((c0, f), F32)

    x = x_ref[0]
    mod = mod_ref[0]
    h = _norm_mod(x, g_ref[...], mod[3:4], mod[4:5]).astype(BF16)
    a = _dot(h, wup_ref[:, 0:f])
    v = _dot(h, wup_ref[:, f:2 * f])
    a_ref[c0:c0 + tm, :] = a
    y = cb_ref[...]
    y = y + a_ref[c0 - 2:c0 - 2 + tm, :] * cw_ref[0:1, :]
    y = y + a_ref[c0 - 1:c0 - 1 + tm, :] * cw_ref[1:2, :]
    y = y + a * cw_ref[2:3, :]
    gated = y * (1.0 / (1.0 + jnp.exp(-y))) * v
    out = x + mod[5:6] * _dot(gated.astype(BF16), wdown_ref[...])
    if final_norm:
        out = (out * lax.rsqrt(jnp.mean(out * out, axis=-1, keepdims=True) + EPS)) * fg_ref[...]
    o_ref[0] = out
    a_ref[0:c0, :] = a_ref[tm:tm + c0, :]


def _ffn_layer(x, mod, g, w_up, conv_w, conv_b, w_down, final_g, final_norm, tm=512):
    bsz, s, d = x.shape
    f = D_FF
    return pl.pallas_call(
        functools.partial(_ffn_kernel, tm=tm, final_norm=final_norm),
        grid=(bsz, s // tm),
        in_specs=[
            pl.BlockSpec((1, tm, d), lambda b, i: (b, i, 0)),
            pl.BlockSpec((1, 6, d), lambda b, i: (b, 0, 0)),
            _resident((1, d)),
            _resident((d, 2 * f)),
            _resident((CONV_WIDTH, f)),
            _resident((1, f)),
            _resident((f, d)),
            _resident((1, d)),
        ],
        out_specs=pl.BlockSpec((1, tm, d), lambda b, i: (b, i, 0)),
        out_shape=jax.ShapeDtypeStruct(x.shape, F32),
        scratch_shapes=[pltpu.VMEM((tm + CARRY_ROWS, f), F32)],
        compiler_params=_params("arbitrary", "arbitrary"),
        name="conv_ffn",
    )(x, mod, g.reshape(1, d), w_up, conv_w, conv_b.reshape(1, f), w_down, final_g.reshape(1, d))


def _proj_kernel(x_ref, shift_ref, scale_ref, g_ref, w_ref, *rest, tm, n_tensors, out_scale):
    o_refs, h_ref = rest[:-1], rest[-1]
    h = _norm_mod(x_ref[0], g_ref[...], shift_ref[0], scale_ref[0])
    n_slabs = D_MODEL // LANES
    for c in range(n_slabs):
        h_ref[c] = h[:, c * LANES:(c + 1) * LANES]
    for g, (_, d) in enumerate(BRANCHES):
        n = tm // d
        if d == 1:
            hp = h.astype(BF16)
        else:
            hp = jnp.concatenate(
                [jnp.concatenate([h_ref[c, pl.ds(r, n, stride=d), :] for c in range(n_slabs)], axis=1)
                 for r in range(d)], axis=0).astype(BF16)
        for t in range(n_tensors):
            col = (t * N_BRANCHES + g) * D_ATTN
            res = _dot(hp, w_ref[:, col:col + D_ATTN])
            if out_scale != 1.0:
                res = res * out_scale
            o_ref = o_refs[t * N_BRANCHES + g]
            for r in range(d):
                o_ref[0, r] = res[r * n:(r + 1) * n].astype(BF16)


def _proj(x, shift, scale, g, w, out_scale=1.0, tm=512):
    bsz, s, d_model = x.shape
    n_tensors = w.shape[1] // (N_BRANCHES * D_ATTN)
    dils = [d for _, d in BRANCHES] * n_tensors
    return pl.pallas_call(
        functools.partial(_proj_kernel, tm=tm, n_tensors=n_tensors, out_scale=out_scale),
        grid=(bsz, s // tm),
        in_specs=[
            pl.BlockSpec((1, tm, d_model), lambda b, i: (b, i, 0)),
            pl.BlockSpec((1, 1, d_model), lambda b, i: (b, 0, 0)),
            pl.BlockSpec((1, 1, d_model), lambda b, i: (b, 0, 0)),
            _resident((1, d_model)),
            _resident(w.shape),
        ],
        out_specs=[pl.BlockSpec((1, d, tm // d, D_ATTN), lambda b, i: (b, 0, i, 0)) for d in dils],
        out_shape=[jax.ShapeDtypeStruct((bsz, d, s // d, D_ATTN), BF16) for d in dils],
        scratch_shapes=[pltpu.VMEM((d_model // LANES, tm, LANES), F32)],
        compiler_params=_params("arbitrary", "arbitrary"),
        name="norm_proj",
    )(x, shift, scale, g.reshape(1, d_model), w)


def _bias_table(slopes, dilation, n_steps):
    blk = ATTN_BLOCK
    row = np.arange(blk)[:, None]
    col = np.arange(2 * blk)[None, :]
    delta = row + blk - col
    valid = (delta >= 0) & (delta <= n_steps)
    dist = (delta * dilation).astype(np.float32)
    bias = -np.asarray(slopes, np.float32)[:, None, None] * dist[None]
    later = np.where(valid[None], bias, -np.inf).astype(np.float32)
    first = np.where((valid & (col >= blk))[None], bias, -np.inf).astype(np.float32)
    return np.stack([first, later])


def _ones_table():
    blk = ATTN_BLOCK
    t = np.zeros((N_HEADS // 2, 4 * blk, LANES), np.float32)
    for p in range(N_HEADS // 2):
        t[p, :2 * blk, N_HEADS + 2 * p] = 1.0
        t[p, 2 * blk:, N_HEADS + 2 * p + 1] = 1.0
    return t


def _attn_kernel(q_ref, kp_ref, kc_ref, vp_ref, vc_ref, bias_ref, ones_ref, o_ref, st_ref, s_scr, p_scr,
                 *, n_sub, group):
    blk = ATTN_BLOCK
    first = jnp.minimum(pl.program_id(2), 1)
    lane = lax.broadcasted_iota(jnp.int32, (blk, LANES), 1)
    low_half = lane < HEAD_DIM
    low_half2 = lax.broadcasted_iota(jnp.int32, (2 * blk, LANES), 1) < HEAD_DIM

    for sb in range(n_sub):
        rows = slice(sb * blk, (sb + 1) * blk)

        def keys(prev_ref, cur_ref, cs):
            if sb == 0:
                return jnp.concatenate([prev_ref[:, cs], cur_ref[0:blk, cs]], axis=0)
            return cur_ref[(sb - 1) * blk:(sb + 1) * blk, cs]

        variant = first if sb == 0 else 1
        m_mat = jnp.zeros((blk, LANES), F32)
        l_mat = jnp.zeros((blk, LANES), F32)
        for g0 in range(0, N_HEADS, group):
            ms = {}
            for pair in range(g0 // 2, (g0 + group) // 2):
                cs = slice(pair * LANES, (pair + 1) * LANES)
                q = q_ref[rows, cs]
                zq = jnp.zeros_like(q)
                q2 = jnp.concatenate([jnp.where(low_half, q, zq), jnp.where(low_half, zq, q)], axis=0)
                s2 = lax.dot_general(q2, keys(kp_ref, kc_ref, cs), (((1,), (1,)), ((), ())),
                                     preferred_element_type=F32)
                for half in range(2):
                    head = 2 * pair + half
                    s = s2[half * blk:(half + 1) * blk] + bias_ref[variant, head]
                    s_scr[sb, head] = s
                    ms[head] = jnp.max(s, axis=-1, keepdims=True)
            for head in range(g0, g0 + group):
                p_scr[sb, head] = jnp.exp(s_scr[sb, head] - ms[head]).astype(BF16)
            for pair in range(g0 // 2, (g0 + group) // 2):
                cs = slice(pair * LANES, (pair + 1) * LANES)
                v = keys(vp_ref, vc_ref, cs)
                zv = jnp.zeros_like(v)
                v2 = jnp.concatenate([jnp.where(low_half2, v, zv), jnp.where(low_half2, zv, v)], axis=0)
                v2 = jnp.concatenate([v2, ones_ref[pair]], axis=1)
                p2 = jnp.concatenate([p_scr[sb, 2 * pair], p_scr[sb, 2 * pair + 1]], axis=1)
                u = _dot(p2, v2)
                o_ref[rows, cs] = u[:, :LANES]
                l_mat = l_mat + u[:, LANES:]
                m_mat = jnp.where(lane == 2 * pair, ms[2 * pair], m_mat)
                m_mat = jnp.where(lane == 2 * pair + 1, ms[2 * pair + 1], m_mat)
        st_ref[rows, :] = m_mat + l_mat


def _attn_branch(q, k, v, branch, slopes, n_sub=2, group=8):
    window, d = BRANCHES[branch]
    bsz, _, sub, _ = q.shape
    n_steps = window // d
    blk = ATTN_BLOCK
    qb = n_sub * blk
    assert n_steps <= blk and sub % qb == 0 and 2 * N_HEADS <= LANES
    bias = jnp.asarray(_bias_table(slopes, d, n_steps))
    ones = jnp.asarray(_ones_table(), dtype=BF16)
    cur = pl.BlockSpec((None, None, qb, D_ATTN), lambda b, r, j: (b, r, j, 0))
    prev = pl.BlockSpec((None, None, blk, D_ATTN), lambda b, r, j: (b, r, jnp.maximum(n_sub * j - 1, 0), 0))
    return pl.pallas_call(
        functools.partial(_attn_kernel, n_sub=n_sub, group=group),
        grid=(bsz, d, sub // qb),
        in_specs=[cur, prev, cur, prev, cur, _resident(bias.shape), _resident(ones.shape)],
        out_specs=[cur, pl.BlockSpec((None, None, qb, LANES), lambda b, r, j: (b, r, j, 0))],
        out_shape=[
            jax.ShapeDtypeStruct((bsz, d, sub, D_ATTN), F32),
            jax.ShapeDtypeStruct((bsz, d, sub, LANES), F32),
        ],
        scratch_shapes=[pltpu.VMEM((n_sub, N_HEADS, blk, 2 * blk), F32),
                        pltpu.VMEM((n_sub, N_HEADS, blk, 2 * blk), BF16)],
        compiler_params=_params("arbitrary", "arbitrary", "arbitrary"),
        name=f"dilated_attn_{branch}",
    )(q, k, k, v, v, bias, ones)


def _combine_kernel(o0_ref, o1_ref, o2_ref, l0_ref, l1_ref, l2_ref, x_ref, mod_ref, wo_ref, expand_ref,
                    out_ref, nat_o_ref, nat_l_ref, *, tm):
    o_refs = (o0_ref, o1_ref, o2_ref)
    l_refs = (l0_ref, l1_ref, l2_ref)
    n_slabs = D_ATTN // LANES
    for g, (_, d) in enumerate(BRANCHES):
        if d == 1:
            continue
        n = tm // d
        for r in range(d):
            nat_l_ref[g - 1, pl.ds(r, n, stride=d), :] = l_refs[g][r]
            for c in range(n_slabs):
                nat_o_ref[g - 1, c, pl.ds(r, n, stride=d), :] = o_refs[g][r, :, c * LANES:(c + 1) * LANES]
    stats = [l0_ref[0], nat_l_ref[0], nat_l_ref[1]]
    m = jnp.maximum(jnp.maximum(stats[0], stats[1]), stats[2])
    es = [jnp.exp(st - m) for st in stats]
    sums = [pltpu.roll(st, LANES - N_HEADS, axis=1) for st in stats]
    inv = 1.0 / (es[0] * sums[0] + es[1] * sums[1] + es[2] * sums[2])
    wts = [e * inv for e in es]
    head_lanes = lax.broadcasted_iota(jnp.int32, (tm, LANES), 1) < N_HEADS
    splits = []
    for w in wts:
        w = jnp.where(head_lanes, w, 0.0)
        hi = w.astype(BF16)
        lo = (w - hi.astype(F32)).astype(BF16)
        splits.append(jnp.concatenate([hi, lo], axis=1))
    cols = []
    wide = 2 * LANES
    for c in range(D_ATTN // wide):
        cs = slice(c * wide, (c + 1) * wide)
        outs = [o0_ref[0, :, cs]] + [
            jnp.concatenate([nat_o_ref[g, 2 * c], nat_o_ref[g, 2 * c + 1]], axis=1) for g in range(2)]
        acc = None
        for g in range(N_BRANCHES):
            term = _dot(splits[g], expand_ref[:, cs]) * outs[g]
            acc = term if acc is None else acc + term
        cols.append(acc.astype(BF16))
    o = jnp.concatenate(cols, axis=-1)
    mod = mod_ref[0]
    out_ref[0] = x_ref[0] + mod[2:3] * _dot(o, wo_ref[...])


def _combine(outs, lses, x, mod, w_o, tm=512):
    bsz, s, d_model = x.shape
    dils = [d for _, d in BRANCHES]
    assert dils[0] == 1
    res_major = lambda w: [pl.BlockSpec((None, d, tm // d, w), lambda b, i: (b, 0, i, 0)) for d in dils]
    head_rows = np.zeros((LANES, D_ATTN), np.float32)
    head_rows[np.arange(D_ATTN) // HEAD_DIM, np.arange(D_ATTN)] = 1.0
    expand = jnp.asarray(np.concatenate([head_rows, head_rows]), dtype=BF16)
    return pl.pallas_call(
        functools.partial(_combine_kernel, tm=tm),
        grid=(bsz, s // tm),
        in_specs=res_major(D_ATTN) + res_major(LANES) + [
            pl.BlockSpec((1, tm, d_model), lambda b, i: (b, i, 0)),
            pl.BlockSpec((1, 6, d_model), lambda b, i: (b, 0, 0)),
            _resident((D_ATTN, d_model)),
            _resident(expand.shape),
        ],
        out_specs=pl.BlockSpec((1, tm, d_model), lambda b, i: (b, i, 0)),
        out_shape=jax.ShapeDtypeStruct(x.shape, F32),
        scratch_shapes=[pltpu.VMEM((N_BRANCHES - 1, D_ATTN // LANES, tm, LANES), F32),
                        pltpu.VMEM((N_BRANCHES - 1, tm, LANES), F32)],
        compiler_params=_params("arbitrary", "arbitrary"),
        name="branch_mix_out_proj",
    )(*outs, *lses, x, mod, w_o, expand)


def kernel(x, c, ada_w, ada_b, norm1_g, norm2_g, pool_w_in, pool_w_grp, pool_scale, pool_w_out,
           kv_norm_g, kv_ada_w, kv_ada_b, w_kv, attn_w_q, attn_w_o,
           ffn_w_up, ffn_conv_w, ffn_conv_b, ffn_w_down, final_g):
    bsz, s, d = x.shape
    depth = ada_w.shape[0]
    n_pool = pool_w_in.shape[0]
    slopes = _alibi_slopes(N_BRANCHES * N_HEADS).reshape(N_BRANCHES, N_HEADS)

    mods = _ada(c, ada_w, ada_b).reshape(depth, bsz, 6, d)
    kv_mod = _ada(c, kv_ada_w[None], kv_ada_b[None]).reshape(bsz, 2, 1, d)

    kv = None
    for layer in range(depth):
        mod = mods[layer]
        if layer < n_pool:
            x = _pool_layer(x, mod, norm1_g[layer], pool_w_in[layer].astype(BF16),
                            pool_w_grp[layer].astype(BF16), pool_scale[layer],
                            pool_w_out[layer].astype(BF16))
        else:
            if layer == n_pool:
                kv = _proj(x, kv_mod[:, 0], kv_mod[:, 1], kv_norm_g, w_kv.astype(BF16))
                ks, vs = kv[:N_BRANCHES], kv[N_BRANCHES:]
            jl = layer - n_pool
            q = _proj(x, mod[:, 0:1], mod[:, 1:2], norm1_g[layer], attn_w_q[jl].astype(BF16),
                      out_scale=HEAD_DIM ** -0.5)
            outs, lses = zip(*[_attn_branch(q[g], ks[g], vs[g], g, slopes[g]) for g in range(N_BRANCHES)])
            x = _combine(outs, lses, x, mod, attn_w_o[jl].astype(BF16))
        x = _ffn_layer(x, mod, norm2_g[layer], ffn_w_up[layer].astype(BF16), ffn_conv_w[layer],
                       ffn_conv_b[layer], ffn_w_down[layer].astype(BF16), final_g,
                       final_norm=(layer == depth - 1))
    return x
```

```python
import functools
import math

import jax
import jax.numpy as jnp
import numpy as np
from jax import lax
from jax.experimental import pallas as pl
from jax.experimental.pallas import tpu as pltpu

D_MODEL = 1024
POOL_WINDOWS = (2, 4, 8, 16)
POOL_GROUP_DIM = D_MODEL // len(POOL_WINDOWS)
BRANCHES = ((128, 1), (512, 4), (2048, 16))
N_BRANCHES = len(BRANCHES)
HEAD_DIM = 64
N_HEADS = D_MODEL // HEAD_DIM
D_ATTN = N_HEADS * HEAD_DIM
ATTN_BLOCK = 128
D_FF = 2816
CONV_WIDTH = 3
EPS = 1e-6

LANES = 128
CARRY_ROWS = 8
VMEM_LIMIT = 56 * 1024 * 1024

BF16 = jnp.bfloat16
F32 = jnp.float32


def _alibi_slopes(n):
    def pow2(m):
        start = 2.0 ** (-(2.0 ** -(math.log2(m) - 3)))
        return [start ** (i + 1) for i in range(m)]
    if math.log2(n).is_integer():
        s = pow2(n)
    else:
        c = 2 ** math.floor(math.log2(n))
        s = pow2(c) + pow2(2 * c)[0::2][: n - c]
    s = np.asarray(s, dtype=np.float32)
    return -np.sort(-s)


def _params(*sem):
    return pltpu.CompilerParams(dimension_semantics=sem, vmem_limit_bytes=VMEM_LIMIT)


def _resident(shape):
    nd = len(shape)
    return pl.BlockSpec(shape, lambda *_: (0,) * nd, pipeline_mode=pl.Buffered(1))


def _norm_mod(x, g, shift, scale):
    y = x * lax.rsqrt(jnp.mean(x * x, axis=-1, keepdims=True) + EPS)
    return (y * g) * (1.0 + scale) + shift


def _dot(a, b):
    return jnp.dot(a, b, preferred_element_type=F32)


def _pack(rows_bf16):
    return pltpu.bitcast(rows_bf16, jnp.uint32)


def _unpack(words):
    return pltpu.bitcast(words, BF16)


def _ada_kernel(c_ref, w_ref, b_ref, o_ref):
    c = c_ref[...]
    cond = c * (1.0 / (1.0 + jnp.exp(-c)))
    o_ref[0] = _dot(cond.astype(BF16), w_ref[0].astype(BF16)) + b_ref[0]


def _ada(c, w, b, tn=1024):
    n_layers, d, n = w.shape
    bsz = c.shape[0]
    return pl.pallas_call(
        _ada_kernel,
        grid=(n_layers, n // tn),
        in_specs=[
            pl.BlockSpec((bsz, d), lambda l, j: (0, 0)),
            pl.BlockSpec((1, d, tn), lambda l, j: (l, 0, j)),
            pl.BlockSpec((1, 1, tn), lambda l, j: (l, 0, j)),
        ],
        out_specs=pl.BlockSpec((1, bsz, tn), lambda l, j: (l, 0, j)),
        out_shape=jax.ShapeDtypeStruct((n_layers, bsz, n), F32),
        compiler_params=_params("arbitrary", "arbitrary"),
        name="ada_mod",
    )(c, w, b.reshape(n_layers, 1, n))


def _pool_kernel(x_ref, mod_ref, g_ref, win_ref, wgrp_ref, scale_ref, wout_ref, o_ref,
                 s1_ref, s2_ref, s4_ref, s8_ref, *, tm, n_split):
    si = pl.program_id(1)
    c0 = CARRY_ROWS
    gd = POOL_GROUP_DIM
    stages = (s1_ref, s2_ref, s4_ref, s8_ref)

    @pl.when(si == 0)
    def _():
        for ref in stages:
            ref[0:c0, :] = jnp.zeros((c0, ref.shape[1]), F32)

    mod = mod_ref[0]
    th = tm // n_split
    for part in range(n_split):
        r0 = part * th
        b0 = c0 + r0
        x = x_ref[0, r0:r0 + th, :]
        h = _norm_mod(x, g_ref[...], mod[0:1], mod[1:2])
        u = _dot(h.astype(BF16), win_ref[...])
        s1_ref[b0:b0 + th, :] = u
        for k in range(1, len(stages)):
            prev, cur = stages[k - 1], stages[k]
            sh = 1 << (k - 1)
            cur[b0:b0 + th, :] = prev[b0:b0 + th, gd:] + prev[b0 - sh:b0 - sh + th, gd:]

        t = si * tm + r0 + lax.broadcasted_iota(jnp.int32, (th, 1), 0)
        ys = []
        for g, w in enumerate(POOL_WINDOWS):
            ref = stages[g]
            sh = w // 2
            wsum = ref[b0:b0 + th, 0:gd] + ref[b0 - sh:b0 - sh + th, 0:gd]
            count = jnp.minimum(t + 1, w).astype(F32)
            pooled = wsum / count - u[:, g * gd:(g + 1) * gd]
            ys.append(_dot(pooled.astype(BF16), wgrp_ref[g]))
        y = jnp.concatenate(ys, axis=-1) * scale_ref[...]
        y = _dot(y.astype(BF16), wout_ref[...])
        o_ref[0, r0:r0 + th, :] = x + mod[2:3] * y

    for ref in stages:
        ref[0:c0, :] = ref[tm:tm + c0, :]


def _pool_layer(x, mod, g, w_in, w_grp, scale, w_out, tm=1024, n_split=4):
    bsz, s, d = x.shape
    gd = POOL_GROUP_DIM
    return pl.pallas_call(
        functools.partial(_pool_kernel, tm=tm, n_split=n_split),
        grid=(bsz, s // tm),
        in_specs=[
            pl.BlockSpec((1, tm, d), lambda b, i: (b, i, 0)),
            pl.BlockSpec((1, 6, d), lambda b, i: (b, 0, 0)),
            _resident((1, d)),
            _resident((d, d)),
            _resident((len(POOL_WINDOWS), gd, gd)),
            _resident((1, d)),
            _resident((d, d)),
        ],
        out_specs=pl.BlockSpec((1, tm, d), lambda b, i: (b, i, 0)),
        out_shape=jax.ShapeDtypeStruct(x.shape, F32),
        scratch_shapes=[pltpu.VMEM((tm + CARRY_ROWS, d - k * gd), F32) for k in range(4)],
        compiler_params=_params("arbitrary", "arbitrary"),
        name="pool_mixer",
    )(x, mod, g.reshape(1, d), w_in, w_grp, scale.reshape(1, d), w_out)


def _ffn_kernel(x_ref, mod_ref, g_ref, wup_ref, cw_ref, cb_ref, wdown_ref, fg_ref, o_ref,
                a_ref, *, tm, final_norm):
    si = pl.program_id(1)
    c0 = CARRY_ROWS
    f = D_FF

    @pl.when(si == 0)
    def _():
        a_ref[0:c0, :] = jnp.zeros((c0, f), F32)

    x = x_ref[0]
    mod = mod_ref[0]
    h = _norm_mod(x, g_ref[...], mod[3:4], mod[4:5]).astype(BF16)
    a = _dot(h, wup_ref[:, 0:f])
    v = _dot(h, wup_ref[:, f:2 * f])
    a_ref[c0:c0 + tm, :] = a
    y = cb_ref[...]
    y = y + a_ref[c0 - 2:c0 - 2 + tm, :] * cw_ref[0:1, :]
    y = y + a_ref[c0 - 1:c0 - 1 + tm, :] * cw_ref[1:2, :]
    y = y + a * cw_ref[2:3, :]
    gated = y * (1.0 / (1.0 + jnp.exp(-y))) * v
    out = x + mod[5:6] * _dot(gated.astype(BF16), wdown_ref[...])
    if final_norm:
        out = (out * lax.rsqrt(jnp.mean(out * out, axis=-1, keepdims=True) + EPS)) * fg_ref[...]
    o_ref[0] = out
    a_ref[0:c0, :] = a_ref[tm:tm + c0, :]


def _ffn_layer(x, mod, g, w_up, conv_w, conv_b, w_down, final_g, final_norm, tm=512):
    bsz, s, d = x.shape
    f = D_FF
    return pl.pallas_call(
        functools.partial(_ffn_kernel, tm=tm, final_norm=final_norm),
        grid=(bsz, s // tm),
        in_specs=[
            pl.BlockSpec((1, tm, d), lambda b, i: (b, i, 0)),
            pl.BlockSpec((1, 6, d), lambda b, i: (b, 0, 0)),
            _resident((1, d)),
            _resident((d, 2 * f)),
            _resident((CONV_WIDTH, f)),
            _resident((1, f)),
            _resident((f, d)),
            _resident((1, d)),
        ],
        out_specs=pl.BlockSpec((1, tm, d), lambda b, i: (b, i, 0)),
        out_shape=jax.ShapeDtypeStruct(x.shape, F32),
        scratch_shapes=[pltpu.VMEM((tm + CARRY_ROWS, f), F32)],
        compiler_params=_params("arbitrary", "arbitrary"),
        name="conv_ffn",
    )(x, mod, g.reshape(1, d), w_up, conv_w, conv_b.reshape(1, f), w_down, final_g.reshape(1, d))


def _proj_kernel(x_ref, shift_ref, scale_ref, g_ref, w_ref, *rest, tm, n_tensors, out_scale):
    o_refs, h_ref = rest[:-1], rest[-1]
    h = _norm_mod(x_ref[0], g_ref[...], shift_ref[0], scale_ref[0])
    n_slabs = D_MODEL // LANES
    for c in range(n_slabs):
        h_ref[c] = h[:, c * LANES:(c + 1) * LANES]
    for g, (_, d) in enumerate(BRANCHES):
        n = tm // d
        if d == 1:
            hp = h.astype(BF16)
        else:
            hp = jnp.concatenate(
                [jnp.concatenate([h_ref[c, pl.ds(r, n, stride=d), :] for c in range(n_slabs)], axis=1)
                 for r in range(d)], axis=0).astype(BF16)
        for t in range(n_tensors):
            col = (t * N_BRANCHES + g) * D_ATTN
            res = _dot(hp, w_ref[:, col:col + D_ATTN])
            if out_scale != 1.0:
                res = res * out_scale
            o_ref = o_refs[t * N_BRANCHES + g]
            for r in range(d):
                o_ref[0, r] = _pack(res[r * n:(r + 1) * n].astype(BF16))


def _proj(x, shift, scale, g, w, out_scale=1.0, tm=512):
    bsz, s, d_model = x.shape
    n_tensors = w.shape[1] // (N_BRANCHES * D_ATTN)
    dils = [d for _, d in BRANCHES] * n_tensors
    return pl.pallas_call(
        functools.partial(_proj_kernel, tm=tm, n_tensors=n_tensors, out_scale=out_scale),
        grid=(bsz, s // tm),
        in_specs=[
            pl.BlockSpec((1, tm, d_model), lambda b, i: (b, i, 0)),
            pl.BlockSpec((1, 1, d_model), lambda b, i: (b, 0, 0)),
            pl.BlockSpec((1, 1, d_model), lambda b, i: (b, 0, 0)),
            _resident((1, d_model)),
            _resident(w.shape),
        ],
        out_specs=[pl.BlockSpec((1, d, tm // d // 2, D_ATTN), lambda b, i: (b, 0, i, 0)) for d in dils],
        out_shape=[jax.ShapeDtypeStruct((bsz, d, s // d // 2, D_ATTN), jnp.uint32) for d in dils],
        scratch_shapes=[pltpu.VMEM((d_model // LANES, tm, LANES), F32)],
        compiler_params=_params("arbitrary", "arbitrary"),
        name="norm_proj",
    )(x, shift, scale, g.reshape(1, d_model), w)


def _bias_table(slopes, dilation, n_steps):
    blk = ATTN_BLOCK
    row = np.arange(blk)[:, None]
    col = np.arange(2 * blk)[None, :]
    delta = row + blk - col
    valid = (delta >= 0) & (delta <= n_steps)
    dist = (delta * dilation).astype(np.float32)
    bias = -np.asarray(slopes, np.float32)[:, None, None] * dist[None]
    later = np.where(valid[None], bias, -np.inf).astype(np.float32)
    first = np.where((valid & (col >= blk))[None], bias, -np.inf).astype(np.float32)
    return np.stack([first, later])


def _ones_table():
    blk = ATTN_BLOCK
    t = np.zeros((N_HEADS // 2, 4 * blk, LANES), np.float32)
    for p in range(N_HEADS // 2):
        t[p, :2 * blk, N_HEADS + 2 * p] = 1.0
        t[p, 2 * blk:, N_HEADS + 2 * p + 1] = 1.0
    return t


def _attn_kernel(q_ref, kp_ref, kc_ref, vp_ref, vc_ref, bias_ref, ones_ref, o_ref, st_ref, s_scr, p_scr,
                 *, n_sub, group):
    blk = ATTN_BLOCK
    first = jnp.minimum(pl.program_id(2), 1)
    lane = lax.broadcasted_iota(jnp.int32, (blk, LANES), 1)
    low_half = lane < HEAD_DIM
    low_half2 = lax.broadcasted_iota(jnp.int32, (2 * blk, LANES), 1) < HEAD_DIM

    half_blk = blk // 2

    def keys(sb, prev_ref, cur_ref, cs):
        if sb == 0:
            return jnp.concatenate([_unpack(prev_ref[:, cs]), _unpack(cur_ref[0:half_blk, cs])], axis=0)
        return _unpack(cur_ref[(sb - 1) * half_blk:(sb + 1) * half_blk, cs])

    units = [(sb, g0) for sb in range(n_sub) for g0 in range(0, N_HEADS, group)]
    ms = {}
    m_mats = [jnp.zeros((blk, LANES), F32) for _ in range(n_sub)]
    l_mats = [jnp.zeros((blk, LANES), F32) for _ in range(n_sub)]

    def scores(sb, g0):
        variant = first if sb == 0 else 1
        for pair in range(g0 // 2, (g0 + group) // 2):
            cs = slice(pair * LANES, (pair + 1) * LANES)
            q = _unpack(q_ref[sb * half_blk:(sb + 1) * half_blk, cs])
            zq = jnp.zeros_like(q)
            q2 = jnp.concatenate([jnp.where(low_half, q, zq), jnp.where(low_half, zq, q)], axis=0)
            s2 = lax.dot_general(q2, keys(sb, kp_ref, kc_ref, cs), (((1,), (1,)), ((), ())),
                                 preferred_element_type=F32)
            for half in range(2):
                head = 2 * pair + half
                s = s2[half * blk:(half + 1) * blk] + bias_ref[variant, head]
                s_scr[sb, head] = s
                ms[sb, head] = jnp.max(s, axis=-1, keepdims=True)

    def probs(sb, g0):
        for head in range(g0, g0 + group):
            p_scr[sb, head] = jnp.exp(s_scr[sb, head] - ms[sb, head]).astype(BF16)

    def values(sb, g0):
        rows = slice(sb * blk, (sb + 1) * blk)
        for pair in range(g0 // 2, (g0 + group) // 2):
            cs = slice(pair * LANES, (pair + 1) * LANES)
            v = keys(sb, vp_ref, vc_ref, cs)
            zv = jnp.zeros_like(v)
            v2 = jnp.concatenate([jnp.where(low_half2, v, zv), jnp.where(low_half2, zv, v)], axis=0)
            v2 = jnp.concatenate([v2, ones_ref[pair]], axis=1)
            p2 = jnp.concatenate([p_scr[sb, 2 * pair], p_scr[sb, 2 * pair + 1]], axis=1)
            u = _dot(p2, v2)
            o_ref[rows, cs] = u[:, :LANES]
            l_mats[sb] = l_mats[sb] + u[:, LANES:]
            m_mats[sb] = jnp.where(lane == 2 * pair, ms[sb, 2 * pair], m_mats[sb])
            m_mats[sb] = jnp.where(lane == 2 * pair + 1, ms[sb, 2 * pair + 1], m_mats[sb])

    scores(*units[0])
    for i, unit in enumerate(units):
        if i + 1 < len(units):
            scores(*units[i + 1])
        probs(*unit)
        values(*unit)
    for sb in range(n_sub):
        st_ref[sb * blk:(sb + 1) * blk, :] = m_mats[sb] + l_mats[sb]


def _attn_branch(q, k, v, branch, slopes, max_sub=4, group=8):
    window, d = BRANCHES[branch]
    bsz, _, packed_sub, _ = q.shape
    sub = 2 * packed_sub
    n_steps = window // d
    blk = ATTN_BLOCK
    n_sub = min(max_sub, sub // blk)
    qb = n_sub * blk
    assert n_steps <= blk and sub % qb == 0 and 2 * N_HEADS <= LANES
    bias = jnp.asarray(_bias_table(slopes, d, n_steps))
    ones = jnp.asarray(_ones_table(), dtype=BF16)
    cur = pl.BlockSpec((None, None, qb, D_ATTN), lambda b, r, j: (b, r, j, 0))
    cur_in = pl.BlockSpec((None, None, qb // 2, D_ATTN), lambda b, r, j: (b, r, j, 0))
    prev = pl.BlockSpec((None, None, blk // 2, D_ATTN),
                        lambda b, r, j: (b, r, jnp.maximum(n_sub * j - 1, 0), 0))
    return pl.pallas_call(
        functools.partial(_attn_kernel, n_sub=n_sub, group=group),
        grid=(bsz, d, sub // qb),
        in_specs=[cur_in, prev, cur_in, prev, cur_in, _resident(bias.shape), _resident(ones.shape)],
        out_specs=[cur, pl.BlockSpec((None, None, qb, LANES), lambda b, r, j: (b, r, j, 0))],
        out_shape=[
            jax.ShapeDtypeStruct((bsz, d, sub, D_ATTN), F32),
            jax.ShapeDtypeStruct((bsz, d, sub, LANES), F32),
        ],
        scratch_shapes=[pltpu.VMEM((n_sub, N_HEADS, blk, 2 * blk), F32),
                        pltpu.VMEM((n_sub, N_HEADS, blk, 2 * blk), BF16)],
        compiler_params=_params("arbitrary", "arbitrary", "arbitrary"),
        name=f"dilated_attn_{branch}",
    )(q, k, k, v, v, bias, ones)


def _combine_kernel(o0_ref, o1_ref, o2_ref, l0_ref, l1_ref, l2_ref, x_ref, mod_ref, wo_ref, expand_ref,
                    out_ref, nat_o_ref, nat_l_ref, *, tm):
    o_refs = (o0_ref, o1_ref, o2_ref)
    l_refs = (l0_ref, l1_ref, l2_ref)
    n_slabs = D_ATTN // LANES
    for g, (_, d) in enumerate(BRANCHES):
        if d == 1:
            continue
        n = tm // d
        for r in range(d):
            nat_l_ref[g - 1, pl.ds(r, n, stride=d), :] = l_refs[g][r]
            for c in range(n_slabs):
                nat_o_ref[g - 1, c, pl.ds(r, n, stride=d), :] = o_refs[g][r, :, c * LANES:(c + 1) * LANES]
    stats = [l0_ref[0], nat_l_ref[0], nat_l_ref[1]]
    m = jnp.maximum(jnp.maximum(stats[0], stats[1]), stats[2])
    es = [jnp.exp(st - m) for st in stats]
    sums = [pltpu.roll(st, LANES - N_HEADS, axis=1) for st in stats]
    inv = 1.0 / (es[0] * sums[0] + es[1] * sums[1] + es[2] * sums[2])
    wts = [e * inv for e in es]
    head_lanes = lax.broadcasted_iota(jnp.int32, (tm, LANES), 1) < N_HEADS
    splits = []
    for w in wts:
        w = jnp.where(head_lanes, w, 0.0)
        hi = w.astype(BF16)
        lo = (w - hi.astype(F32)).astype(BF16)
        splits.append(jnp.concatenate([hi, lo], axis=1))
    cols = []
    wide = 2 * LANES
    for c in range(D_ATTN // wide):
        cs = slice(c * wide, (c + 1) * wide)
        outs = [o0_ref[0, :, cs]] + [
            jnp.concatenate([nat_o_ref[g, 2 * c], nat_o_ref[g, 2 * c + 1]], axis=1) for g in range(2)]
        acc = None
        for g in range(N_BRANCHES):
            term = _dot(splits[g], expand_ref[:, cs]) * outs[g]
            acc = term if acc is None else acc + term
        cols.append(acc.astype(BF16))
    o = jnp.concatenate(cols, axis=-1)
    mod = mod_ref[0]
    out_ref[0] = x_ref[0] + mod[2:3] * _dot(o, wo_ref[...])


def _combine(outs, lses, x, mod, w_o, tm=512):
    bsz, s, d_model = x.shape
    dils = [d for _, d in BRANCHES]
    assert dils[0] == 1
    res_major = lambda w: [pl.BlockSpec((None, d, tm // d, w), lambda b, i: (b, 0, i, 0)) for d in dils]
    head_rows = np.zeros((LANES, D_ATTN), np.float32)
    head_rows[np.arange(D_ATTN) // HEAD_DIM, np.arange(D_ATTN)] = 1.0
    expand = jnp.asarray(np.concatenate([head_rows, head_rows]), dtype=BF16)
    return pl.pallas_call(
        functools.partial(_combine_kernel, tm=tm),
        grid=(bsz, s // tm),
        in_specs=res_major(D_ATTN) + res_major(LANES) + [
            pl.BlockSpec((1, tm, d_model), lambda b, i: (b, i, 0)),
            pl.BlockSpec((1, 6, d_model), lambda b, i: (b, 0, 0)),
            _resident((D_ATTN, d_model)),
            _resident(expand.shape),
        ],
        out_specs=pl.BlockSpec((1, tm, d_model), lambda b, i: (b, i, 0)),
        out_shape=jax.ShapeDtypeStruct(x.shape, F32),
        scratch_shapes=[pltpu.VMEM((N_BRANCHES - 1, D_ATTN // LANES, tm, LANES), F32),
                        pltpu.VMEM((N_BRANCHES - 1, tm, LANES), F32)],
        compiler_params=_params("arbitrary", "arbitrary"),
        name="branch_mix_out_proj",
    )(*outs, *lses, x, mod, w_o, expand)


def kernel(x, c, ada_w, ada_b, norm1_g, norm2_g, pool_w_in, pool_w_grp, pool_scale, pool_w_out,
           kv_norm_g, kv_ada_w, kv_ada_b, w_kv, attn_w_q, attn_w_o,
           ffn_w_up, ffn_conv_w, ffn_conv_b, ffn_w_down, final_g):
    bsz, s, d = x.shape
    depth = ada_w.shape[0]
    n_pool = pool_w_in.shape[0]
    slopes = _alibi_slopes(N_BRANCHES * N_HEADS).reshape(N_BRANCHES, N_HEADS)

    mods = _ada(c, ada_w, ada_b).reshape(depth, bsz, 6, d)
    kv_mod = _ada(c, kv_ada_w[None], kv_ada_b[None]).reshape(bsz, 2, 1, d)

    kv = None
    for layer in range(depth):
        mod = mods[layer]
        if layer < n_pool:
            x = _pool_layer(x, mod, norm1_g[layer], pool_w_in[layer].astype(BF16),
                            pool_w_grp[layer].astype(BF16), pool_scale[layer],
                            pool_w_out[layer].astype(BF16))
        else:
            if layer == n_pool:
                kv = _proj(x, kv_mod[:, 0], kv_mod[:, 1], kv_norm_g, w_kv.astype(BF16))
                ks, vs = kv[:N_BRANCHES], kv[N_BRANCHES:]
            jl = layer - n_pool
            q = _proj(x, mod[:, 0:1], mod[:, 1:2], norm1_g[layer], attn_w_q[jl].astype(BF16),
                      out_scale=HEAD_DIM ** -0.5)
            outs, lses = zip(*[_attn_branch(q[g], ks[g], vs[g], g, slopes[g]) for g in range(N_BRANCHES)])
            x = _combine(outs, lses, x, mod, attn_w_o[jl].astype(BF16))
        x = _ffn_layer(x, mod, norm2_g[layer], ffn_w_up[layer].astype(BF16), ffn_conv_w[layer],
                       ffn_conv_b[layer], ffn_w_down[layer].astype(BF16), final_g,
                       final_norm=(layer == depth - 1))
    return x
```

```python
import functools
import math

import jax
import jax.numpy as jnp
import numpy as np
from jax import lax
from jax.experimental import pallas as pl
from jax.experimental.pallas import tpu as pltpu

D_MODEL = 1024
POOL_WINDOWS = (2, 4, 8, 16)
POOL_GROUP_DIM = D_MODEL // len(POOL_WINDOWS)
BRANCHES = ((128, 1), (512, 4), (2048, 16))
N_BRANCHES = len(BRANCHES)
HEAD_DIM = 64
N_HEADS = D_MODEL // HEAD_DIM
D_ATTN = N_HEADS * HEAD_DIM
ATTN_BLOCK = 128
D_FF = 2816
CONV_WIDTH = 3
EPS = 1e-6

LANES = 128
CARRY_ROWS = 8
VMEM_LIMIT = 56 * 1024 * 1024

BF16 = jnp.bfloat16
F32 = jnp.float32


def _alibi_slopes(n):
    def pow2(m):
        start = 2.0 ** (-(2.0 ** -(math.log2(m) - 3)))
        return [start ** (i + 1) for i in range(m)]
    if math.log2(n).is_integer():
        s = pow2(n)
    else:
        c = 2 ** math.floor(math.log2(n))
        s = pow2(c) + pow2(2 * c)[0::2][: n - c]
    s = np.asarray(s, dtype=np.float32)
    return -np.sort(-s)


def _params(*sem):
    return pltpu.CompilerParams(dimension_semantics=sem, vmem_limit_bytes=VMEM_LIMIT)


def _resident(shape):
    nd = len(shape)
    return pl.BlockSpec(shape, lambda *_: (0,) * nd, pipeline_mode=pl.Buffered(1))


def _resident_layer(shape, layer):
    nd = len(shape) - 1
    return pl.BlockSpec((None,) + tuple(shape[1:]), lambda *_: (layer,) + (0,) * nd,
                        pipeline_mode=pl.Buffered(1))


def _mod_spec(layer, d):
    return pl.BlockSpec((None, 1, 6, d), lambda b, i: (layer, b, 0, 0))


def _norm_mod(x, g, shift, scale):
    y = x * lax.rsqrt(jnp.mean(x * x, axis=-1, keepdims=True) + EPS)
    return (y * g) * (1.0 + scale) + shift


def _dot(a, b):
    return jnp.dot(a, b, preferred_element_type=F32)


def _pack(rows_bf16):
    return pltpu.bitcast(rows_bf16, jnp.uint32)


def _unpack(words):
    return pltpu.bitcast(words, BF16)


def _ada_kernel(c_ref, w_ref, b_ref, o_ref):
    c = c_ref[...]
    cond = c * (1.0 / (1.0 + jnp.exp(-c)))
    o_ref[0] = _dot(cond.astype(BF16), w_ref[0].astype(BF16)) + b_ref[0]


def _ada(c, w, b, tn=1024):
    n_layers, d, n = w.shape
    bsz = c.shape[0]
    return pl.pallas_call(
        _ada_kernel,
        grid=(n_layers, n // tn),
        in_specs=[
            pl.BlockSpec((bsz, d), lambda l, j: (0, 0)),
            pl.BlockSpec((1, d, tn), lambda l, j: (l, 0, j)),
            pl.BlockSpec((1, 1, tn), lambda l, j: (l, 0, j)),
        ],
        out_specs=pl.BlockSpec((1, bsz, tn), lambda l, j: (l, 0, j)),
        out_shape=jax.ShapeDtypeStruct((n_layers, bsz, n), F32),
        compiler_params=_params("arbitrary", "arbitrary"),
        name="ada_mod",
    )(c, w, b.reshape(n_layers, 1, n))


def _pool_kernel(x_ref, mod_ref, g_ref, win_ref, wgrp_ref, scale_ref, wout_ref, o_ref,
                 s1_ref, s2_ref, s4_ref, s8_ref, *, tm, n_split):
    si = pl.program_id(1)
    c0 = CARRY_ROWS
    gd = POOL_GROUP_DIM
    stages = (s1_ref, s2_ref, s4_ref, s8_ref)

    @pl.when(si == 0)
    def _():
        for ref in stages:
            ref[0:c0, :] = jnp.zeros((c0, ref.shape[1]), F32)

    mod = mod_ref[0]
    th = tm // n_split
    for part in range(n_split):
        r0 = part * th
        b0 = c0 + r0
        x = x_ref[0, r0:r0 + th, :]
        h = _norm_mod(x, g_ref[...], mod[0:1], mod[1:2])
        u = _dot(h.astype(BF16), win_ref[...])
        s1_ref[b0:b0 + th, :] = u
        for k in range(1, len(stages)):
            prev, cur = stages[k - 1], stages[k]
            sh = 1 << (k - 1)
            cur[b0:b0 + th, :] = prev[b0:b0 + th, gd:] + prev[b0 - sh:b0 - sh + th, gd:]

        t = si * tm + r0 + lax.broadcasted_iota(jnp.int32, (th, 1), 0)
        ys = []
        for g, w in enumerate(POOL_WINDOWS):
            ref = stages[g]
            sh = w // 2
            wsum = ref[b0:b0 + th, 0:gd] + ref[b0 - sh:b0 - sh + th, 0:gd]
            count = jnp.minimum(t + 1, w).astype(F32)
            pooled = wsum / count - u[:, g * gd:(g + 1) * gd]
            ys.append(_dot(pooled.astype(BF16), wgrp_ref[g]))
        y = jnp.concatenate(ys, axis=-1) * scale_ref[...]
        y = _dot(y.astype(BF16), wout_ref[...])
        o_ref[0, r0:r0 + th, :] = x + mod[2:3] * y

    for ref in stages:
        ref[0:c0, :] = ref[tm:tm + c0, :]


def _pool_layer(x, mods, layer, g, w_in, w_grp, scale, w_out, tm=1024, n_split=4):
    bsz, s, d = x.shape
    gd = POOL_GROUP_DIM
    g, scale = g[:, None, :], scale[:, None, :]
    return pl.pallas_call(
        functools.partial(_pool_kernel, tm=tm, n_split=n_split),
        grid=(bsz, s // tm),
        in_specs=[
            pl.BlockSpec((1, tm, d), lambda b, i: (b, i, 0)),
            _mod_spec(layer, d),
            _resident_layer(g.shape, layer),
            _resident_layer(w_in.shape, layer),
            _resident_layer(w_grp.shape, layer),
            _resident_layer(scale.shape, layer),
            _resident_layer(w_out.shape, layer),
        ],
        out_specs=pl.BlockSpec((1, tm, d), lambda b, i: (b, i, 0)),
        out_shape=jax.ShapeDtypeStruct(x.shape, F32),
        scratch_shapes=[pltpu.VMEM((tm + CARRY_ROWS, d - k * gd), F32) for k in range(4)],
        compiler_params=_params("arbitrary", "arbitrary"),
        name="pool_mixer",
    )(x, mods, g, w_in, w_grp, scale, w_out)


def _ffn_kernel(x_ref, mod_ref, g_ref, wup_ref, cw_ref, cb_ref, wdown_ref, fg_ref, o_ref,
                a_ref, *, tm, final_norm):
    si = pl.program_id(1)
    c0 = CARRY_ROWS
    f = D_FF

    @pl.when(si == 0)
    def _():
        a_ref[0:c0, :] = jnp.zeros((c0, f), F32)

    x = x_ref[0]
    mod = mod_ref[0]
    h = _norm_mod(x, g_ref[...], mod[3:4], mod[4:5]).astype(BF16)
    a = _dot(h, wup_ref[:, 0:f])
    v = _dot(h, wup_ref[:, f:2 * f])
    a_ref[c0:c0 + tm, :] = a
    y = cb_ref[...]
    y = y + a_ref[c0 - 2:c0 - 2 + tm, :] * cw_ref[0:1, :]
    y = y + a_ref[c0 - 1:c0 - 1 + tm, :] * cw_ref[1:2, :]
    y = y + a * cw_ref[2:3, :]
    gated = y * (1.0 / (1.0 + jnp.exp(-y))) * v
    out = x + mod[5:6] * _dot(gated.astype(BF16), wdown_ref[...])
    if final_norm:
        out = (out * lax.rsqrt(jnp.mean(out * out, axis=-1, keepdims=True) + EPS)) * fg_ref[...]
    o_ref[0] = out
    a_ref[0:c0, :] = a_ref[tm:tm + c0, :]


def _ffn_layer(x, mods, layer, g, w_up, conv_w, conv_b, w_down, final_g, final_norm, tm=512):
    bsz, s, d = x.shape
    f = D_FF
    g, conv_b = g[:, None, :], conv_b[:, None, :]
    return pl.pallas_call(
        functools.partial(_ffn_kernel, tm=tm, final_norm=final_norm),
        grid=(bsz, s // tm),
        in_specs=[
            pl.BlockSpec((1, tm, d), lambda b, i: (b, i, 0)),
            _mod_spec(layer, d),
            _resident_layer(g.shape, layer),
            _resident_layer(w_up.shape, layer),
            _resident_layer(conv_w.shape, layer),
            _resident_layer(conv_b.shape, layer),
            _resident_layer(w_down.shape, layer),
            _resident((1, d)),
        ],
        out_specs=pl.BlockSpec((1, tm, d), lambda b, i: (b, i, 0)),
        out_shape=jax.ShapeDtypeStruct(x.shape, F32),
        scratch_shapes=[pltpu.VMEM((tm + CARRY_ROWS, f), F32)],
        compiler_params=_params("arbitrary", "arbitrary"),
        name="conv_ffn",
    )(x, mods, g, w_up, conv_w, conv_b, w_down, final_g.reshape(1, d))


def _proj_kernel(x_ref, shift_ref, scale_ref, g_ref, w_ref, *rest, tm, n_tensors, out_scale):
    o_refs, h_ref = rest[:-1], rest[-1]
    h = _norm_mod(x_ref[0], g_ref[...], shift_ref[0], scale_ref[0])
    n_slabs = D_MODEL // LANES
    for c in range(n_slabs):
        h_ref[c] = h[:, c * LANES:(c + 1) * LANES]
    for g, (_, d) in enumerate(BRANCHES):
        n = tm // d
        if d == 1:
            hp = h.astype(BF16)
        else:
            hp = jnp.concatenate(
                [jnp.concatenate([h_ref[c, pl.ds(r, n, stride=d), :] for c in range(n_slabs)], axis=1)
                 for r in range(d)], axis=0).astype(BF16)
        for t in range(n_tensors):
            col = (t * N_BRANCHES + g) * D_ATTN
            res = _dot(hp, w_ref[:, col:col + D_ATTN])
            if out_scale != 1.0:
                res = res * out_scale
            o_ref = o_refs[t * N_BRANCHES + g]
            for r in range(d):
                o_ref[0, r] = _pack(res[r * n:(r + 1) * n].astype(BF16))


def _proj(x, shift, scale, g, w, layer, out_scale=1.0, tm=512):
    bsz, s, d_model = x.shape
    n_tensors = w.shape[2] // (N_BRANCHES * D_ATTN)
    dils = [d for _, d in BRANCHES] * n_tensors
    return pl.pallas_call(
        functools.partial(_proj_kernel, tm=tm, n_tensors=n_tensors, out_scale=out_scale),
        grid=(bsz, s // tm),
        in_specs=[
            pl.BlockSpec((1, tm, d_model), lambda b, i: (b, i, 0)),
            pl.BlockSpec((1, 1, d_model), lambda b, i: (b, 0, 0)),
            pl.BlockSpec((1, 1, d_model), lambda b, i: (b, 0, 0)),
            _resident((1, d_model)),
            _resident_layer(w.shape, layer),
        ],
        out_specs=[pl.BlockSpec((1, d, tm // d // 2, D_ATTN), lambda b, i: (b, 0, i, 0)) for d in dils],
        out_shape=[jax.ShapeDtypeStruct((bsz, d, s // d // 2, D_ATTN), jnp.uint32) for d in dils],
        scratch_shapes=[pltpu.VMEM((d_model // LANES, tm, LANES), F32)],
        compiler_params=_params("arbitrary", "arbitrary"),
        name="norm_proj",
    )(x, shift, scale, g.reshape(1, d_model), w)


def _bias_table(slopes, dilation, n_steps):
    blk = ATTN_BLOCK
    row = np.arange(blk)[:, None]
    col = np.arange(2 * blk)[None, :]
    delta = row + blk - col
    valid = (delta >= 0) & (delta <= n_steps)
    dist = (delta * dilation).astype(np.float32)
    bias = -np.asarray(slopes, np.float32)[:, None, None] * dist[None]
    later = np.where(valid[None], bias, -np.inf).astype(np.float32)
    first = np.where((valid & (col >= blk))[None], bias, -np.inf).astype(np.float32)
    return np.stack([first, later])


def _ones_table():
    blk = ATTN_BLOCK
    t = np.zeros((N_HEADS // 2, 4 * blk, LANES), np.float32)
    for p in range(N_HEADS // 2):
        t[p, :2 * blk, N_HEADS + 2 * p] = 1.0
        t[p, 2 * blk:, N_HEADS + 2 * p + 1] = 1.0
    return t


def _attn_kernel(q_ref, kp_ref, kc_ref, vp_ref, vc_ref, bias_ref, ones_ref, o_ref, st_ref, s_scr, p_scr,
                 *, n_res, n_sub, group):
    blk = ATTN_BLOCK
    first = jnp.minimum(pl.program_id(2), 1)
    lane = lax.broadcasted_iota(jnp.int32, (blk, LANES), 1)
    low_half = lane < HEAD_DIM
    low_half2 = lax.broadcasted_iota(jnp.int32, (2 * blk, LANES), 1) < HEAD_DIM
    half_blk = blk // 2

    def keys(res, sb, prev_ref, cur_ref, cs):
        if sb == 0:
            return jnp.concatenate(
                [_unpack(prev_ref[res, :, cs]), _unpack(cur_ref[res, 0:half_blk, cs])], axis=0)
        return _unpack(cur_ref[res, (sb - 1) * half_blk:(sb + 1) * half_blk, cs])

    units = [(res, sb, g0) for res in range(n_res) for sb in range(n_sub)
             for g0 in range(0, N_HEADS, group)]
    ms = {}
    m_mats = {(res, sb): jnp.zeros((blk, LANES), F32) for res in range(n_res) for sb in range(n_sub)}
    l_mats = dict(m_mats)

    def scores(res, sb, g0):
        variant = first if sb == 0 else 1
        slot = res * n_sub + sb
        for pair in range(g0 // 2, (g0 + group) // 2):
            cs = slice(pair * LANES, (pair + 1) * LANES)
            q = _unpack(q_ref[res, sb * half_blk:(sb + 1) * half_blk, cs])
            zq = jnp.zeros_like(q)
            q2 = jnp.concatenate([jnp.where(low_half, q, zq), jnp.where(low_half, zq, q)], axis=0)
            s2 = lax.dot_general(q2, keys(res, sb, kp_ref, kc_ref, cs), (((1,), (1,)), ((), ())),
                                 preferred_element_type=F32)
            for half in range(2):
                head = 2 * pair + half
                s = s2[half * blk:(half + 1) * blk] + bias_ref[variant, head]
                s_scr[slot, head] = s
                ms[res, sb, head] = jnp.max(s, axis=-1, keepdims=True)

    def probs(res, sb, g0):
        slot = res * n_sub + sb
        for head in range(g0, g0 + group):
            p_scr[slot, head] = jnp.exp(s_scr[slot, head] - ms[res, sb, head]).astype(BF16)

    def values(res, sb, g0):
        slot = res * n_sub + sb
        rows = slice(sb * blk, (sb + 1) * blk)
        for pair in range(g0 // 2, (g0 + group) // 2):
            cs = slice(pair * LANES, (pair + 1) * LANES)
            v = keys(res, sb, vp_ref, vc_ref, cs)
            zv = jnp.zeros_like(v)
            v2 = jnp.concatenate([jnp.where(low_half2, v, zv), jnp.where(low_half2, zv, v)], axis=0)
            v2 = jnp.concatenate([v2, ones_ref[pair]], axis=1)
            p2 = jnp.concatenate([p_scr[slot, 2 * pair], p_scr[slot, 2 * pair + 1]], axis=1)
            u = _dot(p2, v2)
            o_ref[res, rows, cs] = u[:, :LANES]
            l_mats[res, sb] = l_mats[res, sb] + u[:, LANES:]
            for head in (2 * pair, 2 * pair + 1):
                m_mats[res, sb] = jnp.where(lane == head, ms[res, sb, head], m_mats[res, sb])

    scores(*units[0])
    for i, unit in enumerate(units):
        if i + 1 < len(units):
            scores(*units[i + 1])
        probs(*unit)
        values(*unit)
    for (res, sb), m_mat in m_mats.items():
        st_ref[res, sb * blk:(sb + 1) * blk, :] = m_mat + l_mats[res, sb]


def _attn_branch(q, k, v, branch, slopes, blocks_per_step=4, group=8):
    window, d = BRANCHES[branch]
    bsz, _, packed_sub, _ = q.shape
    sub = 2 * packed_sub
    n_steps = window // d
    blk = ATTN_BLOCK
    n_sub = min(blocks_per_step, sub // blk)
    n_res = min(blocks_per_step // n_sub, d)
    qb = n_sub * blk
    assert n_steps <= blk and sub % qb == 0 and d % n_res == 0 and 2 * N_HEADS <= LANES
    bias = jnp.asarray(_bias_table(slopes, d, n_steps))
    ones = jnp.asarray(_ones_table(), dtype=BF16)
    cur = lambda rows, w: pl.BlockSpec((None, n_res, rows, w), lambda b, r, j: (b, r, j, 0))
    prev = pl.BlockSpec((None, n_res, blk // 2, D_ATTN),
                        lambda b, r, j: (b, r, jnp.maximum(n_sub * j - 1, 0), 0))
    cur_in = cur(qb // 2, D_ATTN)
    return pl.pallas_call(
        functools.partial(_attn_kernel, n_res=n_res, n_sub=n_sub, group=group),
        grid=(bsz, d // n_res, sub // qb),
        in_specs=[cur_in, prev, cur_in, prev, cur_in, _resident(bias.shape), _resident(ones.shape)],
        out_specs=[cur(qb, D_ATTN), cur(qb, LANES)],
        out_shape=[
            jax.ShapeDtypeStruct((bsz, d, sub, D_ATTN), F32),
            jax.ShapeDtypeStruct((bsz, d, sub, LANES), F32),
        ],
        scratch_shapes=[pltpu.VMEM((n_res * n_sub, N_HEADS, blk, 2 * blk), F32),
                        pltpu.VMEM((n_res * n_sub, N_HEADS, blk, 2 * blk), BF16)],
        compiler_params=_params("arbitrary", "arbitrary", "arbitrary"),
        name=f"dilated_attn_{branch}",
    )(q, k, k, v, v, bias, ones)


def _combine_kernel(o0_ref, o1_ref, o2_ref, l0_ref, l1_ref, l2_ref, x_ref, mod_ref, wo_ref, expand_ref,
                    out_ref, nat_o_ref, nat_l_ref, *, tm):
    o_refs = (o0_ref, o1_ref, o2_ref)
    l_refs = (l0_ref, l1_ref, l2_ref)
    n_slabs = D_ATTN // LANES
    for g, (_, d) in enumerate(BRANCHES):
        if d == 1:
            continue
        n = tm // d
        for r in range(d):
            nat_l_ref[g - 1, pl.ds(r, n, stride=d), :] = l_refs[g][r]
            for c in range(n_slabs):
                nat_o_ref[g - 1, c, pl.ds(r, n, stride=d), :] = o_refs[g][r, :, c * LANES:(c + 1) * LANES]
    stats = [l0_ref[0], nat_l_ref[0], nat_l_ref[1]]
    m = jnp.maximum(jnp.maximum(stats[0], stats[1]), stats[2])
    es = [jnp.exp(st - m) for st in stats]
    sums = [pltpu.roll(st, LANES - N_HEADS, axis=1) for st in stats]
    inv = 1.0 / (es[0] * sums[0] + es[1] * sums[1] + es[2] * sums[2])
    wts = [e * inv for e in es]
    head_lanes = lax.broadcasted_iota(jnp.int32, (tm, LANES), 1) < N_HEADS
    splits = []
    for w in wts:
        w = jnp.where(head_lanes, w, 0.0)
        hi = w.astype(BF16)
        lo = (w - hi.astype(F32)).astype(BF16)
        splits.append(jnp.concatenate([hi, lo], axis=1))
    cols = []
    wide = 2 * LANES
    for c in range(D_ATTN // wide):
        cs = slice(c * wide, (c + 1) * wide)
        outs = [o0_ref[0, :, cs]] + [
            jnp.concatenate([nat_o_ref[g, 2 * c], nat_o_ref[g, 2 * c + 1]], axis=1) for g in range(2)]
        acc = None
        for g in range(N_BRANCHES):
            term = _dot(splits[g], expand_ref[:, cs]) * outs[g]
            acc = term if acc is None else acc + term
        cols.append(acc.astype(BF16))
    o = jnp.concatenate(cols, axis=-1)
    mod = mod_ref[0]
    out_ref[0] = x_ref[0] + mod[2:3] * _dot(o, wo_ref[...])


def _combine(outs, lses, x, mods, layer, w_o, w_o_layer, tm=512):
    bsz, s, d_model = x.shape
    dils = [d for _, d in BRANCHES]
    assert dils[0] == 1
    res_major = lambda w: [pl.BlockSpec((None, d, tm // d, w), lambda b, i: (b, 0, i, 0)) for d in dils]
    head_rows = np.zeros((LANES, D_ATTN), np.float32)
    head_rows[np.arange(D_ATTN) // HEAD_DIM, np.arange(D_ATTN)] = 1.0
    expand = jnp.asarray(np.concatenate([head_rows, head_rows]), dtype=BF16)
    return pl.pallas_call(
        functools.partial(_combine_kernel, tm=tm),
        grid=(bsz, s // tm),
        in_specs=res_major(D_ATTN) + res_major(LANES) + [
            pl.BlockSpec((1, tm, d_model), lambda b, i: (b, i, 0)),
            _mod_spec(layer, d_model),
            _resident_layer(w_o.shape, w_o_layer),
            _resident(expand.shape),
        ],
        out_specs=pl.BlockSpec((1, tm, d_model), lambda b, i: (b, i, 0)),
        out_shape=jax.ShapeDtypeStruct(x.shape, F32),
        scratch_shapes=[pltpu.VMEM((N_BRANCHES - 1, D_ATTN // LANES, tm, LANES), F32),
                        pltpu.VMEM((N_BRANCHES - 1, tm, LANES), F32)],
        compiler_params=_params("arbitrary", "arbitrary"),
        name="branch_mix_out_proj",
    )(*outs, *lses, x, mods, w_o, expand)


def kernel(x, c, ada_w, ada_b, norm1_g, norm2_g, pool_w_in, pool_w_grp, pool_scale, pool_w_out,
           kv_norm_g, kv_ada_w, kv_ada_b, w_kv, attn_w_q, attn_w_o,
           ffn_w_up, ffn_conv_w, ffn_conv_b, ffn_w_down, final_g):
    bsz, s, d = x.shape
    depth = ada_w.shape[0]
    n_pool = pool_w_in.shape[0]
    slopes = _alibi_slopes(N_BRANCHES * N_HEADS).reshape(N_BRANCHES, N_HEADS)

    mods = _ada(c, ada_w, ada_b).reshape(depth, bsz, 6, d)
    kv_mod = _ada(c, kv_ada_w[None], kv_ada_b[None]).reshape(bsz, 2, 1, d)

    pool_w = [w.astype(BF16) for w in (pool_w_in, pool_w_grp, pool_w_out)]
    w_kv_b, w_q_b, w_o_b = w_kv[None].astype(BF16), attn_w_q.astype(BF16), attn_w_o.astype(BF16)
    w_up_b, w_down_b = ffn_w_up.astype(BF16), ffn_w_down.astype(BF16)

    ks = vs = None
    for layer in range(depth):
        if layer < n_pool:
            x = _pool_layer(x, mods, layer, norm1_g[:n_pool], pool_w[0], pool_w[1], pool_scale, pool_w[2])
        else:
            if layer == n_pool:
                kv = _proj(x, kv_mod[:, 0], kv_mod[:, 1], kv_norm_g, w_kv_b, 0)
                ks, vs = kv[:N_BRANCHES], kv[N_BRANCHES:]
            jl = layer - n_pool
            mod = mods[layer]
            q = _proj(x, mod[:, 0:1], mod[:, 1:2], norm1_g[layer], w_q_b, jl, out_scale=HEAD_DIM ** -0.5)
            outs, lses = zip(*[_attn_branch(q[g], ks[g], vs[g], g, slopes[g]) for g in range(N_BRANCHES)])
            x = _combine(outs, lses, x, mods, layer, w_o_b, jl)
        x = _ffn_layer(x, mods, layer, norm2_g, w_up_b, ffn_conv_w, ffn_conv_b, w_down_b, final_g,
                       final_norm=(layer == depth - 1))
    return x
```

```python
import functools
import math

import jax
import jax.numpy as jnp
import numpy as np
from jax import lax
from jax.experimental import pallas as pl
from jax.experimental.pallas import tpu as pltpu

D_MODEL = 1024
POOL_WINDOWS = (2, 4, 8, 16)
POOL_GROUP_DIM = D_MODEL // len(POOL_WINDOWS)
BRANCHES = ((128, 1), (512, 4), (2048, 16))
N_BRANCHES = len(BRANCHES)
HEAD_DIM = 64
N_HEADS = D_MODEL // HEAD_DIM
D_ATTN = N_HEADS * HEAD_DIM
ATTN_BLOCK = 128
D_FF = 2816
CONV_WIDTH = 3
EPS = 1e-6

LANES = 128
CARRY_ROWS = 8
VMEM_LIMIT = 56 * 1024 * 1024

BF16 = jnp.bfloat16
F32 = jnp.float32


def _alibi_slopes(n):
    def pow2(m):
        start = 2.0 ** (-(2.0 ** -(math.log2(m) - 3)))
        return [start ** (i + 1) for i in range(m)]
    if math.log2(n).is_integer():
        s = pow2(n)
    else:
        c = 2 ** math.floor(math.log2(n))
        s = pow2(c) + pow2(2 * c)[0::2][: n - c]
    s = np.asarray(s, dtype=np.float32)
    return -np.sort(-s)


def _params(*sem):
    return pltpu.CompilerParams(dimension_semantics=sem, vmem_limit_bytes=VMEM_LIMIT)


def _resident(shape):
    nd = len(shape)
    return pl.BlockSpec(shape, lambda *_: (0,) * nd, pipeline_mode=pl.Buffered(1))


def _resident_layer(shape, layer):
    nd = len(shape) - 1
    return pl.BlockSpec((None,) + tuple(shape[1:]), lambda *_: (layer,) + (0,) * nd,
                        pipeline_mode=pl.Buffered(1))


def _mod_spec(layer, d):
    return pl.BlockSpec((None, 1, 6, d), lambda b, i: (layer, b, 0, 0))


def _norm_mod(x, g, shift, scale):
    y = x * lax.rsqrt(jnp.mean(x * x, axis=-1, keepdims=True) + EPS)
    return (y * g) * (1.0 + scale) + shift


def _dot(a, b):
    return jnp.dot(a, b, preferred_element_type=F32)


def _pack(rows_bf16):
    return pltpu.bitcast(rows_bf16, jnp.uint32)


def _unpack(words):
    return pltpu.bitcast(words, BF16)


def _ada_kernel(c_ref, w_ref, b_ref, o_ref):
    c = c_ref[...]
    cond = c * (1.0 / (1.0 + jnp.exp(-c)))
    o_ref[0] = _dot(cond.astype(BF16), w_ref[0].astype(BF16)) + b_ref[0]


def _ada(c, w, b, tn=1024):
    n_layers, d, n = w.shape
    bsz = c.shape[0]
    return pl.pallas_call(
        _ada_kernel,
        grid=(n_layers, n // tn),
        in_specs=[
            pl.BlockSpec((bsz, d), lambda l, j: (0, 0)),
            pl.BlockSpec((1, d, tn), lambda l, j: (l, 0, j)),
            pl.BlockSpec((1, 1, tn), lambda l, j: (l, 0, j)),
        ],
        out_specs=pl.BlockSpec((1, bsz, tn), lambda l, j: (l, 0, j)),
        out_shape=jax.ShapeDtypeStruct((n_layers, bsz, n), F32),
        compiler_params=_params("arbitrary", "arbitrary"),
        name="ada_mod",
    )(c, w, b.reshape(n_layers, 1, n))


def _pool_kernel(x_ref, mod_ref, g_ref, win_ref, wgrp_ref, scale_ref, wout_ref, o_ref,
                 s1_ref, s2_ref, s4_ref, s8_ref, *, tm, n_split):
    si = pl.program_id(1)
    c0 = CARRY_ROWS
    gd = POOL_GROUP_DIM
    stages = (s1_ref, s2_ref, s4_ref, s8_ref)

    @pl.when(si == 0)
    def _():
        for ref in stages:
            ref[0:c0, :] = jnp.zeros((c0, ref.shape[1]), F32)

    mod = mod_ref[0]
    th = tm // n_split
    for part in range(n_split):
        r0 = part * th
        b0 = c0 + r0
        x = x_ref[0, r0:r0 + th, :]
        h = _norm_mod(x, g_ref[...], mod[0:1], mod[1:2])
        u = _dot(h.astype(BF16), win_ref[...])
        s1_ref[b0:b0 + th, :] = u
        for k in range(1, len(stages)):
            prev, cur = stages[k - 1], stages[k]
            sh = 1 << (k - 1)
            cur[b0:b0 + th, :] = prev[b0:b0 + th, gd:] + prev[b0 - sh:b0 - sh + th, gd:]

        t = si * tm + r0 + lax.broadcasted_iota(jnp.int32, (th, 1), 0)
        ys = []
        for g, w in enumerate(POOL_WINDOWS):
            ref = stages[g]
            sh = w // 2
            wsum = ref[b0:b0 + th, 0:gd] + ref[b0 - sh:b0 - sh + th, 0:gd]
            count = jnp.minimum(t + 1, w).astype(F32)
            pooled = wsum / count - u[:, g * gd:(g + 1) * gd]
            ys.append(_dot(pooled.astype(BF16), wgrp_ref[g]))
        y = jnp.concatenate(ys, axis=-1) * scale_ref[...]
        y = _dot(y.astype(BF16), wout_ref[...])
        o_ref[0, r0:r0 + th, :] = x + mod[2:3] * y

    for ref in stages:
        ref[0:c0, :] = ref[tm:tm + c0, :]


def _pool_layer(x, mods, layer, g, w_in, w_grp, scale, w_out, tm=1024, n_split=4):
    bsz, s, d = x.shape
    gd = POOL_GROUP_DIM
    g, scale = g[:, None, :], scale[:, None, :]
    return pl.pallas_call(
        functools.partial(_pool_kernel, tm=tm, n_split=n_split),
        grid=(bsz, s // tm),
        in_specs=[
            pl.BlockSpec((1, tm, d), lambda b, i: (b, i, 0)),
            _mod_spec(layer, d),
            _resident_layer(g.shape, layer),
            _resident_layer(w_in.shape, layer),
            _resident_layer(w_grp.shape, layer),
            _resident_layer(scale.shape, layer),
            _resident_layer(w_out.shape, layer),
        ],
        out_specs=pl.BlockSpec((1, tm, d), lambda b, i: (b, i, 0)),
        out_shape=jax.ShapeDtypeStruct(x.shape, F32),
        scratch_shapes=[pltpu.VMEM((tm + CARRY_ROWS, d - k * gd), F32) for k in range(4)],
        compiler_params=_params("arbitrary", "arbitrary"),
        name="pool_mixer",
    )(x, mods, g, w_in, w_grp, scale, w_out)


def _ffn_kernel(x_ref, mod_ref, g_ref, wup_ref, cw_ref, cb_ref, wdown_ref, fg_ref, o_ref,
                a_ref, *, tm, final_norm):
    si = pl.program_id(1)
    c0 = CARRY_ROWS
    f = D_FF

    @pl.when(si == 0)
    def _():
        a_ref[0:c0, :] = jnp.zeros((c0, f), F32)

    x = x_ref[0]
    mod = mod_ref[0]
    h = _norm_mod(x, g_ref[...], mod[3:4], mod[4:5]).astype(BF16)
    a = _dot(h, wup_ref[:, 0:f])
    v = _dot(h, wup_ref[:, f:2 * f])
    a_ref[c0:c0 + tm, :] = a
    y = cb_ref[...]
    y = y + a_ref[c0 - 2:c0 - 2 + tm, :] * cw_ref[0:1, :]
    y = y + a_ref[c0 - 1:c0 - 1 + tm, :] * cw_ref[1:2, :]
    y = y + a * cw_ref[2:3, :]
    gated = y * (1.0 / (1.0 + jnp.exp(-y))) * v
    out = x + mod[5:6] * _dot(gated.astype(BF16), wdown_ref[...])
    if final_norm:
        out = (out * lax.rsqrt(jnp.mean(out * out, axis=-1, keepdims=True) + EPS)) * fg_ref[...]
    o_ref[0] = out
    a_ref[0:c0, :] = a_ref[tm:tm + c0, :]


def _ffn_layer(x, mods, layer, g, w_up, conv_w, conv_b, w_down, final_g, final_norm, tm=512):
    bsz, s, d = x.shape
    f = D_FF
    g, conv_b = g[:, None, :], conv_b[:, None, :]
    return pl.pallas_call(
        functools.partial(_ffn_kernel, tm=tm, final_norm=final_norm),
        grid=(bsz, s // tm),
        in_specs=[
            pl.BlockSpec((1, tm, d), lambda b, i: (b, i, 0)),
            _mod_spec(layer, d),
            _resident_layer(g.shape, layer),
            _resident_layer(w_up.shape, layer),
            _resident_layer(conv_w.shape, layer),
            _resident_layer(conv_b.shape, layer),
            _resident_layer(w_down.shape, layer),
            _resident((1, d)),
        ],
        out_specs=pl.BlockSpec((1, tm, d), lambda b, i: (b, i, 0)),
        out_shape=jax.ShapeDtypeStruct(x.shape, F32),
        scratch_shapes=[pltpu.VMEM((tm + CARRY_ROWS, f), F32)],
        compiler_params=_params("arbitrary", "arbitrary"),
        name="conv_ffn",
    )(x, mods, g, w_up, conv_w, conv_b, w_down, final_g.reshape(1, d))


def _proj_kernel(x_ref, shift_ref, scale_ref, g_ref, w_ref, *rest, tm, n_tensors, out_scale):
    o_refs, h_ref = rest[:-1], rest[-1]
    h = _norm_mod(x_ref[0], g_ref[...], shift_ref[0], scale_ref[0])
    n_slabs = D_MODEL // LANES
    for c in range(n_slabs):
        h_ref[c] = h[:, c * LANES:(c + 1) * LANES]
    for g, (_, d) in enumerate(BRANCHES):
        n = tm // d
        if d == 1:
            hp = h.astype(BF16)
        else:
            hp = jnp.concatenate(
                [jnp.concatenate([h_ref[c, pl.ds(r, n, stride=d), :] for c in range(n_slabs)], axis=1)
                 for r in range(d)], axis=0).astype(BF16)
        for t in range(n_tensors):
            col = (t * N_BRANCHES + g) * D_ATTN
            res = _dot(hp, w_ref[:, col:col + D_ATTN])
            if out_scale != 1.0:
                res = res * out_scale
            o_ref = o_refs[t * N_BRANCHES + g]
            for r in range(d):
                o_ref[0, r] = _pack(res[r * n:(r + 1) * n].astype(BF16))


def _proj(x, shift, scale, g, w, layer, out_scale=1.0, tm=512):
    bsz, s, d_model = x.shape
    n_tensors = w.shape[2] // (N_BRANCHES * D_ATTN)
    dils = [d for _, d in BRANCHES] * n_tensors
    return pl.pallas_call(
        functools.partial(_proj_kernel, tm=tm, n_tensors=n_tensors, out_scale=out_scale),
        grid=(bsz, s // tm),
        in_specs=[
            pl.BlockSpec((1, tm, d_model), lambda b, i: (b, i, 0)),
            pl.BlockSpec((1, 1, d_model), lambda b, i: (b, 0, 0)),
            pl.BlockSpec((1, 1, d_model), lambda b, i: (b, 0, 0)),
            _resident((1, d_model)),
            _resident_layer(w.shape, layer),
        ],
        out_specs=[pl.BlockSpec((1, d, tm // d // 2, D_ATTN), lambda b, i: (b, 0, i, 0)) for d in dils],
        out_shape=[jax.ShapeDtypeStruct((bsz, d, s // d // 2, D_ATTN), jnp.uint32) for d in dils],
        scratch_shapes=[pltpu.VMEM((d_model // LANES, tm, LANES), F32)],
        compiler_params=_params("arbitrary", "arbitrary"),
        name="norm_proj",
    )(x, shift, scale, g.reshape(1, d_model), w)


def _bias_table(slopes, dilation, n_steps):
    blk = ATTN_BLOCK
    row = np.arange(blk)[:, None]
    col = np.arange(2 * blk)[None, :]
    delta = row + blk - col
    valid = (delta >= 0) & (delta <= n_steps)
    dist = (delta * dilation).astype(np.float32)
    bias = -np.asarray(slopes, np.float32)[:, None, None] * dist[None]
    later = np.where(valid[None], bias, -np.inf).astype(np.float32)
    first = np.where((valid & (col >= blk))[None], bias, -np.inf).astype(np.float32)
    return np.stack([first, later])


def _ones_table():
    blk = ATTN_BLOCK
    t = np.zeros((N_HEADS // 2, 4 * blk, LANES), np.float32)
    for p in range(N_HEADS // 2):
        t[p, :2 * blk, N_HEADS + 2 * p] = 1.0
        t[p, 2 * blk:, N_HEADS + 2 * p + 1] = 1.0
    return t


def _attn_kernel(q_ref, kp_ref, kc_ref, vp_ref, vc_ref, bias_ref, ones_ref, o_ref, st_ref, s_scr, p_scr,
                 *, n_res, n_sub, group):
    blk = ATTN_BLOCK
    first = jnp.minimum(pl.program_id(2), 1)
    lane = lax.broadcasted_iota(jnp.int32, (blk, LANES), 1)
    low_half = lane < HEAD_DIM
    low_half2 = lax.broadcasted_iota(jnp.int32, (2 * blk, LANES), 1) < HEAD_DIM
    half_blk = blk // 2

    def keys(res, sb, prev_ref, cur_ref, cs):
        if sb == 0:
            return jnp.concatenate(
                [_unpack(prev_ref[res, :, cs]), _unpack(cur_ref[res, 0:half_blk, cs])], axis=0)
        return _unpack(cur_ref[res, (sb - 1) * half_blk:(sb + 1) * half_blk, cs])

    units = [(res, sb, g0) for res in range(n_res) for sb in range(n_sub)
             for g0 in range(0, N_HEADS, group)]
    unit_slot = {u: i % 2 for i, u in enumerate(units)}
    ms = {}
    m_mats = {(res, sb): jnp.zeros((blk, LANES), F32) for res in range(n_res) for sb in range(n_sub)}
    l_mats = dict(m_mats)

    def scores(res, sb, g0):
        variant = first if sb == 0 else 1
        slot = unit_slot[res, sb, g0]
        for pair in range(g0 // 2, (g0 + group) // 2):
            cs = slice(pair * LANES, (pair + 1) * LANES)
            q = _unpack(q_ref[res, sb * half_blk:(sb + 1) * half_blk, cs])
            zq = jnp.zeros_like(q)
            q2 = jnp.concatenate([jnp.where(low_half, q, zq), jnp.where(low_half, zq, q)], axis=0)
            s2 = lax.dot_general(q2, keys(res, sb, kp_ref, kc_ref, cs), (((1,), (1,)), ((), ())),
                                 preferred_element_type=F32)
            for half in range(2):
                head = 2 * pair + half
                s = s2[half * blk:(half + 1) * blk] + bias_ref[variant, head]
                s_scr[slot, head - g0] = s
                ms[res, sb, head] = jnp.max(s, axis=-1, keepdims=True)

    def probs(res, sb, g0):
        slot = unit_slot[res, sb, g0]
        for head in range(g0, g0 + group):
            p_scr[slot, head - g0] = jnp.exp(s_scr[slot, head - g0] - ms[res, sb, head]).astype(BF16)

    def values(res, sb, g0):
        slot = unit_slot[res, sb, g0]
        rows = slice(sb * blk, (sb + 1) * blk)
        for pair in range(g0 // 2, (g0 + group) // 2):
            cs = slice(pair * LANES, (pair + 1) * LANES)
            v = keys(res, sb, vp_ref, vc_ref, cs)
            zv = jnp.zeros_like(v)
            v2 = jnp.concatenate([jnp.where(low_half2, v, zv), jnp.where(low_half2, zv, v)], axis=0)
            v2 = jnp.concatenate([v2, ones_ref[pair]], axis=1)
            p2 = jnp.concatenate([p_scr[slot, 2 * pair - g0], p_scr[slot, 2 * pair + 1 - g0]], axis=1)
            u = _dot(p2, v2)
            o_ref[res, rows, cs] = u[:, :LANES]
            l_mats[res, sb] = l_mats[res, sb] + u[:, LANES:]
            for head in (2 * pair, 2 * pair + 1):
                m_mats[res, sb] = jnp.where(lane == head, ms[res, sb, head], m_mats[res, sb])

    scores(*units[0])
    for i, unit in enumerate(units):
        if i + 1 < len(units):
            scores(*units[i + 1])
        probs(*unit)
        values(*unit)
    for (res, sb), m_mat in m_mats.items():
        st_ref[res, sb * blk:(sb + 1) * blk, :] = m_mat + l_mats[res, sb]


def _attn_branch(q, k, v, branch, slopes, blocks_per_step=8, group=8):
    window, d = BRANCHES[branch]
    bsz, _, packed_sub, _ = q.shape
    sub = 2 * packed_sub
    n_steps = window // d
    blk = ATTN_BLOCK
    n_sub = min(blocks_per_step, sub // blk)
    n_res = min(blocks_per_step // n_sub, d)
    qb = n_sub * blk
    assert n_steps <= blk and sub % qb == 0 and d % n_res == 0 and 2 * N_HEADS <= LANES
    bias = jnp.asarray(_bias_table(slopes, d, n_steps))
    ones = jnp.asarray(_ones_table(), dtype=BF16)
    cur = lambda rows, w: pl.BlockSpec((None, n_res, rows, w), lambda b, r, j: (b, r, j, 0))
    prev = pl.BlockSpec((None, n_res, blk // 2, D_ATTN),
                        lambda b, r, j: (b, r, jnp.maximum(n_sub * j - 1, 0), 0))
    cur_in = cur(qb // 2, D_ATTN)
    return pl.pallas_call(
        functools.partial(_attn_kernel, n_res=n_res, n_sub=n_sub, group=group),
        grid=(bsz, d // n_res, sub // qb),
        in_specs=[cur_in, prev, cur_in, prev, cur_in, _resident(bias.shape), _resident(ones.shape)],
        out_specs=[cur(qb, D_ATTN), cur(qb, LANES)],
        out_shape=[
            jax.ShapeDtypeStruct((bsz, d, sub, D_ATTN), F32),
            jax.ShapeDtypeStruct((bsz, d, sub, LANES), F32),
        ],
        scratch_shapes=[pltpu.VMEM((2, group, blk, 2 * blk), F32),
                        pltpu.VMEM((2, group, blk, 2 * blk), BF16)],
        compiler_params=_params("arbitrary", "arbitrary", "arbitrary"),
        name=f"dilated_attn_{branch}",
    )(q, k, k, v, v, bias, ones)


def _combine_kernel(o0_ref, o1_ref, o2_ref, l0_ref, l1_ref, l2_ref, x_ref, mod_ref, wo_ref, expand_ref,
                    out_ref, nat_o_ref, nat_l_ref, *, tm):
    o_refs = (o0_ref, o1_ref, o2_ref)
    l_refs = (l0_ref, l1_ref, l2_ref)
    n_slabs = D_ATTN // LANES
    for g, (_, d) in enumerate(BRANCHES):
        if d == 1:
            continue
        n = tm // d
        for r in range(d):
            nat_l_ref[g - 1, pl.ds(r, n, stride=d), :] = l_refs[g][r]
            for c in range(n_slabs):
                nat_o_ref[g - 1, c, pl.ds(r, n, stride=d), :] = o_refs[g][r, :, c * LANES:(c + 1) * LANES]
    stats = [l0_ref[0], nat_l_ref[0], nat_l_ref[1]]
    m = jnp.maximum(jnp.maximum(stats[0], stats[1]), stats[2])
    es = [jnp.exp(st - m) for st in stats]
    sums = [pltpu.roll(st, LANES - N_HEADS, axis=1) for st in stats]
    inv = 1.0 / (es[0] * sums[0] + es[1] * sums[1] + es[2] * sums[2])
    wts = [e * inv for e in es]
    head_lanes = lax.broadcasted_iota(jnp.int32, (tm, LANES), 1) < N_HEADS
    splits = []
    for w in wts:
        w = jnp.where(head_lanes, w, 0.0)
        hi = w.astype(BF16)
        lo = (w - hi.astype(F32)).astype(BF16)
        splits.append(jnp.concatenate([hi, lo], axis=1))
    cols = []
    wide = 2 * LANES
    for c in range(D_ATTN // wide):
        cs = slice(c * wide, (c + 1) * wide)
        outs = [o0_ref[0, :, cs]] + [
            jnp.concatenate([nat_o_ref[g, 2 * c], nat_o_ref[g, 2 * c + 1]], axis=1) for g in range(2)]
        acc = None
        for g in range(N_BRANCHES):
            term = _dot(splits[g], expand_ref[:, cs]) * outs[g]
            acc = term if acc is None else acc + term
        cols.append(acc.astype(BF16))
    o = jnp.concatenate(cols, axis=-1)
    mod = mod_ref[0]
    out_ref[0] = x_ref[0] + mod[2:3] * _dot(o, wo_ref[...])


def _combine(outs, lses, x, mods, layer, w_o, w_o_layer, tm=512):
    bsz, s, d_model = x.shape
    dils = [d for _, d in BRANCHES]
    assert dils[0] == 1
    res_major = lambda w: [pl.BlockSpec((None, d, tm // d, w), lambda b, i: (b, 0, i, 0)) for d in dils]
    head_rows = np.zeros((LANES, D_ATTN), np.float32)
    head_rows[np.arange(D_ATTN) // HEAD_DIM, np.arange(D_ATTN)] = 1.0
    expand = jnp.asarray(np.concatenate([head_rows, head_rows]), dtype=BF16)
    return pl.pallas_call(
        functools.partial(_combine_kernel, tm=tm),
        grid=(bsz, s // tm),
        in_specs=res_major(D_ATTN) + res_major(LANES) + [
            pl.BlockSpec((1, tm, d_model), lambda b, i: (b, i, 0)),
            _mod_spec(layer, d_model),
            _resident_layer(w_o.shape, w_o_layer),
            _resident(expand.shape),
        ],
        out_specs=pl.BlockSpec((1, tm, d_model), lambda b, i: (b, i, 0)),
        out_shape=jax.ShapeDtypeStruct(x.shape, F32),
        scratch_shapes=[pltpu.VMEM((N_BRANCHES - 1, D_ATTN // LANES, tm, LANES), F32),
                        pltpu.VMEM((N_BRANCHES - 1, tm, LANES), F32)],
        compiler_params=_params("arbitrary", "arbitrary"),
        name="branch_mix_out_proj",
    )(*outs, *lses, x, mods, w_o, expand)


def kernel(x, c, ada_w, ada_b, norm1_g, norm2_g, pool_w_in, pool_w_grp, pool_scale, pool_w_out,
           kv_norm_g, kv_ada_w, kv_ada_b, w_kv, attn_w_q, attn_w_o,
           ffn_w_up, ffn_conv_w, ffn_conv_b, ffn_w_down, final_g):
    bsz, s, d = x.shape
    depth = ada_w.shape[0]
    n_pool = pool_w_in.shape[0]
    slopes = _alibi_slopes(N_BRANCHES * N_HEADS).reshape(N_BRANCHES, N_HEADS)

    mods = _ada(c, ada_w, ada_b).reshape(depth, bsz, 6, d)
    kv_mod = _ada(c, kv_ada_w[None], kv_ada_b[None]).reshape(bsz, 2, 1, d)

    pool_w = [w.astype(BF16) for w in (pool_w_in, pool_w_grp, pool_w_out)]
    w_kv_b, w_q_b, w_o_b = w_kv[None].astype(BF16), attn_w_q.astype(BF16), attn_w_o.astype(BF16)
    w_up_b, w_down_b = ffn_w_up.astype(BF16), ffn_w_down.astype(BF16)

    ks = vs = None
    for layer in range(depth):
        if layer < n_pool:
            x = _pool_layer(x, mods, layer, norm1_g[:n_pool], pool_w[0], pool_w[1], pool_scale, pool_w[2])
        else:
            if layer == n_pool:
                kv = _proj(x, kv_mod[:, 0], kv_mod[:, 1], kv_norm_g, w_kv_b, 0)
                ks, vs = kv[:N_BRANCHES], kv[N_BRANCHES:]
            jl = layer - n_pool
            mod = mods[layer]
            q = _proj(x, mod[:, 0:1], mod[:, 1:2], norm1_g[layer], w_q_b, jl, out_scale=HEAD_DIM ** -0.5)
            outs, lses = zip(*[_attn_branch(q[g], ks[g], vs[g], g, slopes[g]) for g in range(N_BRANCHES)])
            x = _combine(outs, lses, x, mods, layer, w_o_b, jl)
        x = _ffn_layer(x, mods, layer, norm2_g, w_up_b, ffn_conv_w, ffn_conv_b, w_down_b, final_g,
                       final_norm=(layer == depth - 1))
    return x
```

```python
import functools
import math

import jax
import jax.numpy as jnp
import numpy as np
from jax import lax
from jax.experimental import pallas as pl
from jax.experimental.pallas import tpu as pltpu

D_MODEL = 1024
POOL_WINDOWS = (2, 4, 8, 16)
POOL_GROUP_DIM = D_MODEL // len(POOL_WINDOWS)
BRANCHES = ((128, 1), (512, 4), (2048, 16))
N_BRANCHES = len(BRANCHES)
HEAD_DIM = 64
N_HEADS = D_MODEL // HEAD_DIM
D_ATTN = N_HEADS * HEAD_DIM
ATTN_BLOCK = 128
D_FF = 2816
CONV_WIDTH = 3
EPS = 1e-6

LANES = 128
CARRY_ROWS = 8
VMEM_LIMIT = 56 * 1024 * 1024

BF16 = jnp.bfloat16
F32 = jnp.float32


def _alibi_slopes(n):
    def pow2(m):
        start = 2.0 ** (-(2.0 ** -(math.log2(m) - 3)))
        return [start ** (i + 1) for i in range(m)]
    if math.log2(n).is_integer():
        s = pow2(n)
    else:
        c = 2 ** math.floor(math.log2(n))
        s = pow2(c) + pow2(2 * c)[0::2][: n - c]
    s = np.asarray(s, dtype=np.float32)
    return -np.sort(-s)


def _params(*sem):
    return pltpu.CompilerParams(dimension_semantics=sem, vmem_limit_bytes=VMEM_LIMIT)


def _resident(shape):
    nd = len(shape)
    return pl.BlockSpec(shape, lambda *_: (0,) * nd, pipeline_mode=pl.Buffered(1))


def _resident_layer(shape, layer):
    nd = len(shape) - 1
    return pl.BlockSpec((None,) + tuple(shape[1:]), lambda *_: (layer,) + (0,) * nd,
                        pipeline_mode=pl.Buffered(1))


def _mod_spec(layer, d):
    return pl.BlockSpec((None, 1, 6, d), lambda b, i: (layer, b, 0, 0))


def _norm_mod(x, g, shift, scale):
    y = x * lax.rsqrt(jnp.mean(x * x, axis=-1, keepdims=True) + EPS)
    return (y * g) * (1.0 + scale) + shift


def _dot(a, b):
    return jnp.dot(a, b, preferred_element_type=F32)


def _pack(rows_bf16):
    return pltpu.bitcast(rows_bf16, jnp.uint32)


def _unpack(words):
    return pltpu.bitcast(words, BF16)


def _ada_kernel(c_ref, w_ref, b_ref, o_ref):
    c = c_ref[...]
    cond = c * (1.0 / (1.0 + jnp.exp(-c)))
    o_ref[0] = _dot(cond.astype(BF16), w_ref[0].astype(BF16)) + b_ref[0]


def _ada(c, w, b, tn=1024):
    n_layers, d, n = w.shape
    bsz = c.shape[0]
    return pl.pallas_call(
        _ada_kernel,
        grid=(n_layers, n // tn),
        in_specs=[
            pl.BlockSpec((bsz, d), lambda l, j: (0, 0)),
            pl.BlockSpec((1, d, tn), lambda l, j: (l, 0, j)),
            pl.BlockSpec((1, 1, tn), lambda l, j: (l, 0, j)),
        ],
        out_specs=pl.BlockSpec((1, bsz, tn), lambda l, j: (l, 0, j)),
        out_shape=jax.ShapeDtypeStruct((n_layers, bsz, n), F32),
        compiler_params=_params("arbitrary", "arbitrary"),
        name="ada_mod",
    )(c, w, b.reshape(n_layers, 1, n))


def _pool_kernel(x_ref, mod_ref, g_ref, win_ref, wgrp_ref, scale_ref, wout_ref, o_ref,
                 s1_ref, s2_ref, s4_ref, s8_ref, *, tm, n_split):
    si = pl.program_id(1)
    c0 = CARRY_ROWS
    gd = POOL_GROUP_DIM
    stages = (s1_ref, s2_ref, s4_ref, s8_ref)

    @pl.when(si == 0)
    def _():
        for ref in stages:
            ref[0:c0, :] = jnp.zeros((c0, ref.shape[1]), F32)

    mod = mod_ref[0]
    th = tm // n_split
    for part in range(n_split):
        r0 = part * th
        b0 = c0 + r0
        x = x_ref[0, r0:r0 + th, :]
        h = _norm_mod(x, g_ref[...], mod[0:1], mod[1:2])
        u = _dot(h.astype(BF16), win_ref[...])
        s1_ref[b0:b0 + th, :] = u
        for k in range(1, len(stages)):
            prev, cur = stages[k - 1], stages[k]
            sh = 1 << (k - 1)
            cur[b0:b0 + th, :] = prev[b0:b0 + th, gd:] + prev[b0 - sh:b0 - sh + th, gd:]

        t = si * tm + r0 + lax.broadcasted_iota(jnp.int32, (th, 1), 0)
        ys = []
        for g, w in enumerate(POOL_WINDOWS):
            ref = stages[g]
            sh = w // 2
            wsum = ref[b0:b0 + th, 0:gd] + ref[b0 - sh:b0 - sh + th, 0:gd]
            count = jnp.minimum(t + 1, w).astype(F32)
            pooled = wsum / count - u[:, g * gd:(g + 1) * gd]
            ys.append(_dot(pooled.astype(BF16), wgrp_ref[g]))
        y = jnp.concatenate(ys, axis=-1) * scale_ref[...]
        y = _dot(y.astype(BF16), wout_ref[...])
        o_ref[0, r0:r0 + th, :] = x + mod[2:3] * y

    for ref in stages:
        ref[0:c0, :] = ref[tm:tm + c0, :]


def _pool_layer(x, mods, layer, g, w_in, w_grp, scale, w_out, tm=1024, n_split=4):
    bsz, s, d = x.shape
    gd = POOL_GROUP_DIM
    g, scale = g[:, None, :], scale[:, None, :]
    return pl.pallas_call(
        functools.partial(_pool_kernel, tm=tm, n_split=n_split),
        grid=(bsz, s // tm),
        in_specs=[
            pl.BlockSpec((1, tm, d), lambda b, i: (b, i, 0)),
            _mod_spec(layer, d),
            _resident_layer(g.shape, layer),
            _resident_layer(w_in.shape, layer),
            _resident_layer(w_grp.shape, layer),
            _resident_layer(scale.shape, layer),
            _resident_layer(w_out.shape, layer),
        ],
        out_specs=pl.BlockSpec((1, tm, d), lambda b, i: (b, i, 0)),
        out_shape=jax.ShapeDtypeStruct(x.shape, F32),
        scratch_shapes=[pltpu.VMEM((tm + CARRY_ROWS, d - k * gd), F32) for k in range(4)],
        compiler_params=_params("arbitrary", "arbitrary"),
        name="pool_mixer",
    )(x, mods, g, w_in, w_grp, scale, w_out)


def _ffn_kernel(x_ref, mod_ref, g_ref, wup_ref, cw_ref, cb_ref, wdown_ref, fg_ref, o_ref,
                a_ref, *, tm, final_norm):
    si = pl.program_id(1)
    c0 = CARRY_ROWS
    f = D_FF

    @pl.when(si == 0)
    def _():
        a_ref[0:c0, :] = jnp.zeros((c0, f), F32)

    x = x_ref[0]
    mod = mod_ref[0]
    h = _norm_mod(x, g_ref[...], mod[3:4], mod[4:5]).astype(BF16)
    a = _dot(h, wup_ref[:, 0:f])
    v = _dot(h, wup_ref[:, f:2 * f])
    a_ref[c0:c0 + tm, :] = a
    y = cb_ref[...]
    y = y + a_ref[c0 - 2:c0 - 2 + tm, :] * cw_ref[0:1, :]
    y = y + a_ref[c0 - 1:c0 - 1 + tm, :] * cw_ref[1:2, :]
    y = y + a * cw_ref[2:3, :]
    gated = y * (1.0 / (1.0 + jnp.exp(-y))) * v
    out = x + mod[5:6] * _dot(gated.astype(BF16), wdown_ref[...])
    if final_norm:
        out = (out * lax.rsqrt(jnp.mean(out * out, axis=-1, keepdims=True) + EPS)) * fg_ref[...]
    o_ref[0] = out
    a_ref[0:c0, :] = a_ref[tm:tm + c0, :]


def _ffn_layer(x, mods, layer, g, w_up, conv_w, conv_b, w_down, final_g, final_norm, tm=512):
    bsz, s, d = x.shape
    f = D_FF
    g, conv_b = g[:, None, :], conv_b[:, None, :]
    return pl.pallas_call(
        functools.partial(_ffn_kernel, tm=tm, final_norm=final_norm),
        grid=(bsz, s // tm),
        in_specs=[
            pl.BlockSpec((1, tm, d), lambda b, i: (b, i, 0)),
            _mod_spec(layer, d),
            _resident_layer(g.shape, layer),
            _resident_layer(w_up.shape, layer),
            _resident_layer(conv_w.shape, layer),
            _resident_layer(conv_b.shape, layer),
            _resident_layer(w_down.shape, layer),
            _resident((1, d)),
        ],
        out_specs=pl.BlockSpec((1, tm, d), lambda b, i: (b, i, 0)),
        out_shape=jax.ShapeDtypeStruct(x.shape, F32),
        scratch_shapes=[pltpu.VMEM((tm + CARRY_ROWS, f), F32)],
        compiler_params=_params("arbitrary", "arbitrary"),
        name="conv_ffn",
    )(x, mods, g, w_up, conv_w, conv_b, w_down, final_g.reshape(1, d))


def _proj_kernel(x_ref, shift_ref, scale_ref, g_ref, w_ref, *rest, tm, n_tensors, out_scale):
    o_refs, h_ref = rest[:-1], rest[-1]
    h = _norm_mod(x_ref[0], g_ref[...], shift_ref[0], scale_ref[0])
    n_slabs = D_MODEL // LANES
    for c in range(n_slabs):
        h_ref[c] = h[:, c * LANES:(c + 1) * LANES]
    for g, (_, d) in enumerate(BRANCHES):
        n = tm // d
        if d == 1:
            hp = h.astype(BF16)
        else:
            hp = jnp.concatenate(
                [jnp.concatenate([h_ref[c, pl.ds(r, n, stride=d), :] for c in range(n_slabs)], axis=1)
                 for r in range(d)], axis=0).astype(BF16)
        for t in range(n_tensors):
            col = (t * N_BRANCHES + g) * D_ATTN
            res = _dot(hp, w_ref[:, col:col + D_ATTN])
            if out_scale != 1.0:
                res = res * out_scale
            o_ref = o_refs[t * N_BRANCHES + g]
            for r in range(d):
                o_ref[0, r] = _pack(res[r * n:(r + 1) * n].astype(BF16))


def _proj(x, shift, scale, g, w, layer, out_scale=1.0, tm=512):
    bsz, s, d_model = x.shape
    n_tensors = w.shape[2] // (N_BRANCHES * D_ATTN)
    dils = [d for _, d in BRANCHES] * n_tensors
    return pl.pallas_call(
        functools.partial(_proj_kernel, tm=tm, n_tensors=n_tensors, out_scale=out_scale),
        grid=(bsz, s // tm),
        in_specs=[
            pl.BlockSpec((1, tm, d_model), lambda b, i: (b, i, 0)),
            pl.BlockSpec((1, 1, d_model), lambda b, i: (b, 0, 0)),
            pl.BlockSpec((1, 1, d_model), lambda b, i: (b, 0, 0)),
            _resident((1, d_model)),
            _resident_layer(w.shape, layer),
        ],
        out_specs=[pl.BlockSpec((1, d, tm // d // 2, D_ATTN), lambda b, i: (b, 0, i, 0)) for d in dils],
        out_shape=[jax.ShapeDtypeStruct((bsz, d, s // d // 2, D_ATTN), jnp.uint32) for d in dils],
        scratch_shapes=[pltpu.VMEM((d_model // LANES, tm, LANES), F32)],
        compiler_params=_params("arbitrary", "arbitrary"),
        name="norm_proj",
    )(x, shift, scale, g.reshape(1, d_model), w)


def _bias_table(slopes, dilation, n_steps):
    blk = ATTN_BLOCK
    row = np.arange(blk)[:, None]
    col = np.arange(2 * blk)[None, :]
    delta = row + blk - col
    valid = (delta >= 0) & (delta <= n_steps)
    dist = (delta * dilation).astype(np.float32)
    bias = -np.asarray(slopes, np.float32)[:, None, None] * dist[None]
    later = np.where(valid[None], bias, -np.inf).astype(np.float32)
    first = np.where((valid & (col >= blk))[None], bias, -np.inf).astype(np.float32)
    return np.stack([first, later])


def _ones_table():
    blk = ATTN_BLOCK
    t = np.zeros((N_HEADS // 2, 4 * blk, LANES), np.float32)
    for p in range(N_HEADS // 2):
        t[p, :2 * blk, N_HEADS + 2 * p] = 1.0
        t[p, 2 * blk:, N_HEADS + 2 * p + 1] = 1.0
    return t


def _attend(q_ref, kp_ref, kc_ref, vp_ref, vc_ref, bias_ref, ones_ref, o_ref, st_ref, s_scr, p_scr,
            *, first, n_res, n_sub, group):
    blk = ATTN_BLOCK
    lane = lax.broadcasted_iota(jnp.int32, (blk, LANES), 1)
    low_half = lane < HEAD_DIM
    low_half2 = lax.broadcasted_iota(jnp.int32, (2 * blk, LANES), 1) < HEAD_DIM
    half_blk = blk // 2

    def keys(res, sb, prev_ref, cur_ref, cs):
        if sb == 0:
            return jnp.concatenate(
                [_unpack(prev_ref[res, :, cs]), _unpack(cur_ref[res, 0:half_blk, cs])], axis=0)
        return _unpack(cur_ref[res, (sb - 1) * half_blk:(sb + 1) * half_blk, cs])

    units = [(res, sb, g0) for res in range(n_res) for sb in range(n_sub)
             for g0 in range(0, N_HEADS, group)]
    unit_slot = {u: i % 2 for i, u in enumerate(units)}
    ms = {}
    m_mats = {(res, sb): jnp.zeros((blk, LANES), F32) for res in range(n_res) for sb in range(n_sub)}
    l_mats = dict(m_mats)

    def scores(res, sb, g0):
        variant = first if sb == 0 else 1
        slot = unit_slot[res, sb, g0]
        for pair in range(g0 // 2, (g0 + group) // 2):
            cs = slice(pair * LANES, (pair + 1) * LANES)
            q = _unpack(q_ref[res, sb * half_blk:(sb + 1) * half_blk, cs])
            zq = jnp.zeros_like(q)
            q2 = jnp.concatenate([jnp.where(low_half, q, zq), jnp.where(low_half, zq, q)], axis=0)
            s2 = lax.dot_general(q2, keys(res, sb, kp_ref, kc_ref, cs), (((1,), (1,)), ((), ())),
                                 preferred_element_type=F32)
            for half in range(2):
                head = 2 * pair + half
                s = s2[half * blk:(half + 1) * blk] + bias_ref[variant, head]
                s_scr[slot, head - g0] = s
                ms[res, sb, head] = jnp.max(s, axis=-1, keepdims=True)

    def probs(res, sb, g0):
        slot = unit_slot[res, sb, g0]
        for head in range(g0, g0 + group):
            p_scr[slot, head - g0] = jnp.exp(s_scr[slot, head - g0] - ms[res, sb, head]).astype(BF16)

    def values(res, sb, g0):
        slot = unit_slot[res, sb, g0]
        rows = slice(sb * blk, (sb + 1) * blk)
        for pair in range(g0 // 2, (g0 + group) // 2):
            cs = slice(pair * LANES, (pair + 1) * LANES)
            v = keys(res, sb, vp_ref, vc_ref, cs)
            zv = jnp.zeros_like(v)
            v2 = jnp.concatenate([jnp.where(low_half2, v, zv), jnp.where(low_half2, zv, v)], axis=0)
            v2 = jnp.concatenate([v2, ones_ref[pair]], axis=1)
            p2 = jnp.concatenate([p_scr[slot, 2 * pair - g0], p_scr[slot, 2 * pair + 1 - g0]], axis=1)
            u = _dot(p2, v2)
            o_ref[res, rows, cs] = u[:, :LANES]
            l_mats[res, sb] = l_mats[res, sb] + u[:, LANES:]
            for head in (2 * pair, 2 * pair + 1):
                m_mats[res, sb] = jnp.where(lane == head, ms[res, sb, head], m_mats[res, sb])

    scores(*units[0])
    for i, unit in enumerate(units):
        if i + 1 < len(units):
            scores(*units[i + 1])
        probs(*unit)
        values(*unit)
    for (res, sb), m_mat in m_mats.items():
        st_ref[res, sb * blk:(sb + 1) * blk, :] = m_mat + l_mats[res, sb]


def _attn_kernel(*refs, n_res, n_sub, group):
    _attend(*refs, first=jnp.minimum(pl.program_id(2), 1), n_res=n_res, n_sub=n_sub, group=group)


def _attn_tables(branch, slopes):
    window, d = BRANCHES[branch]
    assert window // d <= ATTN_BLOCK and 2 * N_HEADS <= LANES
    return jnp.asarray(_bias_table(slopes, d, window // d)), jnp.asarray(_ones_table(), dtype=BF16)


def _attn_branch(q, k, v, branch, slopes, blocks_per_step=8, group=8):
    window, d = BRANCHES[branch]
    bsz, _, packed_sub, _ = q.shape
    sub = 2 * packed_sub
    blk = ATTN_BLOCK
    n_sub = min(blocks_per_step, sub // blk)
    n_res = min(blocks_per_step // n_sub, d)
    qb = n_sub * blk
    assert sub % qb == 0 and d % n_res == 0
    bias, ones = _attn_tables(branch, slopes)
    cur = lambda rows, w: pl.BlockSpec((None, n_res, rows, w), lambda b, r, j: (b, r, j, 0))
    prev = pl.BlockSpec((None, n_res, blk // 2, D_ATTN),
                        lambda b, r, j: (b, r, jnp.maximum(n_sub * j - 1, 0), 0))
    cur_in = cur(qb // 2, D_ATTN)
    return pl.pallas_call(
        functools.partial(_attn_kernel, n_res=n_res, n_sub=n_sub, group=group),
        grid=(bsz, d // n_res, sub // qb),
        in_specs=[cur_in, prev, cur_in, prev, cur_in, _resident(bias.shape), _resident(ones.shape)],
        out_specs=[cur(qb, D_ATTN), cur(qb, LANES)],
        out_shape=[
            jax.ShapeDtypeStruct((bsz, d, sub, D_ATTN), F32),
            jax.ShapeDtypeStruct((bsz, d, sub, LANES), F32),
        ],
        scratch_shapes=[pltpu.VMEM((2, group, blk, 2 * blk), F32),
                        pltpu.VMEM((2, group, blk, 2 * blk), BF16)],
        compiler_params=_params("arbitrary", "arbitrary", "arbitrary"),
        name=f"dilated_attn_{branch}",
    )(q, k, k, v, v, bias, ones)


def _combine_kernel(q_ref, kp_ref, kc_ref, vp_ref, vc_ref, bias_ref, ones_ref,
                    o1_ref, o2_ref, l1_ref, l2_ref, x_ref, mod_ref, wo_ref, expand_ref,
                    out_ref, o0_ref, l0_ref, s_scr, p_scr, nat_o_ref, nat_l_ref, *, tm, group):
    _attend(q_ref, kp_ref, kc_ref, vp_ref, vc_ref, bias_ref, ones_ref, o0_ref, l0_ref, s_scr, p_scr,
            first=jnp.minimum(pl.program_id(1), 1), n_res=1, n_sub=tm // ATTN_BLOCK, group=group)
    o_refs = (o0_ref, o1_ref, o2_ref)
    l_refs = (l0_ref, l1_ref, l2_ref)
    n_slabs = D_ATTN // LANES
    for g, (_, d) in enumerate(BRANCHES):
        if d == 1:
            continue
        n = tm // d
        for r in range(d):
            nat_l_ref[g - 1, pl.ds(r, n, stride=d), :] = l_refs[g][r]
            for c in range(n_slabs):
                nat_o_ref[g - 1, c, pl.ds(r, n, stride=d), :] = o_refs[g][r, :, c * LANES:(c + 1) * LANES]
    stats = [l0_ref[0], nat_l_ref[0], nat_l_ref[1]]
    m = jnp.maximum(jnp.maximum(stats[0], stats[1]), stats[2])
    es = [jnp.exp(st - m) for st in stats]
    sums = [pltpu.roll(st, LANES - N_HEADS, axis=1) for st in stats]
    inv = 1.0 / (es[0] * sums[0] + es[1] * sums[1] + es[2] * sums[2])
    wts = [e * inv for e in es]
    head_lanes = lax.broadcasted_iota(jnp.int32, (tm, LANES), 1) < N_HEADS
    splits = []
    for w in wts:
        w = jnp.where(head_lanes, w, 0.0)
        hi = w.astype(BF16)
        lo = (w - hi.astype(F32)).astype(BF16)
        splits.append(jnp.concatenate([hi, lo], axis=1))
    cols = []
    wide = 2 * LANES
    for c in range(D_ATTN // wide):
        cs = slice(c * wide, (c + 1) * wide)
        outs = [o0_ref[0, :, cs]] + [
            jnp.concatenate([nat_o_ref[g, 2 * c], nat_o_ref[g, 2 * c + 1]], axis=1) for g in range(2)]
        acc = None
        for g in range(N_BRANCHES):
            term = _dot(splits[g], expand_ref[:, cs]) * outs[g]
            acc = term if acc is None else acc + term
        cols.append(acc.astype(BF16))
    o = jnp.concatenate(cols, axis=-1)
    mod = mod_ref[0]
    out_ref[0] = x_ref[0] + mod[2:3] * _dot(o, wo_ref[...])


def _combine(qkv0, slopes0, outs, stats, x, mods, layer, w_o, w_o_layer, tm=512, group=8):
    bsz, s, d_model = x.shape
    dils = [d for _, d in BRANCHES]
    assert dils[0] == 1
    blk = ATTN_BLOCK
    n_sub = tm // blk
    res_major = lambda w: [pl.BlockSpec((None, d, tm // d, w), lambda b, i: (b, 0, i, 0)) for d in dils[1:]]
    cur = pl.BlockSpec((None, 1, tm // 2, D_ATTN), lambda b, i: (b, 0, i, 0))
    prev = pl.BlockSpec((None, 1, blk // 2, D_ATTN), lambda b, i: (b, 0, jnp.maximum(n_sub * i - 1, 0), 0))
    bias, ones = _attn_tables(0, slopes0)
    head_rows = np.zeros((LANES, D_ATTN), np.float32)
    head_rows[np.arange(D_ATTN) // HEAD_DIM, np.arange(D_ATTN)] = 1.0
    expand = jnp.asarray(np.concatenate([head_rows, head_rows]), dtype=BF16)
    q0, k0, v0 = qkv0
    return pl.pallas_call(
        functools.partial(_combine_kernel, tm=tm, group=group),
        grid=(bsz, s // tm),
        in_specs=[cur, prev, cur, prev, cur, _resident(bias.shape), _resident(ones.shape)]
        + res_major(D_ATTN) + res_major(LANES) + [
            pl.BlockSpec((1, tm, d_model), lambda b, i: (b, i, 0)),
            _mod_spec(layer, d_model),
            _resident_layer(w_o.shape, w_o_layer),
            _resident(expand.shape),
        ],
        out_specs=pl.BlockSpec((1, tm, d_model), lambda b, i: (b, i, 0)),
        out_shape=jax.ShapeDtypeStruct(x.shape, F32),
        scratch_shapes=[pltpu.VMEM((1, tm, D_ATTN), F32),
                        pltpu.VMEM((1, tm, LANES), F32),
                        pltpu.VMEM((2, group, blk, 2 * blk), F32),
                        pltpu.VMEM((2, group, blk, 2 * blk), BF16),
                        pltpu.VMEM((N_BRANCHES - 1, D_ATTN // LANES, tm, LANES), F32),
                        pltpu.VMEM((N_BRANCHES - 1, tm, LANES), F32)],
        compiler_params=_params("arbitrary", "arbitrary"),
        name="branch_mix_out_proj",
    )(q0, k0, k0, v0, v0, bias, ones, *outs, *stats, x, mods, w_o, expand)


def kernel(x, c, ada_w, ada_b, norm1_g, norm2_g, pool_w_in, pool_w_grp, pool_scale, pool_w_out,
           kv_norm_g, kv_ada_w, kv_ada_b, w_kv, attn_w_q, attn_w_o,
           ffn_w_up, ffn_conv_w, ffn_conv_b, ffn_w_down, final_g):
    bsz, s, d = x.shape
    depth = ada_w.shape[0]
    n_pool = pool_w_in.shape[0]
    slopes = _alibi_slopes(N_BRANCHES * N_HEADS).reshape(N_BRANCHES, N_HEADS)

    mods = _ada(c, ada_w, ada_b).reshape(depth, bsz, 6, d)
    kv_mod = _ada(c, kv_ada_w[None], kv_ada_b[None]).reshape(bsz, 2, 1, d)

    pool_w = [w.astype(BF16) for w in (pool_w_in, pool_w_grp, pool_w_out)]
    w_kv_b, w_q_b, w_o_b = w_kv[None].astype(BF16), attn_w_q.astype(BF16), attn_w_o.astype(BF16)
    w_up_b, w_down_b = ffn_w_up.astype(BF16), ffn_w_down.astype(BF16)

    ks = vs = None
    for layer in range(depth):
        if layer < n_pool:
            x = _pool_layer(x, mods, layer, norm1_g[:n_pool], pool_w[0], pool_w[1], pool_scale, pool_w[2])
        else:
            if layer == n_pool:
                kv = _proj(x, kv_mod[:, 0], kv_mod[:, 1], kv_norm_g, w_kv_b, 0)
                ks, vs = kv[:N_BRANCHES], kv[N_BRANCHES:]
            jl = layer - n_pool
            mod = mods[layer]
            q = _proj(x, mod[:, 0:1], mod[:, 1:2], norm1_g[layer], w_q_b, jl, out_scale=HEAD_DIM ** -0.5)
            outs, stats = zip(*[_attn_branch(q[g], ks[g], vs[g], g, slopes[g]) for g in range(1, N_BRANCHES)])
            x = _combine((q[0], ks[0], vs[0]), slopes[0], outs, stats, x, mods, layer, w_o_b, jl)
        x = _ffn_layer(x, mods, layer, norm2_g, w_up_b, ffn_conv_w, ffn_conv_b, w_down_b, final_g,
                       final_norm=(layer == depth - 1))
    return x
```

```python
import functools
import math

import jax
import jax.numpy as jnp
import numpy as np
from jax import lax
from jax.experimental import pallas as pl
from jax.experimental.pallas import tpu as pltpu

D_MODEL = 1024
POOL_WINDOWS = (2, 4, 8, 16)
POOL_GROUP_DIM = D_MODEL // len(POOL_WINDOWS)
BRANCHES = ((128, 1), (512, 4), (2048, 16))
N_BRANCHES = len(BRANCHES)
HEAD_DIM = 64
N_HEADS = D_MODEL // HEAD_DIM
D_ATTN = N_HEADS * HEAD_DIM
ATTN_BLOCK = 128
D_FF = 2816
CONV_WIDTH = 3
EPS = 1e-6

LANES = 128
CARRY_ROWS = 8
VMEM_LIMIT = 56 * 1024 * 1024

BF16 = jnp.bfloat16
F32 = jnp.float32


def _alibi_slopes(n):
    def pow2(m):
        start = 2.0 ** (-(2.0 ** -(math.log2(m) - 3)))
        return [start ** (i + 1) for i in range(m)]
    if math.log2(n).is_integer():
        s = pow2(n)
    else:
        c = 2 ** math.floor(math.log2(n))
        s = pow2(c) + pow2(2 * c)[0::2][: n - c]
    s = np.asarray(s, dtype=np.float32)
    return -np.sort(-s)


def _params(*sem):
    return pltpu.CompilerParams(dimension_semantics=sem, vmem_limit_bytes=VMEM_LIMIT)


def _resident(shape):
    nd = len(shape)
    return pl.BlockSpec(shape, lambda *_: (0,) * nd, pipeline_mode=pl.Buffered(1))


def _resident_layer(shape, layer):
    nd = len(shape) - 1
    return pl.BlockSpec((None,) + tuple(shape[1:]), lambda *_: (layer,) + (0,) * nd,
                        pipeline_mode=pl.Buffered(1))


def _mod_spec(layer, d):
    return pl.BlockSpec((None, 1, 6, d), lambda b, i: (layer, b, 0, 0))


def _norm_mod(x, g, shift, scale):
    y = x * lax.rsqrt(jnp.mean(x * x, axis=-1, keepdims=True) + EPS)
    return (y * g) * (1.0 + scale) + shift


def _dot(a, b):
    return jnp.dot(a, b, preferred_element_type=F32)


def _pack(rows_bf16):
    return pltpu.bitcast(rows_bf16, jnp.uint32)


def _unpack(words):
    return pltpu.bitcast(words, BF16)


def _ada_kernel(c_ref, w_ref, b_ref, o_ref):
    c = c_ref[...]
    cond = c * (1.0 / (1.0 + jnp.exp(-c)))
    o_ref[0] = _dot(cond.astype(BF16), w_ref[0].astype(BF16)) + b_ref[0]


def _ada(c, w, b, tn=1024):
    n_layers, d, n = w.shape
    bsz = c.shape[0]
    return pl.pallas_call(
        _ada_kernel,
        grid=(n_layers, n // tn),
        in_specs=[
            pl.BlockSpec((bsz, d), lambda l, j: (0, 0)),
            pl.BlockSpec((1, d, tn), lambda l, j: (l, 0, j)),
            pl.BlockSpec((1, 1, tn), lambda l, j: (l, 0, j)),
        ],
        out_specs=pl.BlockSpec((1, bsz, tn), lambda l, j: (l, 0, j)),
        out_shape=jax.ShapeDtypeStruct((n_layers, bsz, n), F32),
        compiler_params=_params("arbitrary", "arbitrary"),
        name="ada_mod",
    )(c, w, b.reshape(n_layers, 1, n))


def _pool_kernel(x_ref, mod_ref, g_ref, win_ref, wgrp_ref, scale_ref, wout_ref, o_ref,
                 s1_ref, s2_ref, s4_ref, s8_ref, *, tm, n_split):
    si = pl.program_id(1)
    c0 = CARRY_ROWS
    gd = POOL_GROUP_DIM
    stages = (s1_ref, s2_ref, s4_ref, s8_ref)

    @pl.when(si == 0)
    def _():
        for ref in stages:
            ref[0:c0, :] = jnp.zeros((c0, ref.shape[1]), F32)

    mod = mod_ref[0]
    th = tm // n_split
    for part in range(n_split):
        r0 = part * th
        b0 = c0 + r0
        x = x_ref[0, r0:r0 + th, :]
        h = _norm_mod(x, g_ref[...], mod[0:1], mod[1:2])
        u = _dot(h.astype(BF16), win_ref[...])
        s1_ref[b0:b0 + th, :] = u
        for k in range(1, len(stages)):
            prev, cur = stages[k - 1], stages[k]
            sh = 1 << (k - 1)
            cur[b0:b0 + th, :] = prev[b0:b0 + th, gd:] + prev[b0 - sh:b0 - sh + th, gd:]

        t = si * tm + r0 + lax.broadcasted_iota(jnp.int32, (th, 1), 0)
        ys = []
        for g, w in enumerate(POOL_WINDOWS):
            ref = stages[g]
            sh = w // 2
            wsum = ref[b0:b0 + th, 0:gd] + ref[b0 - sh:b0 - sh + th, 0:gd]
            count = jnp.minimum(t + 1, w).astype(F32)
            pooled = wsum / count - u[:, g * gd:(g + 1) * gd]
            ys.append(_dot(pooled.astype(BF16), wgrp_ref[g]))
        y = jnp.concatenate(ys, axis=-1) * scale_ref[...]
        y = _dot(y.astype(BF16), wout_ref[...])
        o_ref[0, r0:r0 + th, :] = x + mod[2:3] * y

    for ref in stages:
        ref[0:c0, :] = ref[tm:tm + c0, :]


def _pool_layer(x, mods, layer, g, w_in, w_grp, scale, w_out, tm=1024, n_split=4):
    bsz, s, d = x.shape
    gd = POOL_GROUP_DIM
    g, scale = g[:, None, :], scale[:, None, :]
    return pl.pallas_call(
        functools.partial(_pool_kernel, tm=tm, n_split=n_split),
        grid=(bsz, s // tm),
        in_specs=[
            pl.BlockSpec((1, tm, d), lambda b, i: (b, i, 0)),
            _mod_spec(layer, d),
            _resident_layer(g.shape, layer),
            _resident_layer(w_in.shape, layer),
            _resident_layer(w_grp.shape, layer),
            _resident_layer(scale.shape, layer),
            _resident_layer(w_out.shape, layer),
        ],
        out_specs=pl.BlockSpec((1, tm, d), lambda b, i: (b, i, 0)),
        out_shape=jax.ShapeDtypeStruct(x.shape, F32),
        scratch_shapes=[pltpu.VMEM((tm + CARRY_ROWS, d - k * gd), F32) for k in range(4)],
        compiler_params=_params("arbitrary", "arbitrary"),
        name="pool_mixer",
    )(x, mods, g, w_in, w_grp, scale, w_out)


def _ffn_kernel(x_ref, mod_ref, g_ref, wup_ref, cw_ref, cb_ref, wdown_ref, fg_ref, o_ref,
                a_ref, *, tm, n_split, final_norm):
    si = pl.program_id(1)
    c0 = CARRY_ROWS
    f = D_FF

    @pl.when(si == 0)
    def _():
        a_ref[0:c0, :] = jnp.zeros((c0, f), F32)

    mod = mod_ref[0]
    th = tm // n_split
    for part in range(n_split):
        r0 = part * th
        b0 = c0 + r0
        x = x_ref[0, r0:r0 + th, :]
        h = _norm_mod(x, g_ref[...], mod[3:4], mod[4:5]).astype(BF16)
        a = _dot(h, wup_ref[:, 0:f])
        v = _dot(h, wup_ref[:, f:2 * f])
        a_ref[b0:b0 + th, :] = a
        y = cb_ref[...]
        y = y + a_ref[b0 - 2:b0 - 2 + th, :] * cw_ref[0:1, :]
        y = y + a_ref[b0 - 1:b0 - 1 + th, :] * cw_ref[1:2, :]
        y = y + a * cw_ref[2:3, :]
        gated = y * (1.0 / (1.0 + jnp.exp(-y))) * v
        out = x + mod[5:6] * _dot(gated.astype(BF16), wdown_ref[...])
        if final_norm:
            out = (out * lax.rsqrt(jnp.mean(out * out, axis=-1, keepdims=True) + EPS)) * fg_ref[...]
        o_ref[0, r0:r0 + th, :] = out
    a_ref[0:c0, :] = a_ref[tm:tm + c0, :]


def _ffn_layer(x, mods, layer, g, w_up, conv_w, conv_b, w_down, final_g, final_norm, tm=1024, n_split=2):
    bsz, s, d = x.shape
    f = D_FF
    g, conv_b = g[:, None, :], conv_b[:, None, :]
    return pl.pallas_call(
        functools.partial(_ffn_kernel, tm=tm, n_split=n_split, final_norm=final_norm),
        grid=(bsz, s // tm),
        in_specs=[
            pl.BlockSpec((1, tm, d), lambda b, i: (b, i, 0)),
            _mod_spec(layer, d),
            _resident_layer(g.shape, layer),
            _resident_layer(w_up.shape, layer),
            _resident_layer(conv_w.shape, layer),
            _resident_layer(conv_b.shape, layer),
            _resident_layer(w_down.shape, layer),
            _resident((1, d)),
        ],
        out_specs=pl.BlockSpec((1, tm, d), lambda b, i: (b, i, 0)),
        out_shape=jax.ShapeDtypeStruct(x.shape, F32),
        scratch_shapes=[pltpu.VMEM((tm + CARRY_ROWS, f), F32)],
        compiler_params=_params("arbitrary", "arbitrary"),
        name="conv_ffn",
    )(x, mods, g, w_up, conv_w, conv_b, w_down, final_g.reshape(1, d))


def _proj_kernel(x_ref, shift_ref, scale_ref, g_ref, w_ref, *rest, tm, n_tensors, out_scale):
    o_refs, h_ref = rest[:-1], rest[-1]
    h = _norm_mod(x_ref[0], g_ref[...], shift_ref[0], scale_ref[0])
    n_slabs = D_MODEL // LANES
    for c in range(n_slabs):
        h_ref[c] = h[:, c * LANES:(c + 1) * LANES]
    for g, (_, d) in enumerate(BRANCHES):
        n = tm // d
        if d == 1:
            hp = h.astype(BF16)
        else:
            hp = jnp.concatenate(
                [jnp.concatenate([h_ref[c, pl.ds(r, n, stride=d), :] for c in range(n_slabs)], axis=1)
                 for r in range(d)], axis=0).astype(BF16)
        for t in range(n_tensors):
            col = (t * N_BRANCHES + g) * D_ATTN
            res = _dot(hp, w_ref[:, col:col + D_ATTN])
            if out_scale != 1.0:
                res = res * out_scale
            o_ref = o_refs[t * N_BRANCHES + g]
            for r in range(d):
                o_ref[0, r] = _pack(res[r * n:(r + 1) * n].astype(BF16))


def _proj(x, shift, scale, g, w, layer, out_scale=1.0, tm=512):
    bsz, s, d_model = x.shape
    n_tensors = w.shape[2] // (N_BRANCHES * D_ATTN)
    dils = [d for _, d in BRANCHES] * n_tensors
    return pl.pallas_call(
        functools.partial(_proj_kernel, tm=tm, n_tensors=n_tensors, out_scale=out_scale),
        grid=(bsz, s // tm),
        in_specs=[
            pl.BlockSpec((1, tm, d_model), lambda b, i: (b, i, 0)),
            pl.BlockSpec((1, 1, d_model), lambda b, i: (b, 0, 0)),
            pl.BlockSpec((1, 1, d_model), lambda b, i: (b, 0, 0)),
            _resident((1, d_model)),
            _resident_layer(w.shape, layer),
        ],
        out_specs=[pl.BlockSpec((1, d, tm // d // 2, D_ATTN), lambda b, i: (b, 0, i, 0)) for d in dils],
        out_shape=[jax.ShapeDtypeStruct((bsz, d, s // d // 2, D_ATTN), jnp.uint32) for d in dils],
        scratch_shapes=[pltpu.VMEM((d_model // LANES, tm, LANES), F32)],
        compiler_params=_params("arbitrary", "arbitrary"),
        name="norm_proj",
    )(x, shift, scale, g.reshape(1, d_model), w)


def _bias_table(slopes, dilation, n_steps):
    blk = ATTN_BLOCK
    row = np.arange(blk)[:, None]
    col = np.arange(2 * blk)[None, :]
    delta = row + blk - col
    valid = (delta >= 0) & (delta <= n_steps)
    dist = (delta * dilation).astype(np.float32)
    bias = -np.asarray(slopes, np.float32)[:, None, None] * dist[None]
    later = np.where(valid[None], bias, -np.inf).astype(np.float32)
    first = np.where((valid & (col >= blk))[None], bias, -np.inf).astype(np.float32)
    return np.stack([first, later])


def _ones_table():
    blk = ATTN_BLOCK
    t = np.zeros((N_HEADS // 2, 4 * blk, LANES), np.float32)
    for p in range(N_HEADS // 2):
        t[p, :2 * blk, N_HEADS + 2 * p] = 1.0
        t[p, 2 * blk:, N_HEADS + 2 * p + 1] = 1.0
    return t


def _attend(q_ref, kp_ref, kc_ref, vp_ref, vc_ref, bias_ref, ones_ref, o_ref, st_ref, s_scr, p_scr,
            *, first, n_res, n_sub, group):
    blk = ATTN_BLOCK
    lane = lax.broadcasted_iota(jnp.int32, (blk, LANES), 1)
    low_half = lane < HEAD_DIM
    low_half2 = lax.broadcasted_iota(jnp.int32, (2 * blk, LANES), 1) < HEAD_DIM
    half_blk = blk // 2

    def keys(res, sb, prev_ref, cur_ref, cs):
        if sb == 0:
            return jnp.concatenate(
                [_unpack(prev_ref[res, :, cs]), _unpack(cur_ref[res, 0:half_blk, cs])], axis=0)
        return _unpack(cur_ref[res, (sb - 1) * half_blk:(sb + 1) * half_blk, cs])

    units = [(res, sb, g0) for res in range(n_res) for sb in range(n_sub)
             for g0 in range(0, N_HEADS, group)]
    unit_slot = {u: i % 2 for i, u in enumerate(units)}
    ms = {}
    m_mats = {(res, sb): jnp.zeros((blk, LANES), F32) for res in range(n_res) for sb in range(n_sub)}
    l_mats = dict(m_mats)

    def scores(res, sb, g0):
        variant = first if sb == 0 else 1
        slot = unit_slot[res, sb, g0]
        for pair in range(g0 // 2, (g0 + group) // 2):
            cs = slice(pair * LANES, (pair + 1) * LANES)
            q = _unpack(q_ref[res, sb * half_blk:(sb + 1) * half_blk, cs])
            zq = jnp.zeros_like(q)
            q2 = jnp.concatenate([jnp.where(low_half, q, zq), jnp.where(low_half, zq, q)], axis=0)
            s2 = lax.dot_general(q2, keys(res, sb, kp_ref, kc_ref, cs), (((1,), (1,)), ((), ())),
                                 preferred_element_type=F32)
            for half in range(2):
                head = 2 * pair + half
                s = s2[half * blk:(half + 1) * blk] + bias_ref[variant, head]
                s_scr[slot, head - g0] = s
                ms[res, sb, head] = jnp.max(s, axis=-1, keepdims=True)

    def probs(res, sb, g0):
        slot = unit_slot[res, sb, g0]
        for head in range(g0, g0 + group):
            p_scr[slot, head - g0] = jnp.exp(s_scr[slot, head - g0] - ms[res, sb, head]).astype(BF16)

    def values(res, sb, g0):
        slot = unit_slot[res, sb, g0]
        rows = slice(sb * blk, (sb + 1) * blk)
        for pair in range(g0 // 2, (g0 + group) // 2):
            cs = slice(pair * LANES, (pair + 1) * LANES)
            v = keys(res, sb, vp_ref, vc_ref, cs)
            zv = jnp.zeros_like(v)
            v2 = jnp.concatenate([jnp.where(low_half2, v, zv), jnp.where(low_half2, zv, v)], axis=0)
            v2 = jnp.concatenate([v2, ones_ref[pair]], axis=1)
            p2 = jnp.concatenate([p_scr[slot, 2 * pair - g0], p_scr[slot, 2 * pair + 1 - g0]], axis=1)
            u = _dot(p2, v2)
            o_ref[res, rows, cs] = u[:, :LANES]
            l_mats[res, sb] = l_mats[res, sb] + u[:, LANES:]
            for head in (2 * pair, 2 * pair + 1):
                m_mats[res, sb] = jnp.where(lane == head, ms[res, sb, head], m_mats[res, sb])

    scores(*units[0])
    for i, unit in enumerate(units):
        if i + 1 < len(units):
            scores(*units[i + 1])
        probs(*unit)
        values(*unit)
    for (res, sb), m_mat in m_mats.items():
        st_ref[res, sb * blk:(sb + 1) * blk, :] = m_mat + l_mats[res, sb]


def _attn_kernel(*refs, n_res, n_sub, group):
    _attend(*refs, first=jnp.minimum(pl.program_id(2), 1), n_res=n_res, n_sub=n_sub, group=group)


def _attn_tables(branch, slopes):
    window, d = BRANCHES[branch]
    assert window // d <= ATTN_BLOCK and 2 * N_HEADS <= LANES
    return jnp.asarray(_bias_table(slopes, d, window // d)), jnp.asarray(_ones_table(), dtype=BF16)


def _attn_branch(q, k, v, branch, slopes, blocks_per_step=8, group=8):
    window, d = BRANCHES[branch]
    bsz, _, packed_sub, _ = q.shape
    sub = 2 * packed_sub
    blk = ATTN_BLOCK
    n_sub = min(blocks_per_step, sub // blk)
    n_res = min(blocks_per_step // n_sub, d)
    qb = n_sub * blk
    assert sub % qb == 0 and d % n_res == 0
    bias, ones = _attn_tables(branch, slopes)
    cur = lambda rows, w: pl.BlockSpec((None, n_res, rows, w), lambda b, r, j: (b, r, j, 0))
    prev = pl.BlockSpec((None, n_res, blk // 2, D_ATTN),
                        lambda b, r, j: (b, r, jnp.maximum(n_sub * j - 1, 0), 0))
    cur_in = cur(qb // 2, D_ATTN)
    return pl.pallas_call(
        functools.partial(_attn_kernel, n_res=n_res, n_sub=n_sub, group=group),
        grid=(bsz, d // n_res, sub // qb),
        in_specs=[cur_in, prev, cur_in, prev, cur_in, _resident(bias.shape), _resident(ones.shape)],
        out_specs=[cur(qb, D_ATTN), cur(qb, LANES)],
        out_shape=[
            jax.ShapeDtypeStruct((bsz, d, sub, D_ATTN), F32),
            jax.ShapeDtypeStruct((bsz, d, sub, LANES), F32),
        ],
        scratch_shapes=[pltpu.VMEM((2, group, blk, 2 * blk), F32),
                        pltpu.VMEM((2, group, blk, 2 * blk), BF16)],
        compiler_params=_params("arbitrary", "arbitrary", "arbitrary"),
        name=f"dilated_attn_{branch}",
    )(q, k, k, v, v, bias, ones)


def _combine_kernel(q_ref, kp_ref, kc_ref, vp_ref, vc_ref, bias_ref, ones_ref,
                    o1_ref, o2_ref, l1_ref, l2_ref, x_ref, mod_ref, wo_ref, expand_ref,
                    out_ref, o0_ref, l0_ref, s_scr, p_scr, nat_o_ref, nat_l_ref, *, tm, group):
    _attend(q_ref, kp_ref, kc_ref, vp_ref, vc_ref, bias_ref, ones_ref, o0_ref, l0_ref, s_scr, p_scr,
            first=jnp.minimum(pl.program_id(1), 1), n_res=1, n_sub=tm // ATTN_BLOCK, group=group)
    o_refs = (o0_ref, o1_ref, o2_ref)
    l_refs = (l0_ref, l1_ref, l2_ref)
    n_slabs = D_ATTN // LANES
    for g, (_, d) in enumerate(BRANCHES):
        if d == 1:
            continue
        n = tm // d
        for r in range(d):
            nat_l_ref[g - 1, pl.ds(r, n, stride=d), :] = l_refs[g][r]
            for c in range(n_slabs):
                nat_o_ref[g - 1, c, pl.ds(r, n, stride=d), :] = o_refs[g][r, :, c * LANES:(c + 1) * LANES]
    stats = [l0_ref[0], nat_l_ref[0], nat_l_ref[1]]
    m = jnp.maximum(jnp.maximum(stats[0], stats[1]), stats[2])
    es = [jnp.exp(st - m) for st in stats]
    sums = [pltpu.roll(st, LANES - N_HEADS, axis=1) for st in stats]
    inv = 1.0 / (es[0] * sums[0] + es[1] * sums[1] + es[2] * sums[2])
    wts = [e * inv for e in es]
    head_lanes = lax.broadcasted_iota(jnp.int32, (tm, LANES), 1) < N_HEADS
    splits = []
    for w in wts:
        w = jnp.where(head_lanes, w, 0.0)
        hi = w.astype(BF16)
        lo = (w - hi.astype(F32)).astype(BF16)
        splits.append(jnp.concatenate([hi, lo], axis=1))
    cols = []
    wide = 2 * LANES
    for c in range(D_ATTN // wide):
        cs = slice(c * wide, (c + 1) * wide)
        outs = [o0_ref[0, :, cs]] + [
            jnp.concatenate([nat_o_ref[g, 2 * c], nat_o_ref[g, 2 * c + 1]], axis=1) for g in range(2)]
        acc = None
        for g in range(N_BRANCHES):
            term = _dot(splits[g], expand_ref[:, cs]) * outs[g]
            acc = term if acc is None else acc + term
        cols.append(acc.astype(BF16))
    o = jnp.concatenate(cols, axis=-1)
    mod = mod_ref[0]
    out_ref[0] = x_ref[0] + mod[2:3] * _dot(o, wo_ref[...])


def _combine(qkv0, slopes0, outs, stats, x, mods, layer, w_o, w_o_layer, tm=512, group=8):
    bsz, s, d_model = x.shape
    dils = [d for _, d in BRANCHES]
    assert dils[0] == 1
    blk = ATTN_BLOCK
    n_sub = tm // blk
    res_major = lambda w: [pl.BlockSpec((None, d, tm // d, w), lambda b, i: (b, 0, i, 0)) for d in dils[1:]]
    cur = pl.BlockSpec((None, 1, tm // 2, D_ATTN), lambda b, i: (b, 0, i, 0))
    prev = pl.BlockSpec((None, 1, blk // 2, D_ATTN), lambda b, i: (b, 0, jnp.maximum(n_sub * i - 1, 0), 0))
    bias, ones = _attn_tables(0, slopes0)
    head_rows = np.zeros((LANES, D_ATTN), np.float32)
    head_rows[np.arange(D_ATTN) // HEAD_DIM, np.arange(D_ATTN)] = 1.0
    expand = jnp.asarray(np.concatenate([head_rows, head_rows]), dtype=BF16)
    q0, k0, v0 = qkv0
    return pl.pallas_call(
        functools.partial(_combine_kernel, tm=tm, group=group),
        grid=(bsz, s // tm),
        in_specs=[cur, prev, cur, prev, cur, _resident(bias.shape), _resident(ones.shape)]
        + res_major(D_ATTN) + res_major(LANES) + [
            pl.BlockSpec((1, tm, d_model), lambda b, i: (b, i, 0)),
            _mod_spec(layer, d_model),
            _resident_layer(w_o.shape, w_o_layer),
            _resident(expand.shape),
        ],
        out_specs=pl.BlockSpec((1, tm, d_model), lambda b, i: (b, i, 0)),
        out_shape=jax.ShapeDtypeStruct(x.shape, F32),
        scratch_shapes=[pltpu.VMEM((1, tm, D_ATTN), F32),
                        pltpu.VMEM((1, tm, LANES), F32),
                        pltpu.VMEM((2, group, blk, 2 * blk), F32),
                        pltpu.VMEM((2, group, blk, 2 * blk), BF16),
                        pltpu.VMEM((N_BRANCHES - 1, D_ATTN // LANES, tm, LANES), F32),
                        pltpu.VMEM((N_BRANCHES - 1, tm, LANES), F32)],
        compiler_params=_params("arbitrary", "arbitrary"),
        name="branch_mix_out_proj",
    )(q0, k0, k0, v0, v0, bias, ones, *outs, *stats, x, mods, w_o, expand)


def kernel(x, c, ada_w, ada_b, norm1_g, norm2_g, pool_w_in, pool_w_grp, pool_scale, pool_w_out,
           kv_norm_g, kv_ada_w, kv_ada_b, w_kv, attn_w_q, attn_w_o,
           ffn_w_up, ffn_conv_w, ffn_conv_b, ffn_w_down, final_g):
    bsz, s, d = x.shape
    depth = ada_w.shape[0]
    n_pool = pool_w_in.shape[0]
    slopes = _alibi_slopes(N_BRANCHES * N_HEADS).reshape(N_BRANCHES, N_HEADS)

    mods = _ada(c, ada_w, ada_b).reshape(depth, bsz, 6, d)
    kv_mod = _ada(c, kv_ada_w[None], kv_ada_b[None]).reshape(bsz, 2, 1, d)

    pool_w = [w.astype(BF16) for w in (pool_w_in, pool_w_grp, pool_w_out)]
    w_kv_b, w_q_b, w_o_b = w_kv[None].astype(BF16), attn_w_q.astype(BF16), attn_w_o.astype(BF16)
    w_up_b, w_down_b = ffn_w_up.astype(BF16), ffn_w_down.astype(BF16)

    ks = vs = None
    for layer in range(depth):
        if layer < n_pool:
            x = _pool_layer(x, mods, layer, norm1_g[:n_pool], pool_w[0], pool_w[1], pool_scale, pool_w[2])
        else:
            if layer == n_pool:
                kv = _proj(x, kv_mod[:, 0], kv_mod[:, 1], kv_norm_g, w_kv_b, 0)
                ks, vs = kv[:N_BRANCHES], kv[N_BRANCHES:]
            jl = layer - n_pool
            mod = mods[layer]
            q = _proj(x, mod[:, 0:1], mod[:, 1:2], norm1_g[layer], w_q_b, jl, out_scale=HEAD_DIM ** -0.5,
                      tm=1024)
            outs, stats = zip(*[_attn_branch(q[g], ks[g], vs[g], g, slopes[g]) for g in range(1, N_BRANCHES)])
            x = _combine((q[0], ks[0], vs[0]), slopes[0], outs, stats, x, mods, layer, w_o_b, jl)
        x = _ffn_layer(x, mods, layer, norm2_g, w_up_b, ffn_conv_w, ffn_conv_b, w_down_b, final_g,
                       final_norm=(layer == depth - 1))
    return x
```

```python
import functools
import math

import jax
import jax.numpy as jnp
import numpy as np
from jax import lax
from jax.experimental import pallas as pl
from jax.experimental.pallas import tpu as pltpu

D_MODEL = 1024
POOL_WINDOWS = (2, 4, 8, 16)
POOL_GROUP_DIM = D_MODEL // len(POOL_WINDOWS)
BRANCHES = ((128, 1), (512, 4), (2048, 16))
N_BRANCHES = len(BRANCHES)
HEAD_DIM = 64
N_HEADS = D_MODEL // HEAD_DIM
D_ATTN = N_HEADS * HEAD_DIM
ATTN_BLOCK = 128
D_FF = 2816
CONV_WIDTH = 3
EPS = 1e-6

LANES = 128
CARRY_ROWS = 8
VMEM_LIMIT = 56 * 1024 * 1024

BF16 = jnp.bfloat16
F32 = jnp.float32


def _alibi_slopes(n):
    def pow2(m):
        start = 2.0 ** (-(2.0 ** -(math.log2(m) - 3)))
        return [start ** (i + 1) for i in range(m)]
    if math.log2(n).is_integer():
        s = pow2(n)
    else:
        c = 2 ** math.floor(math.log2(n))
        s = pow2(c) + pow2(2 * c)[0::2][: n - c]
    s = np.asarray(s, dtype=np.float32)
    return -np.sort(-s)


def _params(*sem):
    return pltpu.CompilerParams(dimension_semantics=sem, vmem_limit_bytes=VMEM_LIMIT)


def _resident(shape):
    nd = len(shape)
    return pl.BlockSpec(shape, lambda *_: (0,) * nd, pipeline_mode=pl.Buffered(1))


def _resident_layer(shape, layer):
    nd = len(shape) - 1
    return pl.BlockSpec((None,) + tuple(shape[1:]), lambda *_: (layer,) + (0,) * nd,
                        pipeline_mode=pl.Buffered(1))


def _mod_spec(layer, d):
    return pl.BlockSpec((None, 1, 6, d), lambda b, i: (layer, b, 0, 0))


def _norm_mod(x, g, shift, scale):
    y = x * lax.rsqrt(jnp.mean(x * x, axis=-1, keepdims=True) + EPS)
    return (y * g) * (1.0 + scale) + shift


def _dot(a, b):
    return jnp.dot(a, b, preferred_element_type=F32)


def _pack(rows_bf16):
    return pltpu.bitcast(rows_bf16, jnp.uint32)


def _unpack(words):
    return pltpu.bitcast(words, BF16)


def _ada_kernel(c_ref, w_ref, b_ref, o_ref):
    c = c_ref[...]
    cond = c * (1.0 / (1.0 + jnp.exp(-c)))
    o_ref[0] = _dot(cond.astype(BF16), w_ref[0].astype(BF16)) + b_ref[0]


def _ada(c, w, b, tn=1024):
    n_layers, d, n = w.shape
    bsz = c.shape[0]
    return pl.pallas_call(
        _ada_kernel,
        grid=(n_layers, n // tn),
        in_specs=[
            pl.BlockSpec((bsz, d), lambda l, j: (0, 0)),
            pl.BlockSpec((1, d, tn), lambda l, j: (l, 0, j)),
            pl.BlockSpec((1, 1, tn), lambda l, j: (l, 0, j)),
        ],
        out_specs=pl.BlockSpec((1, bsz, tn), lambda l, j: (l, 0, j)),
        out_shape=jax.ShapeDtypeStruct((n_layers, bsz, n), F32),
        compiler_params=_params("arbitrary", "arbitrary"),
        name="ada_mod",
    )(c, w, b.reshape(n_layers, 1, n))


def _pool_kernel(x_ref, mod_ref, g_ref, win_ref, wgrp_ref, scale_ref, wout_ref, o_ref,
                 s1_ref, s2_ref, s4_ref, s8_ref, *, tm, n_split):
    si = pl.program_id(1)
    c0 = CARRY_ROWS
    gd = POOL_GROUP_DIM
    stages = (s1_ref, s2_ref, s4_ref, s8_ref)

    @pl.when(si == 0)
    def _():
        for ref in stages:
            ref[0:c0, :] = jnp.zeros((c0, ref.shape[1]), F32)

    mod = mod_ref[0]
    th = tm // n_split
    for part in range(n_split):
        r0 = part * th
        b0 = c0 + r0
        x = x_ref[0, r0:r0 + th, :]
        h = _norm_mod(x, g_ref[...], mod[0:1], mod[1:2])
        u = _dot(h.astype(BF16), win_ref[...])
        s1_ref[b0:b0 + th, :] = u
        for k in range(1, len(stages)):
            prev, cur = stages[k - 1], stages[k]
            sh = 1 << (k - 1)
            cur[b0:b0 + th, :] = prev[b0:b0 + th, gd:] + prev[b0 - sh:b0 - sh + th, gd:]

        t = si * tm + r0 + lax.broadcasted_iota(jnp.int32, (th, 1), 0)
        ys = []
        for g, w in enumerate(POOL_WINDOWS):
            ref = stages[g]
            sh = w // 2
            wsum = ref[b0:b0 + th, 0:gd] + ref[b0 - sh:b0 - sh + th, 0:gd]
            count = jnp.minimum(t + 1, w).astype(F32)
            pooled = wsum / count - u[:, g * gd:(g + 1) * gd]
            ys.append(_dot(pooled.astype(BF16), wgrp_ref[g]))
        y = jnp.concatenate(ys, axis=-1) * scale_ref[...]
        y = _dot(y.astype(BF16), wout_ref[...])
        o_ref[0, r0:r0 + th, :] = x + mod[2:3] * y

    for ref in stages:
        ref[0:c0, :] = ref[tm:tm + c0, :]


def _pool_layer(x, mods, layer, g, w_in, w_grp, scale, w_out, tm=1024, n_split=4):
    bsz, s, d = x.shape
    gd = POOL_GROUP_DIM
    g, scale = g[:, None, :], scale[:, None, :]
    return pl.pallas_call(
        functools.partial(_pool_kernel, tm=tm, n_split=n_split),
        grid=(bsz, s // tm),
        in_specs=[
            pl.BlockSpec((1, tm, d), lambda b, i: (b, i, 0)),
            _mod_spec(layer, d),
            _resident_layer(g.shape, layer),
            _resident_layer(w_in.shape, layer),
            _resident_layer(w_grp.shape, layer),
            _resident_layer(scale.shape, layer),
            _resident_layer(w_out.shape, layer),
        ],
        out_specs=pl.BlockSpec((1, tm, d), lambda b, i: (b, i, 0)),
        out_shape=jax.ShapeDtypeStruct(x.shape, F32),
        scratch_shapes=[pltpu.VMEM((tm + CARRY_ROWS, d - k * gd), F32) for k in range(4)],
        compiler_params=_params("arbitrary", "arbitrary"),
        name="pool_mixer",
    )(x, mods, g, w_in, w_grp, scale, w_out)


def _ffn_kernel(x_ref, mod_ref, g_ref, wup_ref, cw_ref, cb_ref, wdown_ref, fg_ref, o_ref,
                a_ref, *, tm, final_norm):
    si = pl.program_id(1)
    c0 = CARRY_ROWS
    f = D_FF

    @pl.when(si == 0)
    def _():
        a_ref[0:c0, :] = jnp.zeros((c0, f), F32)

    x = x_ref[0]
    mod = mod_ref[0]
    h = _norm_mod(x, g_ref[...], mod[3:4], mod[4:5]).astype(BF16)
    a = _dot(h, wup_ref[:, 0:f])
    v = _dot(h, wup_ref[:, f:2 * f])
    a_ref[c0:c0 + tm, :] = a
    y = cb_ref[...]
    for k in range(CONV_WIDTH):
        lag = CONV_WIDTH - 1 - k
        tap = a if lag == 0 else a_ref[c0 - lag:c0 - lag + tm, :]
        y = y + tap * cw_ref[k:k + 1, :]
    gated = y * (1.0 / (1.0 + jnp.exp(-y))) * v
    out = x + mod[5:6] * _dot(gated.astype(BF16), wdown_ref[...])
    if final_norm:
        out = (out * lax.rsqrt(jnp.mean(out * out, axis=-1, keepdims=True) + EPS)) * fg_ref[...]
    o_ref[0] = out
    a_ref[0:c0, :] = a_ref[tm:tm + c0, :]


def _ffn_layer(x, mods, layer, g, w_up, conv_w, conv_b, w_down, final_g, final_norm, tm=512):
    bsz, s, d = x.shape
    f = D_FF
    g, conv_b = g[:, None, :], conv_b[:, None, :]
    return pl.pallas_call(
        functools.partial(_ffn_kernel, tm=tm, final_norm=final_norm),
        grid=(bsz, s // tm),
        in_specs=[
            pl.BlockSpec((1, tm, d), lambda b, i: (b, i, 0)),
            _mod_spec(layer, d),
            _resident_layer(g.shape, layer),
            _resident_layer(w_up.shape, layer),
            _resident_layer(conv_w.shape, layer),
            _resident_layer(conv_b.shape, layer),
            _resident_layer(w_down.shape, layer),
            _resident((1, d)),
        ],
        out_specs=pl.BlockSpec((1, tm, d), lambda b, i: (b, i, 0)),
        out_shape=jax.ShapeDtypeStruct(x.shape, F32),
        scratch_shapes=[pltpu.VMEM((tm + CARRY_ROWS, f), F32)],
        compiler_params=_params("arbitrary", "arbitrary"),
        name="conv_ffn",
    )(x, mods, g, w_up, conv_w, conv_b, w_down, final_g.reshape(1, d))


def _proj_kernel(x_ref, shift_ref, scale_ref, g_ref, w_ref, *rest, tm, n_tensors, out_scale):
    o_refs, h_ref = rest[:-1], rest[-1]
    h = _norm_mod(x_ref[0], g_ref[...], shift_ref[0], scale_ref[0])
    n_slabs = D_MODEL // LANES
    for c in range(n_slabs):
        h_ref[c] = h[:, c * LANES:(c + 1) * LANES]
    for g, (_, d) in enumerate(BRANCHES):
        n = tm // d
        if d == 1:
            hp = h.astype(BF16)
        else:
            hp = jnp.concatenate(
                [jnp.concatenate([h_ref[c, pl.ds(r, n, stride=d), :] for c in range(n_slabs)], axis=1)
                 for r in range(d)], axis=0).astype(BF16)
        for t in range(n_tensors):
            col = (t * N_BRANCHES + g) * D_ATTN
            res = _dot(hp, w_ref[:, col:col + D_ATTN])
            if out_scale != 1.0:
                res = res * out_scale
            o_ref = o_refs[t * N_BRANCHES + g]
            for r in range(d):
                o_ref[0, r] = _pack(res[r * n:(r + 1) * n].astype(BF16))


def _proj(x, shift, scale, g, w, layer, out_scale=1.0, tm=512):
    bsz, s, d_model = x.shape
    n_tensors = w.shape[2] // (N_BRANCHES * D_ATTN)
    dils = [d for _, d in BRANCHES] * n_tensors
    return pl.pallas_call(
        functools.partial(_proj_kernel, tm=tm, n_tensors=n_tensors, out_scale=out_scale),
        grid=(bsz, s // tm),
        in_specs=[
            pl.BlockSpec((1, tm, d_model), lambda b, i: (b, i, 0)),
            pl.BlockSpec((1, 1, d_model), lambda b, i: (b, 0, 0)),
            pl.BlockSpec((1, 1, d_model), lambda b, i: (b, 0, 0)),
            _resident((1, d_model)),
            _resident_layer(w.shape, layer),
        ],
        out_specs=[pl.BlockSpec((1, d, tm // d // 2, D_ATTN), lambda b, i: (b, 0, i, 0)) for d in dils],
        out_shape=[jax.ShapeDtypeStruct((bsz, d, s // d // 2, D_ATTN), jnp.uint32) for d in dils],
        scratch_shapes=[pltpu.VMEM((d_model // LANES, tm, LANES), F32)],
        compiler_params=_params("arbitrary", "arbitrary"),
        name="norm_proj",
    )(x, shift, scale, g.reshape(1, d_model), w)


def _bias_table(slopes, dilation, n_steps):
    blk = ATTN_BLOCK
    row = np.arange(blk)[:, None]
    col = np.arange(2 * blk)[None, :]
    delta = row + blk - col
    valid = (delta >= 0) & (delta <= n_steps)
    dist = (delta * dilation).astype(np.float32)
    bias = -np.asarray(slopes, np.float32)[:, None, None] * dist[None]
    later = np.where(valid[None], bias, -np.inf).astype(np.float32)
    first = np.where((valid & (col >= blk))[None], bias, -np.inf).astype(np.float32)
    return np.stack([first, later])


def _ones_table():
    blk = ATTN_BLOCK
    t = np.zeros((N_HEADS // 2, 4 * blk, LANES), np.float32)
    for p in range(N_HEADS // 2):
        t[p, :2 * blk, N_HEADS + 2 * p] = 1.0
        t[p, 2 * blk:, N_HEADS + 2 * p + 1] = 1.0
    return t


def _attend(q_ref, kp_ref, kc_ref, vp_ref, vc_ref, bias_ref, ones_ref, o_ref, st_ref, s_scr, p_scr,
            *, first, n_res, n_sub, group):
    blk = ATTN_BLOCK
    lane = lax.broadcasted_iota(jnp.int32, (blk, LANES), 1)
    low_half = lane < HEAD_DIM
    half_blk = blk // 2

    def solo(sb):
        return sb == 0 and first is None

    def keys(res, sb, prev_ref, cur_ref, cs):
        if solo(sb):
            return _unpack(cur_ref[res, 0:half_blk, cs])
        if sb == 0:
            return jnp.concatenate(
                [_unpack(prev_ref[res, :, cs]), _unpack(cur_ref[res, 0:half_blk, cs])], axis=0)
        return _unpack(cur_ref[res, (sb - 1) * half_blk:(sb + 1) * half_blk, cs])

    units = [(res, sb, g0) for res in range(n_res) for sb in range(n_sub)
             for g0 in range(0, N_HEADS, group)]
    unit_slot = {u: i % 2 for i, u in enumerate(units)}
    ms = {}
    m_mats = {(res, sb): jnp.zeros((blk, LANES), F32) for res in range(n_res) for sb in range(n_sub)}
    l_mats = dict(m_mats)

    def scores(res, sb, g0):
        variant = first if sb == 0 else 1
        kw = blk if solo(sb) else 2 * blk
        slot = unit_slot[res, sb, g0]
        for pair in range(g0 // 2, (g0 + group) // 2):
            cs = slice(pair * LANES, (pair + 1) * LANES)
            q = _unpack(q_ref[res, sb * half_blk:(sb + 1) * half_blk, cs])
            zq = jnp.zeros_like(q)
            q2 = jnp.concatenate([jnp.where(low_half, q, zq), jnp.where(low_half, zq, q)], axis=0)
            s2 = lax.dot_general(q2, keys(res, sb, kp_ref, kc_ref, cs), (((1,), (1,)), ((), ())),
                                 preferred_element_type=F32)
            for half in range(2):
                head = 2 * pair + half
                bias = bias_ref[1, head, :, blk:] if solo(sb) else bias_ref[variant, head]
                s = s2[half * blk:(half + 1) * blk] + bias
                s_scr[slot, head - g0, :, 0:kw] = s
                ms[res, sb, head] = jnp.max(s, axis=-1, keepdims=True)

    def probs(res, sb, g0):
        slot = unit_slot[res, sb, g0]
        kw = blk if solo(sb) else 2 * blk
        for head in range(g0, g0 + group):
            p_scr[slot, head - g0, :, 0:kw] = jnp.exp(
                s_scr[slot, head - g0, :, 0:kw] - ms[res, sb, head]).astype(BF16)

    def values(res, sb, g0):
        slot = unit_slot[res, sb, g0]
        rows = slice(sb * blk, (sb + 1) * blk)
        for pair in range(g0 // 2, (g0 + group) // 2):
            cs = slice(pair * LANES, (pair + 1) * LANES)
            v = keys(res, sb, vp_ref, vc_ref, cs)
            kw = v.shape[0]
            zv = jnp.zeros_like(v)
            keep = lax.broadcasted_iota(jnp.int32, (kw, LANES), 1) < HEAD_DIM
            v2 = jnp.concatenate([jnp.where(keep, v, zv), jnp.where(keep, zv, v)], axis=0)
            ones = jnp.concatenate([ones_ref[pair, 0:kw], ones_ref[pair, 2 * blk:2 * blk + kw]], axis=0)
            v2 = jnp.concatenate([v2, ones], axis=1)
            p2 = jnp.concatenate([p_scr[slot, 2 * pair - g0, :, 0:kw],
                                  p_scr[slot, 2 * pair + 1 - g0, :, 0:kw]], axis=1)
            u = _dot(p2, v2)
            o_ref[res, rows, cs] = u[:, :LANES]
            l_mats[res, sb] = l_mats[res, sb] + u[:, LANES:]
            for head in (2 * pair, 2 * pair + 1):
                m_mats[res, sb] = jnp.where(lane == head, ms[res, sb, head], m_mats[res, sb])

    scores(*units[0])
    for i, unit in enumerate(units):
        if i + 1 < len(units):
            scores(*units[i + 1])
        probs(*unit)
        values(*unit)
    for (res, sb), m_mat in m_mats.items():
        st_ref[res, sb * blk:(sb + 1) * blk, :] = m_mat + l_mats[res, sb]


def _attn_kernel(*refs, n_res, n_sub, group, single_step):
    first = None if single_step else jnp.minimum(pl.program_id(2), 1)
    _attend(*refs, first=first, n_res=n_res, n_sub=n_sub, group=group)


def _attn_tables(branch, slopes):
    window, d = BRANCHES[branch]
    assert window // d <= ATTN_BLOCK and 2 * N_HEADS <= LANES
    return jnp.asarray(_bias_table(slopes, d, window // d)), jnp.asarray(_ones_table(), dtype=BF16)


def _attn_branch(q, k, v, branch, slopes, blocks_per_step=8, group=8):
    window, d = BRANCHES[branch]
    bsz, _, packed_sub, _ = q.shape
    sub = 2 * packed_sub
    blk = ATTN_BLOCK
    n_sub = min(blocks_per_step, sub // blk)
    n_res = min(blocks_per_step // n_sub, d)
    qb = n_sub * blk
    assert sub % qb == 0 and d % n_res == 0
    bias, ones = _attn_tables(branch, slopes)
    cur = lambda rows, w: pl.BlockSpec((None, n_res, rows, w), lambda b, r, j: (b, r, j, 0))
    prev = pl.BlockSpec((None, n_res, blk // 2, D_ATTN),
                        lambda b, r, j: (b, r, jnp.maximum(n_sub * j - 1, 0), 0))
    cur_in = cur(qb // 2, D_ATTN)
    return pl.pallas_call(
        functools.partial(_attn_kernel, n_res=n_res, n_sub=n_sub, group=group, single_step=(sub == qb)),
        grid=(bsz, d // n_res, sub // qb),
        in_specs=[cur_in, prev, cur_in, prev, cur_in, _resident(bias.shape), _resident(ones.shape)],
        out_specs=[cur(qb, D_ATTN), cur(qb, LANES)],
        out_shape=[
            jax.ShapeDtypeStruct((bsz, d, sub, D_ATTN), F32),
            jax.ShapeDtypeStruct((bsz, d, sub, LANES), F32),
        ],
        scratch_shapes=[pltpu.VMEM((2, group, blk, 2 * blk), F32),
                        pltpu.VMEM((2, group, blk, 2 * blk), BF16)],
        compiler_params=_params("arbitrary", "arbitrary", "arbitrary"),
        name=f"dilated_attn_{branch}",
    )(q, k, k, v, v, bias, ones)


def _combine_kernel(q_ref, kp_ref, kc_ref, vp_ref, vc_ref, bias_ref, ones_ref,
                    o1_ref, o2_ref, l1_ref, l2_ref, x_ref, mod_ref, wo_ref, expand_ref,
                    out_ref, o0_ref, l0_ref, s_scr, p_scr, nat_o_ref, nat_l_ref, *, tm, group):
    _attend(q_ref, kp_ref, kc_ref, vp_ref, vc_ref, bias_ref, ones_ref, o0_ref, l0_ref, s_scr, p_scr,
            first=jnp.minimum(pl.program_id(1), 1), n_res=1, n_sub=tm // ATTN_BLOCK, group=group)
    o_refs = (o0_ref, o1_ref, o2_ref)
    l_refs = (l0_ref, l1_ref, l2_ref)
    n_slabs = D_ATTN // LANES
    for g, (_, d) in enumerate(BRANCHES):
        if d == 1:
            continue
        n = tm // d
        for r in range(d):
            nat_l_ref[g - 1, pl.ds(r, n, stride=d), :] = l_refs[g][r]
            for c in range(n_slabs):
                nat_o_ref[g - 1, c, pl.ds(r, n, stride=d), :] = o_refs[g][r, :, c * LANES:(c + 1) * LANES]
    stats = [l0_ref[0], nat_l_ref[0], nat_l_ref[1]]
    m = jnp.maximum(jnp.maximum(stats[0], stats[1]), stats[2])
    es = [jnp.exp(st - m) for st in stats]
    sums = [pltpu.roll(st, LANES - N_HEADS, axis=1) for st in stats]
    inv = 1.0 / (es[0] * sums[0] + es[1] * sums[1] + es[2] * sums[2])
    wts = [e * inv for e in es]
    head_lanes = lax.broadcasted_iota(jnp.int32, (tm, LANES), 1) < N_HEADS
    splits = []
    for w in wts:
        w = jnp.where(head_lanes, w, 0.0)
        hi = w.astype(BF16)
        lo = (w - hi.astype(F32)).astype(BF16)
        splits.append(jnp.concatenate([hi, lo], axis=1))
    cols = []
    wide = 2 * LANES
    for c in range(D_ATTN // wide):
        cs = slice(c * wide, (c + 1) * wide)
        outs = [o0_ref[0, :, cs]] + [
            jnp.concatenate([nat_o_ref[g, 2 * c], nat_o_ref[g, 2 * c + 1]], axis=1) for g in range(2)]
        acc = None
        for g in range(N_BRANCHES):
            term = _dot(splits[g], expand_ref[:, cs]) * outs[g]
            acc = term if acc is None else acc + term
        cols.append(acc.astype(BF16))
    o = jnp.concatenate(cols, axis=-1)
    mod = mod_ref[0]
    out_ref[0] = x_ref[0] + mod[2:3] * _dot(o, wo_ref[...])


def _combine(qkv0, slopes0, outs, stats, x, mods, layer, w_o, w_o_layer, tm=512, group=8):
    bsz, s, d_model = x.shape
    dils = [d for _, d in BRANCHES]
    assert dils[0] == 1
    blk = ATTN_BLOCK
    n_sub = tm // blk
    res_major = lambda w: [pl.BlockSpec((None, d, tm // d, w), lambda b, i: (b, 0, i, 0)) for d in dils[1:]]
    cur = pl.BlockSpec((None, 1, tm // 2, D_ATTN), lambda b, i: (b, 0, i, 0))
    prev = pl.BlockSpec((None, 1, blk // 2, D_ATTN), lambda b, i: (b, 0, jnp.maximum(n_sub * i - 1, 0), 0))
    bias, ones = _attn_tables(0, slopes0)
    head_rows = np.zeros((LANES, D_ATTN), np.float32)
    head_rows[np.arange(D_ATTN) // HEAD_DIM, np.arange(D_ATTN)] = 1.0
    expand = jnp.asarray(np.concatenate([head_rows, head_rows]), dtype=BF16)
    q0, k0, v0 = qkv0
    return pl.pallas_call(
        functools.partial(_combine_kernel, tm=tm, group=group),
        grid=(bsz, s // tm),
        in_specs=[cur, prev, cur, prev, cur, _resident(bias.shape), _resident(ones.shape)]
        + res_major(D_ATTN) + res_major(LANES) + [
            pl.BlockSpec((1, tm, d_model), lambda b, i: (b, i, 0)),
            _mod_spec(layer, d_model),
            _resident_layer(w_o.shape, w_o_layer),
            _resident(expand.shape),
        ],
        out_specs=pl.BlockSpec((1, tm, d_model), lambda b, i: (b, i, 0)),
        out_shape=jax.ShapeDtypeStruct(x.shape, F32),
        scratch_shapes=[pltpu.VMEM((1, tm, D_ATTN), F32),
                        pltpu.VMEM((1, tm, LANES), F32),
                        pltpu.VMEM((2, group, blk, 2 * blk), F32),
                        pltpu.VMEM((2, group, blk, 2 * blk), BF16),
                        pltpu.VMEM((N_BRANCHES - 1, D_ATTN // LANES, tm, LANES), F32),
                        pltpu.VMEM((N_BRANCHES - 1, tm, LANES), F32)],
        compiler_params=_params("arbitrary", "arbitrary"),
        name="branch_mix_out_proj",
    )(q0, k0, k0, v0, v0, bias, ones, *outs, *stats, x, mods, w_o, expand)


def kernel(x, c, ada_w, ada_b, norm1_g, norm2_g, pool_w_in, pool_w_grp, pool_scale, pool_w_out,
           kv_norm_g, kv_ada_w, kv_ada_b, w_kv, attn_w_q, attn_w_o,
           ffn_w_up, ffn_conv_w, ffn_conv_b, ffn_w_down, final_g):
    bsz, s, d = x.shape
    depth = ada_w.shape[0]
    n_pool = pool_w_in.shape[0]
    slopes = _alibi_slopes(N_BRANCHES * N_HEADS).reshape(N_BRANCHES, N_HEADS)

    mods = _ada(c, ada_w, ada_b).reshape(depth, bsz, 6, d)
    kv_mod = _ada(c, kv_ada_w[None], kv_ada_b[None]).reshape(bsz, 2, 1, d)

    pool_w = [w.astype(BF16) for w in (pool_w_in, pool_w_grp, pool_w_out)]
    w_kv_b, w_q_b, w_o_b = w_kv[None].astype(BF16), attn_w_q.astype(BF16), attn_w_o.astype(BF16)
    w_up_b, w_down_b = ffn_w_up.astype(BF16), ffn_w_down.astype(BF16)

    ks = vs = None
    for layer in range(depth):
        if layer < n_pool:
            x = _pool_layer(x, mods, layer, norm1_g[:n_pool], pool_w[0], pool_w[1], pool_scale, pool_w[2])
        else:
            if layer == n_pool:
                kv = _proj(x, kv_mod[:, 0], kv_mod[:, 1], kv_norm_g, w_kv_b, 0)
                ks, vs = kv[:N_BRANCHES], kv[N_BRANCHES:]
            jl = layer - n_pool
            mod = mods[layer]
            q = _proj(x, mod[:, 0:1], mod[:, 1:2], norm1_g[layer], w_q_b, jl, out_scale=HEAD_DIM ** -0.5)
            outs, stats = zip(*[_attn_branch(q[g], ks[g], vs[g], g, slopes[g]) for g in range(1, N_BRANCHES)])
            x = _combine((q[0], ks[0], vs[0]), slopes[0], outs, stats, x, mods, layer, w_o_b, jl)
        x = _ffn_layer(x, mods, layer, norm2_g, w_up_b, ffn_conv_w, ffn_conv_b, w_down_b, final_g,
                       final_norm=(layer == depth - 1))
    return x
```

```python
import functools
import math

import jax
import jax.numpy as jnp
import numpy as np
from jax import lax
from jax.experimental import pallas as pl
from jax.experimental.pallas import tpu as pltpu

D_MODEL = 1024
POOL_WINDOWS = (2, 4, 8, 16)
POOL_GROUP_DIM = D_MODEL // len(POOL_WINDOWS)
BRANCHES = ((128, 1), (512, 4), (2048, 16))
N_BRANCHES = len(BRANCHES)
HEAD_DIM = 64
N_HEADS = D_MODEL // HEAD_DIM
D_ATTN = N_HEADS * HEAD_DIM
ATTN_BLOCK = 128
D_FF = 2816
CONV_WIDTH = 3
EPS = 1e-6

LANES = 128
CARRY_ROWS = 8
VMEM_LIMIT = 56 * 1024 * 1024

BF16 = jnp.bfloat16
F32 = jnp.float32


def _alibi_slopes(n):
    def pow2(m):
        start = 2.0 ** (-(2.0 ** -(math.log2(m) - 3)))
        return [start ** (i + 1) for i in range(m)]
    if math.log2(n).is_integer():
        s = pow2(n)
    else:
        c = 2 ** math.floor(math.log2(n))
        s = pow2(c) + pow2(2 * c)[0::2][: n - c]
    s = np.asarray(s, dtype=np.float32)
    return -np.sort(-s)


def _params(*sem):
    return pltpu.CompilerParams(dimension_semantics=sem, vmem_limit_bytes=VMEM_LIMIT)


def _resident(shape):
    nd = len(shape)
    return pl.BlockSpec(shape, lambda *_: (0,) * nd, pipeline_mode=pl.Buffered(1))


def _resident_layer(shape, layer):
    nd = len(shape) - 1
    return pl.BlockSpec((None,) + tuple(shape[1:]), lambda *_: (layer,) + (0,) * nd,
                        pipeline_mode=pl.Buffered(1))


def _mod_spec(layer, d):
    return pl.BlockSpec((None, 1, 6, d), lambda b, i: (layer, b, 0, 0))


def _norm_mod(x, g, shift, scale):
    y = x * lax.rsqrt(jnp.mean(x * x, axis=-1, keepdims=True) + EPS)
    return (y * g) * (1.0 + scale) + shift


def _dot(a, b):
    return jnp.dot(a, b, preferred_element_type=F32)


def _pack(rows_bf16):
    return pltpu.bitcast(rows_bf16, jnp.uint32)


def _unpack(words):
    return pltpu.bitcast(words, BF16)


def _ada_kernel(c_ref, w_ref, b_ref, o_ref):
    c = c_ref[...]
    cond = c * (1.0 / (1.0 + jnp.exp(-c)))
    o_ref[0] = _dot(cond.astype(BF16), w_ref[0].astype(BF16)) + b_ref[0]


def _ada(c, w, b, tn=1024):
    n_layers, d, n = w.shape
    bsz = c.shape[0]
    return pl.pallas_call(
        _ada_kernel,
        grid=(n_layers, n // tn),
        in_specs=[
            pl.BlockSpec((bsz, d), lambda l, j: (0, 0)),
            pl.BlockSpec((1, d, tn), lambda l, j: (l, 0, j)),
            pl.BlockSpec((1, 1, tn), lambda l, j: (l, 0, j)),
        ],
        out_specs=pl.BlockSpec((1, bsz, tn), lambda l, j: (l, 0, j)),
        out_shape=jax.ShapeDtypeStruct((n_layers, bsz, n), F32),
        compiler_params=_params("arbitrary", "arbitrary"),
        name="ada_mod",
    )(c, w, b.reshape(n_layers, 1, n))


def _pool_kernel(x_ref, mod_ref, g_ref, win_ref, wgrp_ref, scale_ref, wout_ref, o_ref,
                 s1_ref, s2_ref, s4_ref, s8_ref, *, tm, n_split):
    si = pl.program_id(1)
    c0 = CARRY_ROWS
    gd = POOL_GROUP_DIM
    stages = (s1_ref, s2_ref, s4_ref, s8_ref)

    @pl.when(si == 0)
    def _():
        for ref in stages:
            ref[0:c0, :] = jnp.zeros((c0, ref.shape[1]), F32)

    mod = mod_ref[0]
    th = tm // n_split
    for part in range(n_split):
        r0 = part * th
        b0 = c0 + r0
        x = x_ref[0, r0:r0 + th, :]
        h = _norm_mod(x, g_ref[...], mod[0:1], mod[1:2])
        u = _dot(h.astype(BF16), win_ref[...])
        s1_ref[b0:b0 + th, :] = u
        for k in range(1, len(stages)):
            prev, cur = stages[k - 1], stages[k]
            sh = 1 << (k - 1)
            cur[b0:b0 + th, :] = prev[b0:b0 + th, gd:] + prev[b0 - sh:b0 - sh + th, gd:]

        t = si * tm + r0 + lax.broadcasted_iota(jnp.int32, (th, 1), 0)
        ys = []
        for g, w in enumerate(POOL_WINDOWS):
            ref = stages[g]
            sh = w // 2
            wsum = ref[b0:b0 + th, 0:gd] + ref[b0 - sh:b0 - sh + th, 0:gd]
            count = jnp.minimum(t + 1, w).astype(F32)
            pooled = wsum / count - u[:, g * gd:(g + 1) * gd]
            ys.append(_dot(pooled.astype(BF16), wgrp_ref[g]))
        y = jnp.concatenate(ys, axis=-1) * scale_ref[...]
        y = _dot(y.astype(BF16), wout_ref[...])
        o_ref[0, r0:r0 + th, :] = x + mod[2:3] * y

    for ref in stages:
        ref[0:c0, :] = ref[tm:tm + c0, :]


def _pool_layer(x, mods, layer, g, w_in, w_grp, scale, w_out, tm=1024, n_split=4):
    bsz, s, d = x.shape
    gd = POOL_GROUP_DIM
    g, scale = g[:, None, :], scale[:, None, :]
    return pl.pallas_call(
        functools.partial(_pool_kernel, tm=tm, n_split=n_split),
        grid=(bsz, s // tm),
        in_specs=[
            pl.BlockSpec((1, tm, d), lambda b, i: (b, i, 0)),
            _mod_spec(layer, d),
            _resident_layer(g.shape, layer),
            _resident_layer(w_in.shape, layer),
            _resident_layer(w_grp.shape, layer),
            _resident_layer(scale.shape, layer),
            _resident_layer(w_out.shape, layer),
        ],
        out_specs=pl.BlockSpec((1, tm, d), lambda b, i: (b, i, 0)),
        out_shape=jax.ShapeDtypeStruct(x.shape, F32),
        scratch_shapes=[pltpu.VMEM((tm + CARRY_ROWS, d - k * gd), F32) for k in range(4)],
        compiler_params=_params("arbitrary", "arbitrary"),
        name="pool_mixer",
    )(x, mods, g, w_in, w_grp, scale, w_out)


def _ffn_kernel(x_ref, mod_ref, g_ref, wup_ref, cw_ref, cb_ref, wdown_ref, fg_ref, o_ref,
                a_ref, *, tm, final_norm):
    si = pl.program_id(1)
    c0 = CARRY_ROWS
    f = D_FF

    @pl.when(si == 0)
    def _():
        a_ref[0:c0, :] = jnp.zeros((c0, f), F32)

    x = x_ref[0]
    mod = mod_ref[0]
    h = _norm_mod(x, g_ref[...], mod[3:4], mod[4:5]).astype(BF16)
    a = _dot(h, wup_ref[:, 0:f])
    v = _dot(h, wup_ref[:, f:2 * f])
    a_ref[c0:c0 + tm, :] = a
    y = cb_ref[...]
    for k in range(CONV_WIDTH):
        lag = CONV_WIDTH - 1 - k
        tap = a if lag == 0 else a_ref[c0 - lag:c0 - lag + tm, :]
        y = y + tap * cw_ref[k:k + 1, :]
    gated = y * (1.0 / (1.0 + jnp.exp(-y))) * v
    out = x + mod[5:6] * _dot(gated.astype(BF16), wdown_ref[...])
    if final_norm:
        out = (out * lax.rsqrt(jnp.mean(out * out, axis=-1, keepdims=True) + EPS)) * fg_ref[...]
    o_ref[0] = out
    a_ref[0:c0, :] = a_ref[tm:tm + c0, :]


def _ffn_layer(x, mods, layer, g, w_up, conv_w, conv_b, w_down, final_g, final_norm, tm=512):
    bsz, s, d = x.shape
    f = D_FF
    g, conv_b = g[:, None, :], conv_b[:, None, :]
    return pl.pallas_call(
        functools.partial(_ffn_kernel, tm=tm, final_norm=final_norm),
        grid=(bsz, s // tm),
        in_specs=[
            pl.BlockSpec((1, tm, d), lambda b, i: (b, i, 0)),
            _mod_spec(layer, d),
            _resident_layer(g.shape, layer),
            _resident_layer(w_up.shape, layer),
            _resident_layer(conv_w.shape, layer),
            _resident_layer(conv_b.shape, layer),
            _resident_layer(w_down.shape, layer),
            _resident((1, d)),
        ],
        out_specs=pl.BlockSpec((1, tm, d), lambda b, i: (b, i, 0)),
        out_shape=jax.ShapeDtypeStruct(x.shape, F32),
        scratch_shapes=[pltpu.VMEM((tm + CARRY_ROWS, f), F32)],
        compiler_params=_params("arbitrary", "arbitrary"),
        name="conv_ffn",
    )(x, mods, g, w_up, conv_w, conv_b, w_down, final_g.reshape(1, d))


def _proj_kernel(x_ref, shift_ref, scale_ref, g_ref, w_ref, *rest, tm, n_tensors, out_scale):
    o_refs, h_ref = rest[:-1], rest[-1]
    h = _norm_mod(x_ref[0], g_ref[...], shift_ref[0], scale_ref[0])
    n_slabs = D_MODEL // LANES
    for c in range(n_slabs):
        h_ref[c] = h[:, c * LANES:(c + 1) * LANES]
    for g, (_, d) in enumerate(BRANCHES):
        n = tm // d
        if d == 1:
            hp = h.astype(BF16)
        else:
            hp = jnp.concatenate(
                [jnp.concatenate([h_ref[c, pl.ds(r, n, stride=d), :] for c in range(n_slabs)], axis=1)
                 for r in range(d)], axis=0).astype(BF16)
        for t in range(n_tensors):
            col = (t * N_BRANCHES + g) * D_ATTN
            res = _dot(hp, w_ref[:, col:col + D_ATTN])
            if out_scale != 1.0:
                res = res * out_scale
            o_ref = o_refs[t * N_BRANCHES + g]
            for r in range(d):
                o_ref[0, r] = _pack(res[r * n:(r + 1) * n].astype(BF16))


def _proj(x, shift, scale, g, w, layer, out_scale=1.0, tm=512):
    bsz, s, d_model = x.shape
    n_tensors = w.shape[2] // (N_BRANCHES * D_ATTN)
    dils = [d for _, d in BRANCHES] * n_tensors
    return pl.pallas_call(
        functools.partial(_proj_kernel, tm=tm, n_tensors=n_tensors, out_scale=out_scale),
        grid=(bsz, s // tm),
        in_specs=[
            pl.BlockSpec((1, tm, d_model), lambda b, i: (b, i, 0)),
            pl.BlockSpec((1, 1, d_model), lambda b, i: (b, 0, 0)),
            pl.BlockSpec((1, 1, d_model), lambda b, i: (b, 0, 0)),
            _resident((1, d_model)),
            _resident_layer(w.shape, layer),
        ],
        out_specs=[pl.BlockSpec((1, d, tm // d // 2, D_ATTN), lambda b, i: (b, 0, i, 0)) for d in dils],
        out_shape=[jax.ShapeDtypeStruct((bsz, d, s // d // 2, D_ATTN), jnp.uint32) for d in dils],
        scratch_shapes=[pltpu.VMEM((d_model // LANES, tm, LANES), F32)],
        compiler_params=_params("arbitrary", "arbitrary"),
        name="norm_proj",
    )(x, shift, scale, g.reshape(1, d_model), w)


def _bias_table(slopes, dilation, n_steps):
    blk = ATTN_BLOCK
    row = np.arange(blk)[:, None]
    col = np.arange(2 * blk)[None, :]
    delta = row + blk - col
    valid = (delta >= 0) & (delta <= n_steps)
    dist = (delta * dilation).astype(np.float32)
    bias = -np.asarray(slopes, np.float32)[:, None, None] * dist[None]
    later = np.where(valid[None], bias, -np.inf).astype(np.float32)
    first = np.where((valid & (col >= blk))[None], bias, -np.inf).astype(np.float32)
    return np.stack([first, later])


def _ones_table():
    blk = ATTN_BLOCK
    t = np.zeros((N_HEADS // 2, 4 * blk, LANES), np.float32)
    for p in range(N_HEADS // 2):
        t[p, :2 * blk, N_HEADS + 2 * p] = 1.0
        t[p, 2 * blk:, N_HEADS + 2 * p + 1] = 1.0
    return t


def _attend(q_ref, kp_ref, kc_ref, vp_ref, vc_ref, bias_ref, ones_ref, o_ref, st_ref, s_scr, p_scr,
            *, first, n_res, n_sub, group):
    blk = ATTN_BLOCK
    lane = lax.broadcasted_iota(jnp.int32, (blk, LANES), 1)
    low_half = lane < HEAD_DIM
    half_blk = blk // 2

    def solo(sb):
        return sb == 0 and first is None

    def keys(res, sb, prev_ref, cur_ref, cs):
        if solo(sb):
            return _unpack(cur_ref[res, 0:half_blk, cs])
        if sb == 0:
            return jnp.concatenate(
                [_unpack(prev_ref[res, :, cs]), _unpack(cur_ref[res, 0:half_blk, cs])], axis=0)
        return _unpack(cur_ref[res, (sb - 1) * half_blk:(sb + 1) * half_blk, cs])

    units = [(res, sb, g0) for res in range(n_res) for sb in range(n_sub)
             for g0 in range(0, N_HEADS, group)]
    unit_slot = {u: i % 2 for i, u in enumerate(units)}
    ms = {}
    m_mats = {(res, sb): jnp.zeros((blk, LANES), F32) for res in range(n_res) for sb in range(n_sub)}
    l_mats = dict(m_mats)

    def scores(res, sb, g0):
        variant = first if sb == 0 else 1
        kw = blk if solo(sb) else 2 * blk
        slot = unit_slot[res, sb, g0]
        for pair in range(g0 // 2, (g0 + group) // 2):
            cs = slice(pair * LANES, (pair + 1) * LANES)
            q = _unpack(q_ref[res, sb * half_blk:(sb + 1) * half_blk, cs])
            zq = jnp.zeros_like(q)
            q2 = jnp.concatenate([jnp.where(low_half, q, zq), jnp.where(low_half, zq, q)], axis=0)
            s2 = lax.dot_general(q2, keys(res, sb, kp_ref, kc_ref, cs), (((1,), (1,)), ((), ())),
                                 preferred_element_type=F32)
            for half in range(2):
                head = 2 * pair + half
                bias = bias_ref[1, head, :, blk:] if solo(sb) else bias_ref[variant, head]
                s = s2[half * blk:(half + 1) * blk] + bias
                s_scr[slot, head - g0, :, 0:kw] = s
                ms[res, sb, head] = jnp.max(s, axis=-1, keepdims=True)

    def probs(res, sb, g0):
        slot = unit_slot[res, sb, g0]
        kw = blk if solo(sb) else 2 * blk
        for head in range(g0, g0 + group):
            p_scr[slot, head - g0, :, 0:kw] = jnp.exp(
                s_scr[slot, head - g0, :, 0:kw] - ms[res, sb, head]).astype(BF16)

    def values(res, sb, g0):
        slot = unit_slot[res, sb, g0]
        rows = slice(sb * blk, (sb + 1) * blk)
        for pair in range(g0 // 2, (g0 + group) // 2):
            cs = slice(pair * LANES, (pair + 1) * LANES)
            v = keys(res, sb, vp_ref, vc_ref, cs)
            kw = v.shape[0]
            zv = jnp.zeros_like(v)
            keep = lax.broadcasted_iota(jnp.int32, (kw, LANES), 1) < HEAD_DIM
            v2 = jnp.concatenate([jnp.where(keep, v, zv), jnp.where(keep, zv, v)], axis=0)
            ones = jnp.concatenate([ones_ref[pair, 0:kw], ones_ref[pair, 2 * blk:2 * blk + kw]], axis=0)
            v2 = jnp.concatenate([v2, ones], axis=1)
            p2 = jnp.concatenate([p_scr[slot, 2 * pair - g0, :, 0:kw],
                                  p_scr[slot, 2 * pair + 1 - g0, :, 0:kw]], axis=1)
            u = _dot(p2, v2)
            o_ref[res, rows, cs] = u[:, :LANES]
            l_mats[res, sb] = l_mats[res, sb] + u[:, LANES:]
            for head in (2 * pair, 2 * pair + 1):
                m_mats[res, sb] = jnp.where(lane == head, ms[res, sb, head], m_mats[res, sb])

    scores(*units[0])
    for i, unit in enumerate(units):
        if i + 1 < len(units):
            scores(*units[i + 1])
        probs(*unit)
        values(*unit)
    for (res, sb), m_mat in m_mats.items():
        st_ref[res, sb * blk:(sb + 1) * blk, :] = m_mat + l_mats[res, sb]


def _attn_kernel(*refs, n_res, n_sub, group, single_step):
    if single_step:
        q_ref, kc_ref, vc_ref = refs[:3]
        _attend(q_ref, None, kc_ref, None, vc_ref, *refs[3:], first=None, n_res=n_res, n_sub=n_sub, group=group)
    else:
        _attend(*refs, first=jnp.minimum(pl.program_id(2), 1), n_res=n_res, n_sub=n_sub, group=group)


def _attn_tables(branch, slopes):
    window, d = BRANCHES[branch]
    assert window // d <= ATTN_BLOCK and 2 * N_HEADS <= LANES
    return jnp.asarray(_bias_table(slopes, d, window // d)), jnp.asarray(_ones_table(), dtype=BF16)


def _attn_branch(q, k, v, branch, slopes, blocks_per_step=8, group=8):
    window, d = BRANCHES[branch]
    bsz, _, packed_sub, _ = q.shape
    sub = 2 * packed_sub
    blk = ATTN_BLOCK
    n_sub = min(blocks_per_step, sub // blk)
    n_res = min(blocks_per_step // n_sub, d)
    qb = n_sub * blk
    assert sub % qb == 0 and d % n_res == 0
    bias, ones = _attn_tables(branch, slopes)
    cur = lambda rows, w: pl.BlockSpec((None, n_res, rows, w), lambda b, r, j: (b, r, j, 0))
    prev = pl.BlockSpec((None, n_res, blk // 2, D_ATTN),
                        lambda b, r, j: (b, r, jnp.maximum(n_sub * j - 1, 0), 0))
    cur_in = cur(qb // 2, D_ATTN)
    single_step = sub == qb
    if single_step:
        qkv_specs, qkv = [cur_in, cur_in, cur_in], (q, k, v)
    else:
        qkv_specs, qkv = [cur_in, prev, cur_in, prev, cur_in], (q, k, k, v, v)
    return pl.pallas_call(
        functools.partial(_attn_kernel, n_res=n_res, n_sub=n_sub, group=group, single_step=single_step),
        grid=(bsz, d // n_res, sub // qb),
        in_specs=qkv_specs + [_resident(bias.shape), _resident(ones.shape)],
        out_specs=[cur(qb, D_ATTN), cur(qb, LANES)],
        out_shape=[
            jax.ShapeDtypeStruct((bsz, d, sub, D_ATTN), F32),
            jax.ShapeDtypeStruct((bsz, d, sub, LANES), F32),
        ],
        scratch_shapes=[pltpu.VMEM((2, group, blk, 2 * blk), F32),
                        pltpu.VMEM((2, group, blk, 2 * blk), BF16)],
        compiler_params=_params("arbitrary", "arbitrary", "arbitrary"),
        name=f"dilated_attn_{branch}",
    )(*qkv, bias, ones)


def _combine_kernel(q_ref, kp_ref, kc_ref, vp_ref, vc_ref, bias_ref, ones_ref,
                    o1_ref, o2_ref, l1_ref, l2_ref, x_ref, mod_ref, wo_ref, expand_ref,
                    out_ref, o0_ref, l0_ref, s_scr, p_scr, nat_o_ref, nat_l_ref, *, tm, group):
    _attend(q_ref, kp_ref, kc_ref, vp_ref, vc_ref, bias_ref, ones_ref, o0_ref, l0_ref, s_scr, p_scr,
            first=jnp.minimum(pl.program_id(1), 1), n_res=1, n_sub=tm // ATTN_BLOCK, group=group)
    o_refs = (o0_ref, o1_ref, o2_ref)
    l_refs = (l0_ref, l1_ref, l2_ref)
    n_slabs = D_ATTN // LANES
    for g, (_, d) in enumerate(BRANCHES):
        if d == 1:
            continue
        n = tm // d
        for r in range(d):
            nat_l_ref[g - 1, pl.ds(r, n, stride=d), :] = l_refs[g][r]
            for c in range(n_slabs):
                nat_o_ref[g - 1, c, pl.ds(r, n, stride=d), :] = o_refs[g][r, :, c * LANES:(c + 1) * LANES]
    stats = [l0_ref[0], nat_l_ref[0], nat_l_ref[1]]
    m = jnp.maximum(jnp.maximum(stats[0], stats[1]), stats[2])
    es = [jnp.exp(st - m) for st in stats]
    sums = [pltpu.roll(st, LANES - N_HEADS, axis=1) for st in stats]
    inv = 1.0 / (es[0] * sums[0] + es[1] * sums[1] + es[2] * sums[2])
    wts = [e * inv for e in es]
    head_lanes = lax.broadcasted_iota(jnp.int32, (tm, LANES), 1) < N_HEADS
    splits = []
    for w in wts:
        w = jnp.where(head_lanes, w, 0.0)
        hi = w.astype(BF16)
        lo = (w - hi.astype(F32)).astype(BF16)
        splits.append(jnp.concatenate([hi, lo], axis=1))
    cols = []
    wide = 2 * LANES
    for c in range(D_ATTN // wide):
        cs = slice(c * wide, (c + 1) * wide)
        outs = [o0_ref[0, :, cs]] + [
            jnp.concatenate([nat_o_ref[g, 2 * c], nat_o_ref[g, 2 * c + 1]], axis=1) for g in range(2)]
        acc = None
        for g in range(N_BRANCHES):
            term = _dot(splits[g], expand_ref[:, cs]) * outs[g]
            acc = term if acc is None else acc + term
        cols.append(acc.astype(BF16))
    o = jnp.concatenate(cols, axis=-1)
    mod = mod_ref[0]
    out_ref[0] = x_ref[0] + mod[2:3] * _dot(o, wo_ref[...])


def _combine(qkv0, slopes0, outs, stats, x, mods, layer, w_o, w_o_layer, tm=512, group=8):
    bsz, s, d_model = x.shape
    dils = [d for _, d in BRANCHES]
    assert dils[0] == 1
    blk = ATTN_BLOCK
    n_sub = tm // blk
    res_major = lambda w: [pl.BlockSpec((None, d, tm // d, w), lambda b, i: (b, 0, i, 0)) for d in dils[1:]]
    cur = pl.BlockSpec((None, 1, tm // 2, D_ATTN), lambda b, i: (b, 0, i, 0))
    prev = pl.BlockSpec((None, 1, blk // 2, D_ATTN), lambda b, i: (b, 0, jnp.maximum(n_sub * i - 1, 0), 0))
    bias, ones = _attn_tables(0, slopes0)
    head_rows = np.zeros((LANES, D_ATTN), np.float32)
    head_rows[np.arange(D_ATTN) // HEAD_DIM, np.arange(D_ATTN)] = 1.0
    expand = jnp.asarray(np.concatenate([head_rows, head_rows]), dtype=BF16)
    q0, k0, v0 = qkv0
    return pl.pallas_call(
        functools.partial(_combine_kernel, tm=tm, group=group),
        grid=(bsz, s // tm),
        in_specs=[cur, prev, cur, prev, cur, _resident(bias.shape), _resident(ones.shape)]
        + res_major(D_ATTN) + res_major(LANES) + [
            pl.BlockSpec((1, tm, d_model), lambda b, i: (b, i, 0)),
            _mod_spec(layer, d_model),
            _resident_layer(w_o.shape, w_o_layer),
            _resident(expand.shape),
        ],
        out_specs=pl.BlockSpec((1, tm, d_model), lambda b, i: (b, i, 0)),
        out_shape=jax.ShapeDtypeStruct(x.shape, F32),
        scratch_shapes=[pltpu.VMEM((1, tm, D_ATTN), F32),
                        pltpu.VMEM((1, tm, LANES), F32),
                        pltpu.VMEM((2, group, blk, 2 * blk), F32),
                        pltpu.VMEM((2, group, blk, 2 * blk), BF16),
                        pltpu.VMEM((N_BRANCHES - 1, D_ATTN // LANES, tm, LANES), F32),
                        pltpu.VMEM((N_BRANCHES - 1, tm, LANES), F32)],
        compiler_params=_params("arbitrary", "arbitrary"),
        name="branch_mix_out_proj",
    )(q0, k0, k0, v0, v0, bias, ones, *outs, *stats, x, mods, w_o, expand)


def kernel(x, c, ada_w, ada_b, norm1_g, norm2_g, pool_w_in, pool_w_grp, pool_scale, pool_w_out,
           kv_norm_g, kv_ada_w, kv_ada_b, w_kv, attn_w_q, attn_w_o,
           ffn_w_up, ffn_conv_w, ffn_conv_b, ffn_w_down, final_g):
    bsz, s, d = x.shape
    depth = ada_w.shape[0]
    n_pool = pool_w_in.shape[0]
    slopes = _alibi_slopes(N_BRANCHES * N_HEADS).reshape(N_BRANCHES, N_HEADS)

    mods = _ada(c, ada_w, ada_b).reshape(depth, bsz, 6, d)
    kv_mod = _ada(c, kv_ada_w[None], kv_ada_b[None]).reshape(bsz, 2, 1, d)

    pool_w = [w.astype(BF16) for w in (pool_w_in, pool_w_grp, pool_w_out)]
    w_kv_b, w_q_b, w_o_b = w_kv[None].astype(BF16), attn_w_q.astype(BF16), attn_w_o.astype(BF16)
    w_up_b, w_down_b = ffn_w_up.astype(BF16), ffn_w_down.astype(BF16)

    ks = vs = None
    for layer in range(depth):
        if layer < n_pool:
            x = _pool_layer(x, mods, layer, norm1_g[:n_pool], pool_w[0], pool_w[1], pool_scale, pool_w[2])
        else:
            if layer == n_pool:
                kv = _proj(x, kv_mod[:, 0], kv_mod[:, 1], kv_norm_g, w_kv_b, 0)
                ks, vs = kv[:N_BRANCHES], kv[N_BRANCHES:]
            jl = layer - n_pool
            mod = mods[layer]
            q = _proj(x, mod[:, 0:1], mod[:, 1:2], norm1_g[layer], w_q_b, jl, out_scale=HEAD_DIM ** -0.5)
            outs, stats = zip(*[_attn_branch(q[g], ks[g], vs[g], g, slopes[g]) for g in range(1, N_BRANCHES)])
            x = _combine((q[0], ks[0], vs[0]), slopes[0], outs, stats, x, mods, layer, w_o_b, jl)
        x = _ffn_layer(x, mods, layer, norm2_g, w_up_b, ffn_conv_w, ffn_conv_b, w_down_b, final_g,
                       final_norm=(layer == depth - 1))
    return x
```

```python
import functools
import math

import jax
import jax.numpy as jnp
import numpy as np
from jax import lax
from jax.experimental import pallas as pl
from jax.experimental.pallas import tpu as pltpu

D_MODEL = 1024
POOL_WINDOWS = (2, 4, 8, 16)
POOL_GROUP_DIM = D_MODEL // len(POOL_WINDOWS)
BRANCHES = ((128, 1), (512, 4), (2048, 16))
N_BRANCHES = len(BRANCHES)
HEAD_DIM = 64
N_HEADS = D_MODEL // HEAD_DIM
D_ATTN = N_HEADS * HEAD_DIM
ATTN_BLOCK = 128
D_FF = 2816
CONV_WIDTH = 3
EPS = 1e-6

LANES = 128
CARRY_ROWS = 8
VMEM_LIMIT = 56 * 1024 * 1024

BF16 = jnp.bfloat16
F32 = jnp.float32


def _alibi_slopes(n):
    def pow2(m):
        start = 2.0 ** (-(2.0 ** -(math.log2(m) - 3)))
        return [start ** (i + 1) for i in range(m)]
    if math.log2(n).is_integer():
        s = pow2(n)
    else:
        c = 2 ** math.floor(math.log2(n))
        s = pow2(c) + pow2(2 * c)[0::2][: n - c]
    s = np.asarray(s, dtype=np.float32)
    return -np.sort(-s)


def _params(*sem):
    return pltpu.CompilerParams(dimension_semantics=sem, vmem_limit_bytes=VMEM_LIMIT)


def _resident(shape):
    nd = len(shape)
    return pl.BlockSpec(shape, lambda *_: (0,) * nd, pipeline_mode=pl.Buffered(1))


def _resident_layer(shape, layer):
    nd = len(shape) - 1
    return pl.BlockSpec((None,) + tuple(shape[1:]), lambda *_: (layer,) + (0,) * nd,
                        pipeline_mode=pl.Buffered(1))


def _mod_spec(layer, d):
    return pl.BlockSpec((None, 1, 6, d), lambda b, i: (layer, b, 0, 0))


def _norm_mod(x, g, shift, scale):
    y = x * lax.rsqrt(jnp.mean(x * x, axis=-1, keepdims=True) + EPS)
    return (y * g) * (1.0 + scale) + shift


def _dot(a, b):
    return jnp.dot(a, b, preferred_element_type=F32)


def _pack(rows_bf16):
    return pltpu.bitcast(rows_bf16, jnp.uint32)


def _unpack(words):
    return pltpu.bitcast(words, BF16)


def _ada_kernel(c_ref, w_ref, b_ref, o_ref):
    c = c_ref[...]
    cond = c * (1.0 / (1.0 + jnp.exp(-c)))
    o_ref[0] = _dot(cond.astype(BF16), w_ref[0].astype(BF16)) + b_ref[0]


def _ada(c, w, b, tn=1024):
    n_layers, d, n = w.shape
    bsz = c.shape[0]
    return pl.pallas_call(
        _ada_kernel,
        grid=(n_layers, n // tn),
        in_specs=[
            pl.BlockSpec((bsz, d), lambda l, j: (0, 0)),
            pl.BlockSpec((1, d, tn), lambda l, j: (l, 0, j)),
            pl.BlockSpec((1, 1, tn), lambda l, j: (l, 0, j)),
        ],
        out_specs=pl.BlockSpec((1, bsz, tn), lambda l, j: (l, 0, j)),
        out_shape=jax.ShapeDtypeStruct((n_layers, bsz, n), F32),
        compiler_params=_params("arbitrary", "arbitrary"),
        name="ada_mod",
    )(c, w, b.reshape(n_layers, 1, n))


def _pool_kernel(x_ref, mod_ref, g_ref, win_ref, wgrp_ref, scale_ref, wout_ref, o_ref,
                 s1_ref, s2_ref, s4_ref, s8_ref, *, tm, n_split):
    si = pl.program_id(1)
    c0 = CARRY_ROWS
    gd = POOL_GROUP_DIM
    stages = (s1_ref, s2_ref, s4_ref, s8_ref)

    @pl.when(si == 0)
    def _():
        for ref in stages:
            ref[0:c0, :] = jnp.zeros((c0, ref.shape[1]), F32)

    mod = mod_ref[0]
    th = tm // n_split
    for part in range(n_split):
        r0 = part * th
        b0 = c0 + r0
        x = x_ref[0, r0:r0 + th, :]
        h = _norm_mod(x, g_ref[...], mod[0:1], mod[1:2])
        u = _dot(h.astype(BF16), win_ref[...])
        s1_ref[b0:b0 + th, :] = u
        for k in range(1, len(stages)):
            prev, cur = stages[k - 1], stages[k]
            sh = 1 << (k - 1)
            cur[b0:b0 + th, :] = prev[b0:b0 + th, gd:] + prev[b0 - sh:b0 - sh + th, gd:]

        t = si * tm + r0 + lax.broadcasted_iota(jnp.int32, (th, 1), 0)
        ys = []
        for g, w in enumerate(POOL_WINDOWS):
            ref = stages[g]
            sh = w // 2
            wsum = ref[b0:b0 + th, 0:gd] + ref[b0 - sh:b0 - sh + th, 0:gd]
            count = jnp.minimum(t + 1, w).astype(F32)
            pooled = wsum / count - u[:, g * gd:(g + 1) * gd]
            ys.append(_dot(pooled.astype(BF16), wgrp_ref[g]))
        y = jnp.concatenate(ys, axis=-1) * scale_ref[...]
        y = _dot(y.astype(BF16), wout_ref[...])
        o_ref[0, r0:r0 + th, :] = x + mod[2:3] * y

    for ref in stages:
        ref[0:c0, :] = ref[tm:tm + c0, :]


def _pool_layer(x, mods, layer, g, w_in, w_grp, scale, w_out, tm=1024, n_split=4):
    bsz, s, d = x.shape
    gd = POOL_GROUP_DIM
    g, scale = g[:, None, :], scale[:, None, :]
    return pl.pallas_call(
        functools.partial(_pool_kernel, tm=tm, n_split=n_split),
        grid=(bsz, s // tm),
        in_specs=[
            pl.BlockSpec((1, tm, d), lambda b, i: (b, i, 0)),
            _mod_spec(layer, d),
            _resident_layer(g.shape, layer),
            _resident_layer(w_in.shape, layer),
            _resident_layer(w_grp.shape, layer),
            _resident_layer(scale.shape, layer),
            _resident_layer(w_out.shape, layer),
        ],
        out_specs=pl.BlockSpec((1, tm, d), lambda b, i: (b, i, 0)),
        out_shape=jax.ShapeDtypeStruct(x.shape, F32),
        scratch_shapes=[pltpu.VMEM((tm + CARRY_ROWS, d - k * gd), F32) for k in range(4)],
        compiler_params=_params("arbitrary", "arbitrary"),
        name="pool_mixer",
    )(x, mods, g, w_in, w_grp, scale, w_out)


def _ffn_kernel(x_ref, mod_ref, g_ref, wup_ref, cw_ref, cb_ref, wdown_ref, fg_ref, o_ref,
                a_ref, *, tm, final_norm):
    si = pl.program_id(1)
    c0 = CARRY_ROWS
    f = D_FF

    @pl.when(si == 0)
    def _():
        a_ref[0:c0, :] = jnp.zeros((c0, f), F32)

    x = x_ref[0]
    mod = mod_ref[0]
    h = _norm_mod(x, g_ref[...], mod[3:4], mod[4:5]).astype(BF16)
    a = _dot(h, wup_ref[:, 0:f])
    v = _dot(h, wup_ref[:, f:2 * f])
    a_ref[c0:c0 + tm, :] = a
    y = cb_ref[...]
    for k in range(CONV_WIDTH):
        lag = CONV_WIDTH - 1 - k
        tap = a if lag == 0 else a_ref[c0 - lag:c0 - lag + tm, :]
        y = y + tap * cw_ref[k:k + 1, :]
    gated = y * (1.0 / (1.0 + jnp.exp(-y))) * v
    out = x + mod[5:6] * _dot(gated.astype(BF16), wdown_ref[...])
    if final_norm:
        out = (out * lax.rsqrt(jnp.mean(out * out, axis=-1, keepdims=True) + EPS)) * fg_ref[...]
    o_ref[0] = out
    a_ref[0:c0, :] = a_ref[tm:tm + c0, :]


def _ffn_layer(x, mods, layer, g, w_up, conv_w, conv_b, w_down, final_g, final_norm, tm=512):
    bsz, s, d = x.shape
    f = D_FF
    g, conv_b = g[:, None, :], conv_b[:, None, :]
    return pl.pallas_call(
        functools.partial(_ffn_kernel, tm=tm, final_norm=final_norm),
        grid=(bsz, s // tm),
        in_specs=[
            pl.BlockSpec((1, tm, d), lambda b, i: (b, i, 0)),
            _mod_spec(layer, d),
            _resident_layer(g.shape, layer),
            _resident_layer(w_up.shape, layer),
            _resident_layer(conv_w.shape, layer),
            _resident_layer(conv_b.shape, layer),
            _resident_layer(w_down.shape, layer),
            _resident((1, d)),
        ],
        out_specs=pl.BlockSpec((1, tm, d), lambda b, i: (b, i, 0)),
        out_shape=jax.ShapeDtypeStruct(x.shape, F32),
        scratch_shapes=[pltpu.VMEM((tm + CARRY_ROWS, f), F32)],
        compiler_params=_params("arbitrary", "arbitrary"),
        name="conv_ffn",
    )(x, mods, g, w_up, conv_w, conv_b, w_down, final_g.reshape(1, d))


def _proj_kernel(x_ref, shift_ref, scale_ref, g_ref, w_ref, *rest, tm, n_tensors, out_scale):
    o_refs, h_ref = rest[:-1], rest[-1]
    h = _norm_mod(x_ref[0], g_ref[...], shift_ref[0], scale_ref[0])
    n_slabs = D_MODEL // LANES
    for c in range(n_slabs):
        h_ref[c] = h[:, c * LANES:(c + 1) * LANES]
    for g, (_, d) in enumerate(BRANCHES):
        n = tm // d
        if d == 1:
            hp = h.astype(BF16)
        else:
            hp = jnp.concatenate(
                [jnp.concatenate([h_ref[c, pl.ds(r, n, stride=d), :] for c in range(n_slabs)], axis=1)
                 for r in range(d)], axis=0).astype(BF16)
        for t in range(n_tensors):
            col = (t * N_BRANCHES + g) * D_ATTN
            res = _dot(hp, w_ref[:, col:col + D_ATTN])
            if out_scale != 1.0:
                res = res * out_scale
            o_ref = o_refs[t * N_BRANCHES + g]
            for r in range(d):
                o_ref[0, r] = _pack(res[r * n:(r + 1) * n].astype(BF16))


def _proj(x, shift, scale, g, w, layer, out_scale=1.0, tm=512):
    bsz, s, d_model = x.shape
    n_tensors = w.shape[2] // (N_BRANCHES * D_ATTN)
    dils = [d for _, d in BRANCHES] * n_tensors
    return pl.pallas_call(
        functools.partial(_proj_kernel, tm=tm, n_tensors=n_tensors, out_scale=out_scale),
        grid=(bsz, s // tm),
        in_specs=[
            pl.BlockSpec((1, tm, d_model), lambda b, i: (b, i, 0)),
            pl.BlockSpec((1, 1, d_model), lambda b, i: (b, 0, 0)),
            pl.BlockSpec((1, 1, d_model), lambda b, i: (b, 0, 0)),
            _resident((1, d_model)),
            _resident_layer(w.shape, layer),
        ],
        out_specs=[pl.BlockSpec((1, d, tm // d // 2, D_ATTN), lambda b, i: (b, 0, i, 0)) for d in dils],
        out_shape=[jax.ShapeDtypeStruct((bsz, d, s // d // 2, D_ATTN), jnp.uint32) for d in dils],
        scratch_shapes=[pltpu.VMEM((d_model // LANES, tm, LANES), F32)],
        compiler_params=_params("arbitrary", "arbitrary"),
        name="norm_proj",
    )(x, shift, scale, g.reshape(1, d_model), w)


def _bias_table(slopes, dilation, n_steps):
    blk = ATTN_BLOCK
    row = np.arange(blk)[:, None]
    col = np.arange(2 * blk)[None, :]
    delta = row + blk - col
    valid = (delta >= 0) & (delta <= n_steps)
    dist = (delta * dilation).astype(np.float32)
    bias = -np.asarray(slopes, np.float32)[:, None, None] * dist[None]
    later = np.where(valid[None], bias, -np.inf).astype(np.float32)
    first = np.where((valid & (col >= blk))[None], bias, -np.inf).astype(np.float32)
    return np.stack([first, later])


def _ones_table():
    blk = ATTN_BLOCK
    t = np.zeros((N_HEADS // 2, 4 * blk, LANES), np.float32)
    for p in range(N_HEADS // 2):
        t[p, :2 * blk, N_HEADS + 2 * p] = 1.0
        t[p, 2 * blk:, N_HEADS + 2 * p + 1] = 1.0
    return t


def _attend(q_ref, kp_ref, kc_ref, vp_ref, vc_ref, bias_ref, ones_ref, o_ref, st_ref, s_scr, p_scr,
            *, first, n_res, n_sub, group):
    blk = ATTN_BLOCK
    lane = lax.broadcasted_iota(jnp.int32, (blk, LANES), 1)
    low_half = lane < HEAD_DIM
    half_blk = blk // 2

    def solo(sb):
        return sb == 0 and first is None

    def keys(res, sb, prev_ref, cur_ref, cs):
        if solo(sb):
            return _unpack(cur_ref[res, 0:half_blk, cs])
        if sb == 0:
            return jnp.concatenate(
                [_unpack(prev_ref[res, :, cs]), _unpack(cur_ref[res, 0:half_blk, cs])], axis=0)
        return _unpack(cur_ref[res, (sb - 1) * half_blk:(sb + 1) * half_blk, cs])

    units = [(res, sb, g0) for res in range(n_res) for sb in range(n_sub)
             for g0 in range(0, N_HEADS, group)]
    unit_slot = {u: i % 2 for i, u in enumerate(units)}
    ms = {}
    m_mats = {(res, sb): jnp.zeros((blk, LANES), F32) for res in range(n_res) for sb in range(n_sub)}
    l_mats = dict(m_mats)

    def scores(res, sb, g0):
        variant = first if sb == 0 else 1
        kw = blk if solo(sb) else 2 * blk
        slot = unit_slot[res, sb, g0]
        for pair in range(g0 // 2, (g0 + group) // 2):
            cs = slice(pair * LANES, (pair + 1) * LANES)
            q = _unpack(q_ref[res, sb * half_blk:(sb + 1) * half_blk, cs])
            zq = jnp.zeros_like(q)
            q2 = jnp.concatenate([jnp.where(low_half, q, zq), jnp.where(low_half, zq, q)], axis=0)
            s2 = lax.dot_general(q2, keys(res, sb, kp_ref, kc_ref, cs), (((1,), (1,)), ((), ())),
                                 preferred_element_type=F32)
            for half in range(2):
                head = 2 * pair + half
                bias = bias_ref[1, head, :, blk:] if solo(sb) else bias_ref[variant, head]
                s = s2[half * blk:(half + 1) * blk] + bias
                s_scr[slot, head - g0, :, 0:kw] = s
                ms[res, sb, head] = jnp.max(s, axis=-1, keepdims=True)

    def probs(res, sb, g0):
        slot = unit_slot[res, sb, g0]
        kw = blk if solo(sb) else 2 * blk
        for head in range(g0, g0 + group):
            p_scr[slot, head - g0, :, 0:kw] = jnp.exp(
                s_scr[slot, head - g0, :, 0:kw] - ms[res, sb, head]).astype(BF16)

    def values(res, sb, g0):
        slot = unit_slot[res, sb, g0]
        rows = slice(sb * blk, (sb + 1) * blk)
        for pair in range(g0 // 2, (g0 + group) // 2):
            cs = slice(pair * LANES, (pair + 1) * LANES)
            v = keys(res, sb, vp_ref, vc_ref, cs)
            kw = v.shape[0]
            zv = jnp.zeros_like(v)
            keep = lax.broadcasted_iota(jnp.int32, (kw, LANES), 1) < HEAD_DIM
            v2 = jnp.concatenate([jnp.where(keep, v, zv), jnp.where(keep, zv, v)], axis=0)
            ones = jnp.concatenate([ones_ref[pair, 0:kw], ones_ref[pair, 2 * blk:2 * blk + kw]], axis=0)
            v2 = jnp.concatenate([v2, ones], axis=1)
            p2 = jnp.concatenate([p_scr[slot, 2 * pair - g0, :, 0:kw],
                                  p_scr[slot, 2 * pair + 1 - g0, :, 0:kw]], axis=1)
            u = _dot(p2, v2)
            o_ref[res, rows, cs] = u[:, :LANES]
            l_mats[res, sb] = l_mats[res, sb] + u[:, LANES:]
            for head in (2 * pair, 2 * pair + 1):
                m_mats[res, sb] = jnp.where(lane == head, ms[res, sb, head], m_mats[res, sb])

    scores(*units[0])
    for i, unit in enumerate(units):
        if i + 1 < len(units):
            scores(*units[i + 1])
        probs(*unit)
        values(*unit)
    for (res, sb), m_mat in m_mats.items():
        st_ref[res, sb * blk:(sb + 1) * blk, :] = m_mat + l_mats[res, sb]


def _attn_kernel(*refs, n_res, n_sub, group, single_step):
    if single_step:
        q_ref, kc_ref, vc_ref = refs[:3]
        _attend(q_ref, None, kc_ref, None, vc_ref, *refs[3:], first=None, n_res=n_res, n_sub=n_sub, group=group)
    else:
        _attend(*refs, first=jnp.minimum(pl.program_id(2), 1), n_res=n_res, n_sub=n_sub, group=group)


def _attn_tables(branch, slopes):
    window, d = BRANCHES[branch]
    assert window // d <= ATTN_BLOCK and 2 * N_HEADS <= LANES
    return jnp.asarray(_bias_table(slopes, d, window // d)), jnp.asarray(_ones_table(), dtype=BF16)


def _head_group(n_res):
    return 2 if n_res == 1 else 8


def _attn_branch(q, k, v, branch, slopes, blocks_per_step=8):
    window, d = BRANCHES[branch]
    bsz, _, packed_sub, _ = q.shape
    sub = 2 * packed_sub
    blk = ATTN_BLOCK
    n_sub = min(blocks_per_step, sub // blk)
    n_res = min(blocks_per_step // n_sub, d)
    group = _head_group(n_res)
    qb = n_sub * blk
    assert sub % qb == 0 and d % n_res == 0
    bias, ones = _attn_tables(branch, slopes)
    cur = lambda rows, w: pl.BlockSpec((None, n_res, rows, w), lambda b, r, j: (b, r, j, 0))
    prev = pl.BlockSpec((None, n_res, blk // 2, D_ATTN),
                        lambda b, r, j: (b, r, jnp.maximum(n_sub * j - 1, 0), 0))
    cur_in = cur(qb // 2, D_ATTN)
    single_step = sub == qb
    if single_step:
        qkv_specs, qkv = [cur_in, cur_in, cur_in], (q, k, v)
    else:
        qkv_specs, qkv = [cur_in, prev, cur_in, prev, cur_in], (q, k, k, v, v)
    return pl.pallas_call(
        functools.partial(_attn_kernel, n_res=n_res, n_sub=n_sub, group=group, single_step=single_step),
        grid=(bsz, d // n_res, sub // qb),
        in_specs=qkv_specs + [_resident(bias.shape), _resident(ones.shape)],
        out_specs=[cur(qb, D_ATTN), cur(qb, LANES)],
        out_shape=[
            jax.ShapeDtypeStruct((bsz, d, sub, D_ATTN), F32),
            jax.ShapeDtypeStruct((bsz, d, sub, LANES), F32),
        ],
        scratch_shapes=[pltpu.VMEM((2, group, blk, 2 * blk), F32),
                        pltpu.VMEM((2, group, blk, 2 * blk), BF16)],
        compiler_params=_params("arbitrary", "arbitrary", "arbitrary"),
        name=f"dilated_attn_{branch}",
    )(*qkv, bias, ones)


def _combine_kernel(q_ref, kp_ref, kc_ref, vp_ref, vc_ref, bias_ref, ones_ref,
                    o1_ref, o2_ref, l1_ref, l2_ref, x_ref, mod_ref, wo_ref, expand_ref,
                    out_ref, o0_ref, l0_ref, s_scr, p_scr, nat_o_ref, nat_l_ref, *, tm, group):
    _attend(q_ref, kp_ref, kc_ref, vp_ref, vc_ref, bias_ref, ones_ref, o0_ref, l0_ref, s_scr, p_scr,
            first=jnp.minimum(pl.program_id(1), 1), n_res=1, n_sub=tm // ATTN_BLOCK, group=group)
    o_refs = (o0_ref, o1_ref, o2_ref)
    l_refs = (l0_ref, l1_ref, l2_ref)
    n_slabs = D_ATTN // LANES
    for g, (_, d) in enumerate(BRANCHES):
        if d == 1:
            continue
        n = tm // d
        for r in range(d):
            nat_l_ref[g - 1, pl.ds(r, n, stride=d), :] = l_refs[g][r]
            for c in range(n_slabs):
                nat_o_ref[g - 1, c, pl.ds(r, n, stride=d), :] = o_refs[g][r, :, c * LANES:(c + 1) * LANES]
    stats = [l0_ref[0], nat_l_ref[0], nat_l_ref[1]]
    m = jnp.maximum(jnp.maximum(stats[0], stats[1]), stats[2])
    es = [jnp.exp(st - m) for st in stats]
    sums = [pltpu.roll(st, LANES - N_HEADS, axis=1) for st in stats]
    inv = 1.0 / (es[0] * sums[0] + es[1] * sums[1] + es[2] * sums[2])
    wts = [e * inv for e in es]
    head_lanes = lax.broadcasted_iota(jnp.int32, (tm, LANES), 1) < N_HEADS
    splits = []
    for w in wts:
        w = jnp.where(head_lanes, w, 0.0)
        hi = w.astype(BF16)
        lo = (w - hi.astype(F32)).astype(BF16)
        splits.append(jnp.concatenate([hi, lo], axis=1))
    cols = []
    wide = 2 * LANES
    for c in range(D_ATTN // wide):
        cs = slice(c * wide, (c + 1) * wide)
        outs = [o0_ref[0, :, cs]] + [
            jnp.concatenate([nat_o_ref[g, 2 * c], nat_o_ref[g, 2 * c + 1]], axis=1) for g in range(2)]
        acc = None
        for g in range(N_BRANCHES):
            term = _dot(splits[g], expand_ref[:, cs]) * outs[g]
            acc = term if acc is None else acc + term
        cols.append(acc.astype(BF16))
    o = jnp.concatenate(cols, axis=-1)
    mod = mod_ref[0]
    out_ref[0] = x_ref[0] + mod[2:3] * _dot(o, wo_ref[...])


def _combine(qkv0, slopes0, outs, stats, x, mods, layer, w_o, w_o_layer, tm=512):
    bsz, s, d_model = x.shape
    dils = [d for _, d in BRANCHES]
    assert dils[0] == 1
    blk = ATTN_BLOCK
    n_sub = tm // blk
    group = _head_group(1)
    res_major = lambda w: [pl.BlockSpec((None, d, tm // d, w), lambda b, i: (b, 0, i, 0)) for d in dils[1:]]
    cur = pl.BlockSpec((None, 1, tm // 2, D_ATTN), lambda b, i: (b, 0, i, 0))
    prev = pl.BlockSpec((None, 1, blk // 2, D_ATTN), lambda b, i: (b, 0, jnp.maximum(n_sub * i - 1, 0), 0))
    bias, ones = _attn_tables(0, slopes0)
    head_rows = np.zeros((LANES, D_ATTN), np.float32)
    head_rows[np.arange(D_ATTN) // HEAD_DIM, np.arange(D_ATTN)] = 1.0
    expand = jnp.asarray(np.concatenate([head_rows, head_rows]), dtype=BF16)
    q0, k0, v0 = qkv0
    return pl.pallas_call(
        functools.partial(_combine_kernel, tm=tm, group=group),
        grid=(bsz, s // tm),
        in_specs=[cur, prev, cur, prev, cur, _resident(bias.shape), _resident(ones.shape)]
        + res_major(D_ATTN) + res_major(LANES) + [
            pl.BlockSpec((1, tm, d_model), lambda b, i: (b, i, 0)),
            _mod_spec(layer, d_model),
            _resident_layer(w_o.shape, w_o_layer),
            _resident(expand.shape),
        ],
        out_specs=pl.BlockSpec((1, tm, d_model), lambda b, i: (b, i, 0)),
        out_shape=jax.ShapeDtypeStruct(x.shape, F32),
        scratch_shapes=[pltpu.VMEM((1, tm, D_ATTN), F32),
                        pltpu.VMEM((1, tm, LANES), F32),
                        pltpu.VMEM((2, group, blk, 2 * blk), F32),
                        pltpu.VMEM((2, group, blk, 2 * blk), BF16),
                        pltpu.VMEM((N_BRANCHES - 1, D_ATTN // LANES, tm, LANES), F32),
                        pltpu.VMEM((N_BRANCHES - 1, tm, LANES), F32)],
        compiler_params=_params("arbitrary", "arbitrary"),
        name="branch_mix_out_proj",
    )(q0, k0, k0, v0, v0, bias, ones, *outs, *stats, x, mods, w_o, expand)


def kernel(x, c, ada_w, ada_b, norm1_g, norm2_g, pool_w_in, pool_w_grp, pool_scale, pool_w_out,
           kv_norm_g, kv_ada_w, kv_ada_b, w_kv, attn_w_q, attn_w_o,
           ffn_w_up, ffn_conv_w, ffn_conv_b, ffn_w_down, final_g):
    bsz, s, d = x.shape
    depth = ada_w.shape[0]
    n_pool = pool_w_in.shape[0]
    slopes = _alibi_slopes(N_BRANCHES * N_HEADS).reshape(N_BRANCHES, N_HEADS)

    mods = _ada(c, ada_w, ada_b).reshape(depth, bsz, 6, d)
    kv_mod = _ada(c, kv_ada_w[None], kv_ada_b[None]).reshape(bsz, 2, 1, d)

    pool_w = [w.astype(BF16) for w in (pool_w_in, pool_w_grp, pool_w_out)]
    w_kv_b, w_q_b, w_o_b = w_kv[None].astype(BF16), attn_w_q.astype(BF16), attn_w_o.astype(BF16)
    w_up_b, w_down_b = ffn_w_up.astype(BF16), ffn_w_down.astype(BF16)

    ks = vs = None
    for layer in range(depth):
        if layer < n_pool:
            x = _pool_layer(x, mods, layer, norm1_g[:n_pool], pool_w[0], pool_w[1], pool_scale, pool_w[2])
        else:
            if layer == n_pool:
                kv = _proj(x, kv_mod[:, 0], kv_mod[:, 1], kv_norm_g, w_kv_b, 0)
                ks, vs = kv[:N_BRANCHES], kv[N_BRANCHES:]
            jl = layer - n_pool
            mod = mods[layer]
            q = _proj(x, mod[:, 0:1], mod[:, 1:2], norm1_g[layer], w_q_b, jl, out_scale=HEAD_DIM ** -0.5)
            outs, stats = zip(*[_attn_branch(q[g], ks[g], vs[g], g, slopes[g]) for g in range(1, N_BRANCHES)])
            x = _combine((q[0], ks[0], vs[0]), slopes[0], outs, stats, x, mods, layer, w_o_b, jl)
        x = _ffn_layer(x, mods, layer, norm2_g, w_up_b, ffn_conv_w, ffn_conv_b, w_down_b, final_g,
                       final_norm=(layer == depth - 1))
    return x
```

```python
import functools
import math

import jax
import jax.numpy as jnp
import numpy as np
from jax import lax
from jax.experimental import pallas as pl
from jax.experimental.pallas import tpu as pltpu

D_MODEL = 1024
POOL_WINDOWS = (2, 4, 8, 16)
POOL_GROUP_DIM = D_MODEL // len(POOL_WINDOWS)
BRANCHES = ((128, 1), (512, 4), (2048, 16))
N_BRANCHES = len(BRANCHES)
HEAD_DIM = 64
N_HEADS = D_MODEL // HEAD_DIM
D_ATTN = N_HEADS * HEAD_DIM
ATTN_BLOCK = 128
D_FF = 2816
CONV_WIDTH = 3
EPS = 1e-6

LANES = 128
CARRY_ROWS = 8
UP_STAGE_ROWS = 32
DOWN_STAGE_ROWS = 128
VMEM_LIMIT = 56 * 1024 * 1024

BF16 = jnp.bfloat16
F32 = jnp.float32


def _alibi_slopes(n):
    def pow2(m):
        start = 2.0 ** (-(2.0 ** -(math.log2(m) - 3)))
        return [start ** (i + 1) for i in range(m)]
    if math.log2(n).is_integer():
        s = pow2(n)
    else:
        c = 2 ** math.floor(math.log2(n))
        s = pow2(c) + pow2(2 * c)[0::2][: n - c]
    s = np.asarray(s, dtype=np.float32)
    return -np.sort(-s)


def _params(*sem):
    return pltpu.CompilerParams(dimension_semantics=sem, vmem_limit_bytes=VMEM_LIMIT)


def _resident(shape):
    nd = len(shape)
    return pl.BlockSpec(shape, lambda *_: (0,) * nd, pipeline_mode=pl.Buffered(1))


def _resident_layer(shape, layer):
    nd = len(shape) - 1
    return pl.BlockSpec((None,) + tuple(shape[1:]), lambda *_: (layer,) + (0,) * nd,
                        pipeline_mode=pl.Buffered(1))


def _mod_spec(layer, d):
    return pl.BlockSpec((None, 1, 6, d), lambda b, i: (layer, b, 0, 0))


def _norm_mod(x, g, shift, scale):
    y = x * lax.rsqrt(jnp.mean(x * x, axis=-1, keepdims=True) + EPS)
    return (y * g) * (1.0 + scale) + shift


def _dot(a, b):
    return jnp.dot(a, b, preferred_element_type=F32)


def _pack(rows_bf16):
    return pltpu.bitcast(rows_bf16, jnp.uint32)


def _unpack(words):
    return pltpu.bitcast(words, BF16)


def _ada_kernel(c_ref, w_ref, b_ref, o_ref):
    c = c_ref[...]
    cond = c * (1.0 / (1.0 + jnp.exp(-c)))
    o_ref[0] = _dot(cond.astype(BF16), w_ref[0].astype(BF16)) + b_ref[0]


def _ada(c, w, b, tn=1024):
    n_layers, d, n = w.shape
    bsz = c.shape[0]
    return pl.pallas_call(
        _ada_kernel,
        grid=(n_layers, n // tn),
        in_specs=[
            pl.BlockSpec((bsz, d), lambda l, j: (0, 0)),
            pl.BlockSpec((1, d, tn), lambda l, j: (l, 0, j)),
            pl.BlockSpec((1, 1, tn), lambda l, j: (l, 0, j)),
        ],
        out_specs=pl.BlockSpec((1, bsz, tn), lambda l, j: (l, 0, j)),
        out_shape=jax.ShapeDtypeStruct((n_layers, bsz, n), F32),
        compiler_params=_params("arbitrary", "arbitrary"),
        name="ada_mod",
    )(c, w, b.reshape(n_layers, 1, n))


def _pool_kernel(x_ref, mod_ref, g_ref, win_ref, wgrp_ref, scale_ref, wout_ref, o_ref,
                 s1_ref, s2_ref, s4_ref, s8_ref, *, tm, n_split):
    si = pl.program_id(1)
    c0 = CARRY_ROWS
    gd = POOL_GROUP_DIM
    stages = (s1_ref, s2_ref, s4_ref, s8_ref)

    @pl.when(si == 0)
    def _():
        for ref in stages:
            ref[0:c0, :] = jnp.zeros((c0, ref.shape[1]), F32)

    mod = mod_ref[0]
    th = tm // n_split
    for part in range(n_split):
        r0 = part * th
        b0 = c0 + r0
        x = x_ref[0, r0:r0 + th, :]
        h = _norm_mod(x, g_ref[...], mod[0:1], mod[1:2])
        u = _dot(h.astype(BF16), win_ref[...])
        s1_ref[b0:b0 + th, :] = u
        for k in range(1, len(stages)):
            prev, cur = stages[k - 1], stages[k]
            sh = 1 << (k - 1)
            cur[b0:b0 + th, :] = prev[b0:b0 + th, gd:] + prev[b0 - sh:b0 - sh + th, gd:]

        t = si * tm + r0 + lax.broadcasted_iota(jnp.int32, (th, 1), 0)
        ys = []
        for g, w in enumerate(POOL_WINDOWS):
            ref = stages[g]
            sh = w // 2
            wsum = ref[b0:b0 + th, 0:gd] + ref[b0 - sh:b0 - sh + th, 0:gd]
            count = jnp.minimum(t + 1, w).astype(F32)
            pooled = wsum / count - u[:, g * gd:(g + 1) * gd]
            ys.append(_dot(pooled.astype(BF16), wgrp_ref[g]))
        y = jnp.concatenate(ys, axis=-1) * scale_ref[...]
        y = _dot(y.astype(BF16), wout_ref[...])
        o_ref[0, r0:r0 + th, :] = x + mod[2:3] * y

    for ref in stages:
        ref[0:c0, :] = ref[tm:tm + c0, :]


def _pool_layer(x, mods, layer, g, w_in, w_grp, scale, w_out, tm=1024, n_split=4):
    bsz, s, d = x.shape
    gd = POOL_GROUP_DIM
    g, scale = g[:, None, :], scale[:, None, :]
    return pl.pallas_call(
        functools.partial(_pool_kernel, tm=tm, n_split=n_split),
        grid=(bsz, s // tm),
        in_specs=[
            pl.BlockSpec((1, tm, d), lambda b, i: (b, i, 0)),
            _mod_spec(layer, d),
            _resident_layer(g.shape, layer),
            _resident_layer(w_in.shape, layer),
            _resident_layer(w_grp.shape, layer),
            _resident_layer(scale.shape, layer),
            _resident_layer(w_out.shape, layer),
        ],
        out_specs=pl.BlockSpec((1, tm, d), lambda b, i: (b, i, 0)),
        out_shape=jax.ShapeDtypeStruct(x.shape, F32),
        scratch_shapes=[pltpu.VMEM((tm + CARRY_ROWS, d - k * gd), F32) for k in range(4)],
        compiler_params=_params("arbitrary", "arbitrary"),
        name="pool_mixer",
    )(x, mods, g, w_in, w_grp, scale, w_out)


def _stage_bf16(w_hbm, layer, dst_ref, stage_ref, sem, rows):
    n = dst_ref.shape[0] // rows

    def chunk(c):
        slot = c % 2
        return pltpu.make_async_copy(w_hbm.at[layer, pl.ds(c * rows, rows), :], stage_ref.at[slot], sem.at[slot])

    chunk(0).start()
    for c in range(n):
        if c + 1 < n:
            chunk(c + 1).start()
        chunk(c).wait()
        dst_ref[c * rows:(c + 1) * rows, :] = stage_ref[c % 2].astype(BF16)


def _ffn_kernel(x_ref, mod_ref, g_ref, wup_hbm, cw_ref, cb_ref, wdown_hbm, fg_ref, o_ref,
                a_ref, wup_ref, wdown_ref, up_stage, down_stage, up_sem, down_sem,
                *, tm, layer, final_norm):
    si = pl.program_id(1)
    c0 = CARRY_ROWS
    f = D_FF

    @pl.when((pl.program_id(0) == 0) & (si == 0))
    def _():
        _stage_bf16(wup_hbm, layer, wup_ref, up_stage, up_sem, UP_STAGE_ROWS)
        _stage_bf16(wdown_hbm, layer, wdown_ref, down_stage, down_sem, DOWN_STAGE_ROWS)

    @pl.when(si == 0)
    def _():
        a_ref[0:c0, :] = jnp.zeros((c0, f), F32)

    x = x_ref[0]
    mod = mod_ref[0]
    h = _norm_mod(x, g_ref[...], mod[3:4], mod[4:5]).astype(BF16)
    a = _dot(h, wup_ref[:, 0:f])
    v = _dot(h, wup_ref[:, f:2 * f])
    a_ref[c0:c0 + tm, :] = a
    y = cb_ref[...]
    for k in range(CONV_WIDTH):
        lag = CONV_WIDTH - 1 - k
        tap = a if lag == 0 else a_ref[c0 - lag:c0 - lag + tm, :]
        y = y + tap * cw_ref[k:k + 1, :]
    gated = y * (1.0 / (1.0 + jnp.exp(-y))) * v
    out = x + mod[5:6] * _dot(gated.astype(BF16), wdown_ref[...])
    if final_norm:
        out = (out * lax.rsqrt(jnp.mean(out * out, axis=-1, keepdims=True) + EPS)) * fg_ref[...]
    o_ref[0] = out
    a_ref[0:c0, :] = a_ref[tm:tm + c0, :]


def _ffn_layer(x, mods, layer, g, w_up, conv_w, conv_b, w_down, final_g, final_norm, tm=512):
    bsz, s, d = x.shape
    f = D_FF
    g, conv_b = g[:, None, :], conv_b[:, None, :]
    assert d % UP_STAGE_ROWS == 0 and f % DOWN_STAGE_ROWS == 0
    return pl.pallas_call(
        functools.partial(_ffn_kernel, tm=tm, layer=layer, final_norm=final_norm),
        grid=(bsz, s // tm),
        in_specs=[
            pl.BlockSpec((1, tm, d), lambda b, i: (b, i, 0)),
            _mod_spec(layer, d),
            _resident_layer(g.shape, layer),
            pl.BlockSpec(memory_space=pl.ANY),
            _resident_layer(conv_w.shape, layer),
            _resident_layer(conv_b.shape, layer),
            pl.BlockSpec(memory_space=pl.ANY),
            _resident((1, d)),
        ],
        out_specs=pl.BlockSpec((1, tm, d), lambda b, i: (b, i, 0)),
        out_shape=jax.ShapeDtypeStruct(x.shape, F32),
        scratch_shapes=[pltpu.VMEM((tm + CARRY_ROWS, f), F32),
                        pltpu.VMEM((d, 2 * f), BF16),
                        pltpu.VMEM((f, d), BF16),
                        pltpu.VMEM((2, UP_STAGE_ROWS, 2 * f), F32),
                        pltpu.VMEM((2, DOWN_STAGE_ROWS, d), F32),
                        pltpu.SemaphoreType.DMA((2,)),
                        pltpu.SemaphoreType.DMA((2,))],
        compiler_params=_params("arbitrary", "arbitrary"),
        name="conv_ffn",
    )(x, mods, g, w_up, conv_w, conv_b, w_down, final_g.reshape(1, d))


def _proj_kernel(x_ref, shift_ref, scale_ref, g_ref, w_ref, *rest, tm, n_tensors, out_scale):
    o_refs, h_ref = rest[:-1], rest[-1]
    h = _norm_mod(x_ref[0], g_ref[...], shift_ref[0], scale_ref[0])
    n_slabs = D_MODEL // LANES
    for c in range(n_slabs):
        h_ref[c] = h[:, c * LANES:(c + 1) * LANES]
    for g, (_, d) in enumerate(BRANCHES):
        n = tm // d
        if d == 1:
            hp = h.astype(BF16)
        else:
            hp = jnp.concatenate(
                [jnp.concatenate([h_ref[c, pl.ds(r, n, stride=d), :] for c in range(n_slabs)], axis=1)
                 for r in range(d)], axis=0).astype(BF16)
        for t in range(n_tensors):
            col = (t * N_BRANCHES + g) * D_ATTN
            res = _dot(hp, w_ref[:, col:col + D_ATTN])
            if out_scale != 1.0:
                res = res * out_scale
            o_ref = o_refs[t * N_BRANCHES + g]
            for r in range(d):
                o_ref[0, r] = _pack(res[r * n:(r + 1) * n].astype(BF16))


def _proj(x, shift, scale, g, w, layer, out_scale=1.0, tm=512):
    bsz, s, d_model = x.shape
    n_tensors = w.shape[2] // (N_BRANCHES * D_ATTN)
    dils = [d for _, d in BRANCHES] * n_tensors
    return pl.pallas_call(
        functools.partial(_proj_kernel, tm=tm, n_tensors=n_tensors, out_scale=out_scale),
        grid=(bsz, s // tm),
        in_specs=[
            pl.BlockSpec((1, tm, d_model), lambda b, i: (b, i, 0)),
            pl.BlockSpec((1, 1, d_model), lambda b, i: (b, 0, 0)),
            pl.BlockSpec((1, 1, d_model), lambda b, i: (b, 0, 0)),
            _resident((1, d_model)),
            _resident_layer(w.shape, layer),
        ],
        out_specs=[pl.BlockSpec((1, d, tm // d // 2, D_ATTN), lambda b, i: (b, 0, i, 0)) for d in dils],
        out_shape=[jax.ShapeDtypeStruct((bsz, d, s // d // 2, D_ATTN), jnp.uint32) for d in dils],
        scratch_shapes=[pltpu.VMEM((d_model // LANES, tm, LANES), F32)],
        compiler_params=_params("arbitrary", "arbitrary"),
        name="norm_proj",
    )(x, shift, scale, g.reshape(1, d_model), w)


def _bias_table(slopes, dilation, n_steps):
    blk = ATTN_BLOCK
    row = np.arange(blk)[:, None]
    col = np.arange(2 * blk)[None, :]
    delta = row + blk - col
    valid = (delta >= 0) & (delta <= n_steps)
    dist = (delta * dilation).astype(np.float32)
    bias = -np.asarray(slopes, np.float32)[:, None, None] * dist[None]
    later = np.where(valid[None], bias, -np.inf).astype(np.float32)
    first = np.where((valid & (col >= blk))[None], bias, -np.inf).astype(np.float32)
    return np.stack([first, later])


def _ones_table():
    blk = ATTN_BLOCK
    t = np.zeros((N_HEADS // 2, 4 * blk, LANES), np.float32)
    for p in range(N_HEADS // 2):
        t[p, :2 * blk, N_HEADS + 2 * p] = 1.0
        t[p, 2 * blk:, N_HEADS + 2 * p + 1] = 1.0
    return t


def _attend(q_ref, kp_ref, kc_ref, vp_ref, vc_ref, bias_ref, ones_ref, o_ref, st_ref, s_scr, p_scr,
            *, first, n_res, n_sub, group):
    blk = ATTN_BLOCK
    lane = lax.broadcasted_iota(jnp.int32, (blk, LANES), 1)
    low_half = lane < HEAD_DIM
    half_blk = blk // 2

    def solo(sb):
        return sb == 0 and first is None

    def keys(res, sb, prev_ref, cur_ref, cs):
        if solo(sb):
            return _unpack(cur_ref[res, 0:half_blk, cs])
        if sb == 0:
            return jnp.concatenate(
                [_unpack(prev_ref[res, :, cs]), _unpack(cur_ref[res, 0:half_blk, cs])], axis=0)
        return _unpack(cur_ref[res, (sb - 1) * half_blk:(sb + 1) * half_blk, cs])

    units = [(res, sb, g0) for res in range(n_res) for sb in range(n_sub)
             for g0 in range(0, N_HEADS, group)]
    unit_slot = {u: i % 2 for i, u in enumerate(units)}
    ms = {}
    m_mats = {(res, sb): jnp.zeros((blk, LANES), F32) for res in range(n_res) for sb in range(n_sub)}
    l_mats = dict(m_mats)

    def scores(res, sb, g0):
        variant = first if sb == 0 else 1
        kw = blk if solo(sb) else 2 * blk
        slot = unit_slot[res, sb, g0]
        for pair in range(g0 // 2, (g0 + group) // 2):
            cs = slice(pair * LANES, (pair + 1) * LANES)
            q = _unpack(q_ref[res, sb * half_blk:(sb + 1) * half_blk, cs])
            zq = jnp.zeros_like(q)
            q2 = jnp.concatenate([jnp.where(low_half, q, zq), jnp.where(low_half, zq, q)], axis=0)
            s2 = lax.dot_general(q2, keys(res, sb, kp_ref, kc_ref, cs), (((1,), (1,)), ((), ())),
                                 preferred_element_type=F32)
            for half in range(2):
                head = 2 * pair + half
                bias = bias_ref[1, head, :, blk:] if solo(sb) else bias_ref[variant, head]
                s = s2[half * blk:(half + 1) * blk] + bias
                s_scr[slot, head - g0, :, 0:kw] = s
                ms[res, sb, head] = jnp.max(s, axis=-1, keepdims=True)

    def probs(res, sb, g0):
        slot = unit_slot[res, sb, g0]
        kw = blk if solo(sb) else 2 * blk
        for head in range(g0, g0 + group):
            p_scr[slot, head - g0, :, 0:kw] = jnp.exp(
                s_scr[slot, head - g0, :, 0:kw] - ms[res, sb, head]).astype(BF16)

    def values(res, sb, g0):
        slot = unit_slot[res, sb, g0]
        rows = slice(sb * blk, (sb + 1) * blk)
        for pair in range(g0 // 2, (g0 + group) // 2):
            cs = slice(pair * LANES, (pair + 1) * LANES)
            v = keys(res, sb, vp_ref, vc_ref, cs)
            kw = v.shape[0]
            zv = jnp.zeros_like(v)
            keep = lax.broadcasted_iota(jnp.int32, (kw, LANES), 1) < HEAD_DIM
            v2 = jnp.concatenate([jnp.where(keep, v, zv), jnp.where(keep, zv, v)], axis=0)
            ones = jnp.concatenate([ones_ref[pair, 0:kw], ones_ref[pair, 2 * blk:2 * blk + kw]], axis=0)
            v2 = jnp.concatenate([v2, ones], axis=1)
            p2 = jnp.concatenate([p_scr[slot, 2 * pair - g0, :, 0:kw],
                                  p_scr[slot, 2 * pair + 1 - g0, :, 0:kw]], axis=1)
            u = _dot(p2, v2)
            o_ref[res, rows, cs] = u[:, :LANES]
            l_mats[res, sb] = l_mats[res, sb] + u[:, LANES:]
            for head in (2 * pair, 2 * pair + 1):
                m_mats[res, sb] = jnp.where(lane == head, ms[res, sb, head], m_mats[res, sb])

    scores(*units[0])
    for i, unit in enumerate(units):
        if i + 1 < len(units):
            scores(*units[i + 1])
        probs(*unit)
        values(*unit)
    for (res, sb), m_mat in m_mats.items():
        st_ref[res, sb * blk:(sb + 1) * blk, :] = m_mat + l_mats[res, sb]


def _attn_kernel(*refs, n_res, n_sub, group, single_step):
    if single_step:
        q_ref, kc_ref, vc_ref = refs[:3]
        _attend(q_ref, None, kc_ref, None, vc_ref, *refs[3:], first=None, n_res=n_res, n_sub=n_sub, group=group)
    else:
        _attend(*refs, first=jnp.minimum(pl.program_id(2), 1), n_res=n_res, n_sub=n_sub, group=group)


def _attn_tables(branch, slopes):
    window, d = BRANCHES[branch]
    assert window // d <= ATTN_BLOCK and 2 * N_HEADS <= LANES
    return jnp.asarray(_bias_table(slopes, d, window // d)), jnp.asarray(_ones_table(), dtype=BF16)


def _head_group(n_res):
    return 2 if n_res == 1 else 8


def _attn_branch(q, k, v, branch, slopes, blocks_per_step=8):
    window, d = BRANCHES[branch]
    bsz, _, packed_sub, _ = q.shape
    sub = 2 * packed_sub
    blk = ATTN_BLOCK
    n_sub = min(blocks_per_step, sub // blk)
    n_res = min(blocks_per_step // n_sub, d)
    group = _head_group(n_res)
    qb = n_sub * blk
    assert sub % qb == 0 and d % n_res == 0
    bias, ones = _attn_tables(branch, slopes)
    cur = lambda rows, w: pl.BlockSpec((None, n_res, rows, w), lambda b, r, j: (b, r, j, 0))
    prev = pl.BlockSpec((None, n_res, blk // 2, D_ATTN),
                        lambda b, r, j: (b, r, jnp.maximum(n_sub * j - 1, 0), 0))
    cur_in = cur(qb // 2, D_ATTN)
    single_step = sub == qb
    if single_step:
        qkv_specs, qkv = [cur_in, cur_in, cur_in], (q, k, v)
    else:
        qkv_specs, qkv = [cur_in, prev, cur_in, prev, cur_in], (q, k, k, v, v)
    return pl.pallas_call(
        functools.partial(_attn_kernel, n_res=n_res, n_sub=n_sub, group=group, single_step=single_step),
        grid=(bsz, d // n_res, sub // qb),
        in_specs=qkv_specs + [_resident(bias.shape), _resident(ones.shape)],
        out_specs=[cur(qb, D_ATTN), cur(qb, LANES)],
        out_shape=[
            jax.ShapeDtypeStruct((bsz, d, sub, D_ATTN), F32),
            jax.ShapeDtypeStruct((bsz, d, sub, LANES), F32),
        ],
        scratch_shapes=[pltpu.VMEM((2, group, blk, 2 * blk), F32),
                        pltpu.VMEM((2, group, blk, 2 * blk), BF16)],
        compiler_params=_params("arbitrary", "arbitrary", "arbitrary"),
        name=f"dilated_attn_{branch}",
    )(*qkv, bias, ones)


def _combine_kernel(q_ref, kp_ref, kc_ref, vp_ref, vc_ref, bias_ref, ones_ref,
                    o1_ref, o2_ref, l1_ref, l2_ref, x_ref, mod_ref, wo_ref, expand_ref,
                    out_ref, o0_ref, l0_ref, s_scr, p_scr, nat_o_ref, nat_l_ref, *, tm, group):
    _attend(q_ref, kp_ref, kc_ref, vp_ref, vc_ref, bias_ref, ones_ref, o0_ref, l0_ref, s_scr, p_scr,
            first=jnp.minimum(pl.program_id(1), 1), n_res=1, n_sub=tm // ATTN_BLOCK, group=group)
    o_refs = (o0_ref, o1_ref, o2_ref)
    l_refs = (l0_ref, l1_ref, l2_ref)
    n_slabs = D_ATTN // LANES
    for g, (_, d) in enumerate(BRANCHES):
        if d == 1:
            continue
        n = tm // d
        for r in range(d):
            nat_l_ref[g - 1, pl.ds(r, n, stride=d), :] = l_refs[g][r]
            for c in range(n_slabs):
                nat_o_ref[g - 1, c, pl.ds(r, n, stride=d), :] = o_refs[g][r, :, c * LANES:(c + 1) * LANES]
    stats = [l0_ref[0], nat_l_ref[0], nat_l_ref[1]]
    m = jnp.maximum(jnp.maximum(stats[0], stats[1]), stats[2])
    es = [jnp.exp(st - m) for st in stats]
    sums = [pltpu.roll(st, LANES - N_HEADS, axis=1) for st in stats]
    inv = 1.0 / (es[0] * sums[0] + es[1] * sums[1] + es[2] * sums[2])
    wts = [e * inv for e in es]
    head_lanes = lax.broadcasted_iota(jnp.int32, (tm, LANES), 1) < N_HEADS
    splits = []
    for w in wts:
        w = jnp.where(head_lanes, w, 0.0)
        hi = w.astype(BF16)
        lo = (w - hi.astype(F32)).astype(BF16)
        splits.append(jnp.concatenate([hi, lo], axis=1))
    cols = []
    wide = 2 * LANES
    for c in range(D_ATTN // wide):
        cs = slice(c * wide, (c + 1) * wide)
        outs = [o0_ref[0, :, cs]] + [
            jnp.concatenate([nat_o_ref[g, 2 * c], nat_o_ref[g, 2 * c + 1]], axis=1) for g in range(2)]
        acc = None
        for g in range(N_BRANCHES):
            term = _dot(splits[g], expand_ref[:, cs]) * outs[g]
            acc = term if acc is None else acc + term
        cols.append(acc.astype(BF16))
    o = jnp.concatenate(cols, axis=-1)
    mod = mod_ref[0]
    out_ref[0] = x_ref[0] + mod[2:3] * _dot(o, wo_ref[...])


def _combine(qkv0, slopes0, outs, stats, x, mods, layer, w_o, w_o_layer, tm=512):
    bsz, s, d_model = x.shape
    dils = [d for _, d in BRANCHES]
    assert dils[0] == 1
    blk = ATTN_BLOCK
    n_sub = tm // blk
    group = _head_group(1)
    res_major = lambda w: [pl.BlockSpec((None, d, tm // d, w), lambda b, i: (b, 0, i, 0)) for d in dils[1:]]
    cur = pl.BlockSpec((None, 1, tm // 2, D_ATTN), lambda b, i: (b, 0, i, 0))
    prev = pl.BlockSpec((None, 1, blk // 2, D_ATTN), lambda b, i: (b, 0, jnp.maximum(n_sub * i - 1, 0), 0))
    bias, ones = _attn_tables(0, slopes0)
    head_rows = np.zeros((LANES, D_ATTN), np.float32)
    head_rows[np.arange(D_ATTN) // HEAD_DIM, np.arange(D_ATTN)] = 1.0
    expand = jnp.asarray(np.concatenate([head_rows, head_rows]), dtype=BF16)
    q0, k0, v0 = qkv0
    return pl.pallas_call(
        functools.partial(_combine_kernel, tm=tm, group=group),
        grid=(bsz, s // tm),
        in_specs=[cur, prev, cur, prev, cur, _resident(bias.shape), _resident(ones.shape)]
        + res_major(D_ATTN) + res_major(LANES) + [
            pl.BlockSpec((1, tm, d_model), lambda b, i: (b, i, 0)),
            _mod_spec(layer, d_model),
            _resident_layer(w_o.shape, w_o_layer),
            _resident(expand.shape),
        ],
        out_specs=pl.BlockSpec((1, tm, d_model), lambda b, i: (b, i, 0)),
        out_shape=jax.ShapeDtypeStruct(x.shape, F32),
        scratch_shapes=[pltpu.VMEM((1, tm, D_ATTN), F32),
                        pltpu.VMEM((1, tm, LANES), F32),
                        pltpu.VMEM((2, group, blk, 2 * blk), F32),
                        pltpu.VMEM((2, group, blk, 2 * blk), BF16),
                        pltpu.VMEM((N_BRANCHES - 1, D_ATTN // LANES, tm, LANES), F32),
                        pltpu.VMEM((N_BRANCHES - 1, tm, LANES), F32)],
        compiler_params=_params("arbitrary", "arbitrary"),
        name="branch_mix_out_proj",
    )(q0, k0, k0, v0, v0, bias, ones, *outs, *stats, x, mods, w_o, expand)


def kernel(x, c, ada_w, ada_b, norm1_g, norm2_g, pool_w_in, pool_w_grp, pool_scale, pool_w_out,
           kv_norm_g, kv_ada_w, kv_ada_b, w_kv, attn_w_q, attn_w_o,
           ffn_w_up, ffn_conv_w, ffn_conv_b, ffn_w_down, final_g):
    bsz, s, d = x.shape
    depth = ada_w.shape[0]
    n_pool = pool_w_in.shape[0]
    slopes = _alibi_slopes(N_BRANCHES * N_HEADS).reshape(N_BRANCHES, N_HEADS)

    mods = _ada(c, ada_w, ada_b).reshape(depth, bsz, 6, d)
    kv_mod = _ada(c, kv_ada_w[None], kv_ada_b[None]).reshape(bsz, 2, 1, d)

    pool_w = [w.astype(BF16) for w in (pool_w_in, pool_w_grp, pool_w_out)]
    w_kv_b, w_q_b, w_o_b = w_kv[None].astype(BF16), attn_w_q.astype(BF16), attn_w_o.astype(BF16)

    ks = vs = None
    for layer in range(depth):
        if layer < n_pool:
            x = _pool_layer(x, mods, layer, norm1_g[:n_pool], pool_w[0], pool_w[1], pool_scale, pool_w[2])
        else:
            if layer == n_pool:
                kv = _proj(x, kv_mod[:, 0], kv_mod[:, 1], kv_norm_g, w_kv_b, 0)
                ks, vs = kv[:N_BRANCHES], kv[N_BRANCHES:]
            jl = layer - n_pool
            mod = mods[layer]
            q = _proj(x, mod[:, 0:1], mod[:, 1:2], norm1_g[layer], w_q_b, jl, out_scale=HEAD_DIM ** -0.5)
            outs, stats = zip(*[_attn_branch(q[g], ks[g], vs[g], g, slopes[g]) for g in range(1, N_BRANCHES)])
            x = _combine((q[0], ks[0], vs[0]), slopes[0], outs, stats, x, mods, layer, w_o_b, jl)
        x = _ffn_layer(x, mods, layer, norm2_g, ffn_w_up, ffn_conv_w, ffn_conv_b, ffn_w_down, final_g,
                       final_norm=(layer == depth - 1))
    return x
```

```python
import functools
import math

import jax
import jax.numpy as jnp
import numpy as np
from jax import lax
from jax.experimental import pallas as pl
from jax.experimental.pallas import tpu as pltpu

D_MODEL = 1024
POOL_WINDOWS = (2, 4, 8, 16)
POOL_GROUP_DIM = D_MODEL // len(POOL_WINDOWS)
BRANCHES = ((128, 1), (512, 4), (2048, 16))
N_BRANCHES = len(BRANCHES)
HEAD_DIM = 64
N_HEADS = D_MODEL // HEAD_DIM
D_ATTN = N_HEADS * HEAD_DIM
ATTN_BLOCK = 128
D_FF = 2816
CONV_WIDTH = 3
EPS = 1e-6

LANES = 128
CARRY_ROWS = 8
UP_STAGE_ROWS = 32
DOWN_STAGE_ROWS = 128
STAGE_DEPTH = 4
VMEM_LIMIT = 56 * 1024 * 1024

BF16 = jnp.bfloat16
F32 = jnp.float32


def _alibi_slopes(n):
    def pow2(m):
        start = 2.0 ** (-(2.0 ** -(math.log2(m) - 3)))
        return [start ** (i + 1) for i in range(m)]
    if math.log2(n).is_integer():
        s = pow2(n)
    else:
        c = 2 ** math.floor(math.log2(n))
        s = pow2(c) + pow2(2 * c)[0::2][: n - c]
    s = np.asarray(s, dtype=np.float32)
    return -np.sort(-s)


def _params(*sem):
    return pltpu.CompilerParams(dimension_semantics=sem, vmem_limit_bytes=VMEM_LIMIT)


def _resident(shape):
    nd = len(shape)
    return pl.BlockSpec(shape, lambda *_: (0,) * nd, pipeline_mode=pl.Buffered(1))


def _resident_layer(shape, layer):
    nd = len(shape) - 1
    return pl.BlockSpec((None,) + tuple(shape[1:]), lambda *_: (layer,) + (0,) * nd,
                        pipeline_mode=pl.Buffered(1))


def _mod_spec(layer, d):
    return pl.BlockSpec((None, 1, 6, d), lambda b, i: (layer, b, 0, 0))


def _norm_mod(x, g, shift, scale):
    y = x * lax.rsqrt(jnp.mean(x * x, axis=-1, keepdims=True) + EPS)
    return (y * g) * (1.0 + scale) + shift


def _dot(a, b):
    return jnp.dot(a, b, preferred_element_type=F32)


def _pack(rows_bf16):
    return pltpu.bitcast(rows_bf16, jnp.uint32)


def _unpack(words):
    return pltpu.bitcast(words, BF16)


def _ada_kernel(c_ref, w_ref, b_ref, o_ref):
    c = c_ref[...]
    cond = c * (1.0 / (1.0 + jnp.exp(-c)))
    o_ref[0] = _dot(cond.astype(BF16), w_ref[0].astype(BF16)) + b_ref[0]


def _ada(c, w, b, tn=1024):
    n_layers, d, n = w.shape
    bsz = c.shape[0]
    return pl.pallas_call(
        _ada_kernel,
        grid=(n_layers, n // tn),
        in_specs=[
            pl.BlockSpec((bsz, d), lambda l, j: (0, 0)),
            pl.BlockSpec((1, d, tn), lambda l, j: (l, 0, j)),
            pl.BlockSpec((1, 1, tn), lambda l, j: (l, 0, j)),
        ],
        out_specs=pl.BlockSpec((1, bsz, tn), lambda l, j: (l, 0, j)),
        out_shape=jax.ShapeDtypeStruct((n_layers, bsz, n), F32),
        compiler_params=_params("arbitrary", "arbitrary"),
        name="ada_mod",
    )(c, w, b.reshape(n_layers, 1, n))


def _pool_kernel(x_ref, mod_ref, g_ref, win_ref, wgrp_ref, scale_ref, wout_ref, o_ref,
                 s1_ref, s2_ref, s4_ref, s8_ref, *, tm, n_split):
    si = pl.program_id(1)
    c0 = CARRY_ROWS
    gd = POOL_GROUP_DIM
    stages = (s1_ref, s2_ref, s4_ref, s8_ref)

    @pl.when(si == 0)
    def _():
        for ref in stages:
            ref[0:c0, :] = jnp.zeros((c0, ref.shape[1]), F32)

    mod = mod_ref[0]
    th = tm // n_split
    for part in range(n_split):
        r0 = part * th
        b0 = c0 + r0
        x = x_ref[0, r0:r0 + th, :]
        h = _norm_mod(x, g_ref[...], mod[0:1], mod[1:2])
        u = _dot(h.astype(BF16), win_ref[...])
        s1_ref[b0:b0 + th, :] = u
        for k in range(1, len(stages)):
            prev, cur = stages[k - 1], stages[k]
            sh = 1 << (k - 1)
            cur[b0:b0 + th, :] = prev[b0:b0 + th, gd:] + prev[b0 - sh:b0 - sh + th, gd:]

        t = si * tm + r0 + lax.broadcasted_iota(jnp.int32, (th, 1), 0)
        ys = []
        for g, w in enumerate(POOL_WINDOWS):
            ref = stages[g]
            sh = w // 2
            wsum = ref[b0:b0 + th, 0:gd] + ref[b0 - sh:b0 - sh + th, 0:gd]
            count = jnp.minimum(t + 1, w).astype(F32)
            pooled = wsum / count - u[:, g * gd:(g + 1) * gd]
            ys.append(_dot(pooled.astype(BF16), wgrp_ref[g]))
        y = jnp.concatenate(ys, axis=-1) * scale_ref[...]
        y = _dot(y.astype(BF16), wout_ref[...])
        o_ref[0, r0:r0 + th, :] = x + mod[2:3] * y

    for ref in stages:
        ref[0:c0, :] = ref[tm:tm + c0, :]


def _pool_layer(x, mods, layer, g, w_in, w_grp, scale, w_out, tm=1024, n_split=4):
    bsz, s, d = x.shape
    gd = POOL_GROUP_DIM
    g, scale = g[:, None, :], scale[:, None, :]
    return pl.pallas_call(
        functools.partial(_pool_kernel, tm=tm, n_split=n_split),
        grid=(bsz, s // tm),
        in_specs=[
            pl.BlockSpec((1, tm, d), lambda b, i: (b, i, 0)),
            _mod_spec(layer, d),
            _resident_layer(g.shape, layer),
            _resident_layer(w_in.shape, layer),
            _resident_layer(w_grp.shape, layer),
            _resident_layer(scale.shape, layer),
            _resident_layer(w_out.shape, layer),
        ],
        out_specs=pl.BlockSpec((1, tm, d), lambda b, i: (b, i, 0)),
        out_shape=jax.ShapeDtypeStruct(x.shape, F32),
        scratch_shapes=[pltpu.VMEM((tm + CARRY_ROWS, d - k * gd), F32) for k in range(4)],
        compiler_params=_params("arbitrary", "arbitrary"),
        name="pool_mixer",
    )(x, mods, g, w_in, w_grp, scale, w_out)


def _stage_bf16(jobs, depth):
    def chunk(job, c):
        w_hbm, layer, _, stage_ref, sem, rows = job
        slot = c % depth
        return pltpu.make_async_copy(w_hbm.at[layer, pl.ds(c * rows, rows), :], stage_ref.at[slot], sem.at[slot])

    counts = [job[2].shape[0] // job[5] for job in jobs]
    for job, n in zip(jobs, counts):
        for c in range(min(depth, n)):
            chunk(job, c).start()
    for c in range(max(counts)):
        for job, n in zip(jobs, counts):
            if c >= n:
                continue
            _, _, dst_ref, stage_ref, _, rows = job
            chunk(job, c).wait()
            dst_ref[c * rows:(c + 1) * rows, :] = stage_ref[c % depth].astype(BF16)
            if c + depth < n:
                chunk(job, c + depth).start()


def _ffn_kernel(x_ref, mod_ref, g_ref, wup_hbm, cw_ref, cb_ref, wdown_hbm, fg_ref, o_ref,
                a_ref, wup_ref, wdown_ref, up_stage, down_stage, up_sem, down_sem,
                *, tm, layer, final_norm):
    si = pl.program_id(1)
    c0 = CARRY_ROWS
    f = D_FF

    @pl.when((pl.program_id(0) == 0) & (si == 0))
    def _():
        _stage_bf16([(wup_hbm, layer, wup_ref, up_stage, up_sem, UP_STAGE_ROWS),
                     (wdown_hbm, layer, wdown_ref, down_stage, down_sem, DOWN_STAGE_ROWS)], STAGE_DEPTH)

    @pl.when(si == 0)
    def _():
        a_ref[0:c0, :] = jnp.zeros((c0, f), F32)

    x = x_ref[0]
    mod = mod_ref[0]
    h = _norm_mod(x, g_ref[...], mod[3:4], mod[4:5]).astype(BF16)
    a = _dot(h, wup_ref[:, 0:f])
    v = _dot(h, wup_ref[:, f:2 * f])
    a_ref[c0:c0 + tm, :] = a
    y = cb_ref[...]
    for k in range(CONV_WIDTH):
        lag = CONV_WIDTH - 1 - k
        tap = a if lag == 0 else a_ref[c0 - lag:c0 - lag + tm, :]
        y = y + tap * cw_ref[k:k + 1, :]
    gated = y * (1.0 / (1.0 + jnp.exp(-y))) * v
    out = x + mod[5:6] * _dot(gated.astype(BF16), wdown_ref[...])
    if final_norm:
        out = (out * lax.rsqrt(jnp.mean(out * out, axis=-1, keepdims=True) + EPS)) * fg_ref[...]
    o_ref[0] = out
    a_ref[0:c0, :] = a_ref[tm:tm + c0, :]


def _ffn_layer(x, mods, layer, g, w_up, conv_w, conv_b, w_down, final_g, final_norm, tm=512):
    bsz, s, d = x.shape
    f = D_FF
    g, conv_b = g[:, None, :], conv_b[:, None, :]
    assert d % UP_STAGE_ROWS == 0 and f % DOWN_STAGE_ROWS == 0
    return pl.pallas_call(
        functools.partial(_ffn_kernel, tm=tm, layer=layer, final_norm=final_norm),
        grid=(bsz, s // tm),
        in_specs=[
            pl.BlockSpec((1, tm, d), lambda b, i: (b, i, 0)),
            _mod_spec(layer, d),
            _resident_layer(g.shape, layer),
            pl.BlockSpec(memory_space=pl.ANY),
            _resident_layer(conv_w.shape, layer),
            _resident_layer(conv_b.shape, layer),
            pl.BlockSpec(memory_space=pl.ANY),
            _resident((1, d)),
        ],
        out_specs=pl.BlockSpec((1, tm, d), lambda b, i: (b, i, 0)),
        out_shape=jax.ShapeDtypeStruct(x.shape, F32),
        scratch_shapes=[pltpu.VMEM((tm + CARRY_ROWS, f), F32),
                        pltpu.VMEM((d, 2 * f), BF16),
                        pltpu.VMEM((f, d), BF16),
                        pltpu.VMEM((STAGE_DEPTH, UP_STAGE_ROWS, 2 * f), F32),
                        pltpu.VMEM((STAGE_DEPTH, DOWN_STAGE_ROWS, d), F32),
                        pltpu.SemaphoreType.DMA((STAGE_DEPTH,)),
                        pltpu.SemaphoreType.DMA((STAGE_DEPTH,))],
        compiler_params=_params("arbitrary", "arbitrary"),
        name="conv_ffn",
    )(x, mods, g, w_up, conv_w, conv_b, w_down, final_g.reshape(1, d))


def _proj_kernel(x_ref, shift_ref, scale_ref, g_ref, w_ref, *rest, tm, n_tensors, out_scale):
    o_refs, h_ref = rest[:-1], rest[-1]
    h = _norm_mod(x_ref[0], g_ref[...], shift_ref[0], scale_ref[0])
    n_slabs = D_MODEL // LANES
    for c in range(n_slabs):
        h_ref[c] = h[:, c * LANES:(c + 1) * LANES]
    for g, (_, d) in enumerate(BRANCHES):
        n = tm // d
        if d == 1:
            hp = h.astype(BF16)
        else:
            hp = jnp.concatenate(
                [jnp.concatenate([h_ref[c, pl.ds(r, n, stride=d), :] for c in range(n_slabs)], axis=1)
                 for r in range(d)], axis=0).astype(BF16)
        for t in range(n_tensors):
            col = (t * N_BRANCHES + g) * D_ATTN
            res = _dot(hp, w_ref[:, col:col + D_ATTN])
            if out_scale != 1.0:
                res = res * out_scale
            o_ref = o_refs[t * N_BRANCHES + g]
            for r in range(d):
                o_ref[0, r] = _pack(res[r * n:(r + 1) * n].astype(BF16))


def _proj(x, shift, scale, g, w, layer, out_scale=1.0, tm=512):
    bsz, s, d_model = x.shape
    n_tensors = w.shape[2] // (N_BRANCHES * D_ATTN)
    dils = [d for _, d in BRANCHES] * n_tensors
    return pl.pallas_call(
        functools.partial(_proj_kernel, tm=tm, n_tensors=n_tensors, out_scale=out_scale),
        grid=(bsz, s // tm),
        in_specs=[
            pl.BlockSpec((1, tm, d_model), lambda b, i: (b, i, 0)),
            pl.BlockSpec((1, 1, d_model), lambda b, i: (b, 0, 0)),
            pl.BlockSpec((1, 1, d_model), lambda b, i: (b, 0, 0)),
            _resident((1, d_model)),
            _resident_layer(w.shape, layer),
        ],
        out_specs=[pl.BlockSpec((1, d, tm // d // 2, D_ATTN), lambda b, i: (b, 0, i, 0)) for d in dils],
        out_shape=[jax.ShapeDtypeStruct((bsz, d, s // d // 2, D_ATTN), jnp.uint32) for d in dils],
        scratch_shapes=[pltpu.VMEM((d_model // LANES, tm, LANES), F32)],
        compiler_params=_params("arbitrary", "arbitrary"),
        name="norm_proj",
    )(x, shift, scale, g.reshape(1, d_model), w)


def _bias_table(slopes, dilation, n_steps):
    blk = ATTN_BLOCK
    row = np.arange(blk)[:, None]
    col = np.arange(2 * blk)[None, :]
    delta = row + blk - col
    valid = (delta >= 0) & (delta <= n_steps)
    dist = (delta * dilation).astype(np.float32)
    bias = -np.asarray(slopes, np.float32)[:, None, None] * dist[None]
    later = np.where(valid[None], bias, -np.inf).astype(np.float32)
    first = np.where((valid & (col >= blk))[None], bias, -np.inf).astype(np.float32)
    return np.stack([first, later])


def _ones_table():
    blk = ATTN_BLOCK
    t = np.zeros((N_HEADS // 2, 4 * blk, LANES), np.float32)
    for p in range(N_HEADS // 2):
        t[p, :2 * blk, N_HEADS + 2 * p] = 1.0
        t[p, 2 * blk:, N_HEADS + 2 * p + 1] = 1.0
    return t


def _attend(q_ref, kp_ref, kc_ref, vp_ref, vc_ref, bias_ref, ones_ref, o_ref, st_ref, s_scr, p_scr,
            *, first, n_res, n_sub, group):
    blk = ATTN_BLOCK
    lane = lax.broadcasted_iota(jnp.int32, (blk, LANES), 1)
    low_half = lane < HEAD_DIM
    half_blk = blk // 2

    def solo(sb):
        return sb == 0 and first is None

    def keys(res, sb, prev_ref, cur_ref, cs):
        if solo(sb):
            return _unpack(cur_ref[res, 0:half_blk, cs])
        if sb == 0:
            return jnp.concatenate(
                [_unpack(prev_ref[res, :, cs]), _unpack(cur_ref[res, 0:half_blk, cs])], axis=0)
        return _unpack(cur_ref[res, (sb - 1) * half_blk:(sb + 1) * half_blk, cs])

    units = [(res, sb, g0) for res in range(n_res) for sb in range(n_sub)
             for g0 in range(0, N_HEADS, group)]
    unit_slot = {u: i % 2 for i, u in enumerate(units)}
    ms = {}
    m_mats = {(res, sb): jnp.zeros((blk, LANES), F32) for res in range(n_res) for sb in range(n_sub)}
    l_mats = dict(m_mats)

    def scores(res, sb, g0):
        variant = first if sb == 0 else 1
        kw = blk if solo(sb) else 2 * blk
        slot = unit_slot[res, sb, g0]
        for pair in range(g0 // 2, (g0 + group) // 2):
            cs = slice(pair * LANES, (pair + 1) * LANES)
            q = _unpack(q_ref[res, sb * half_blk:(sb + 1) * half_blk, cs])
            zq = jnp.zeros_like(q)
            q2 = jnp.concatenate([jnp.where(low_half, q, zq), jnp.where(low_half, zq, q)], axis=0)
            s2 = lax.dot_general(q2, keys(res, sb, kp_ref, kc_ref, cs), (((1,), (1,)), ((), ())),
                                 preferred_element_type=F32)
            for half in range(2):
                head = 2 * pair + half
                bias = bias_ref[1, head, :, blk:] if solo(sb) else bias_ref[variant, head]
                s = s2[half * blk:(half + 1) * blk] + bias
                s_scr[slot, head - g0, :, 0:kw] = s
                ms[res, sb, head] = jnp.max(s, axis=-1, keepdims=True)

    def probs(res, sb, g0):
        slot = unit_slot[res, sb, g0]
        kw = blk if solo(sb) else 2 * blk
        for head in range(g0, g0 + group):
            p_scr[slot, head - g0, :, 0:kw] = jnp.exp(
                s_scr[slot, head - g0, :, 0:kw] - ms[res, sb, head]).astype(BF16)

    def values(res, sb, g0):
        slot = unit_slot[res, sb, g0]
        rows = slice(sb * blk, (sb + 1) * blk)
        for pair in range(g0 // 2, (g0 + group) // 2):
            cs = slice(pair * LANES, (pair + 1) * LANES)
            v = keys(res, sb, vp_ref, vc_ref, cs)
            kw = v.shape[0]
            zv = jnp.zeros_like(v)
            keep = lax.broadcasted_iota(jnp.int32, (kw, LANES), 1) < HEAD_DIM
            v2 = jnp.concatenate([jnp.where(keep, v, zv), jnp.where(keep, zv, v)], axis=0)
            ones = jnp.concatenate([ones_ref[pair, 0:kw], ones_ref[pair, 2 * blk:2 * blk + kw]], axis=0)
            v2 = jnp.concatenate([v2, ones], axis=1)
            p2 = jnp.concatenate([p_scr[slot, 2 * pair - g0, :, 0:kw],
                                  p_scr[slot, 2 * pair + 1 - g0, :, 0:kw]], axis=1)
            u = _dot(p2, v2)
            o_ref[res, rows, cs] = u[:, :LANES]
            l_mats[res, sb] = l_mats[res, sb] + u[:, LANES:]
            for head in (2 * pair, 2 * pair + 1):
                m_mats[res, sb] = jnp.where(lane == head, ms[res, sb, head], m_mats[res, sb])

    scores(*units[0])
    for i, unit in enumerate(units):
        if i + 1 < len(units):
            scores(*units[i + 1])
        probs(*unit)
        values(*unit)
    for (res, sb), m_mat in m_mats.items():
        st_ref[res, sb * blk:(sb + 1) * blk, :] = m_mat + l_mats[res, sb]


def _attn_kernel(*refs, n_res, n_sub, group, single_step):
    if single_step:
        q_ref, kc_ref, vc_ref = refs[:3]
        _attend(q_ref, None, kc_ref, None, vc_ref, *refs[3:], first=None, n_res=n_res, n_sub=n_sub, group=group)
    else:
        _attend(*refs, first=jnp.minimum(pl.program_id(2), 1), n_res=n_res, n_sub=n_sub, group=group)


def _attn_tables(branch, slopes):
    window, d = BRANCHES[branch]
    assert window // d <= ATTN_BLOCK and 2 * N_HEADS <= LANES
    return jnp.asarray(_bias_table(slopes, d, window // d)), jnp.asarray(_ones_table(), dtype=BF16)


def _head_group(n_res):
    return 2 if n_res == 1 else 8


def _attn_branch(q, k, v, branch, slopes, blocks_per_step=8):
    window, d = BRANCHES[branch]
    bsz, _, packed_sub, _ = q.shape
    sub = 2 * packed_sub
    blk = ATTN_BLOCK
    n_sub = min(blocks_per_step, sub // blk)
    n_res = min(blocks_per_step // n_sub, d)
    group = _head_group(n_res)
    qb = n_sub * blk
    assert sub % qb == 0 and d % n_res == 0
    bias, ones = _attn_tables(branch, slopes)
    cur = lambda rows, w: pl.BlockSpec((None, n_res, rows, w), lambda b, r, j: (b, r, j, 0))
    prev = pl.BlockSpec((None, n_res, blk // 2, D_ATTN),
                        lambda b, r, j: (b, r, jnp.maximum(n_sub * j - 1, 0), 0))
    cur_in = cur(qb // 2, D_ATTN)
    single_step = sub == qb
    if single_step:
        qkv_specs, qkv = [cur_in, cur_in, cur_in], (q, k, v)
    else:
        qkv_specs, qkv = [cur_in, prev, cur_in, prev, cur_in], (q, k, k, v, v)
    return pl.pallas_call(
        functools.partial(_attn_kernel, n_res=n_res, n_sub=n_sub, group=group, single_step=single_step),
        grid=(bsz, d // n_res, sub // qb),
        in_specs=qkv_specs + [_resident(bias.shape), _resident(ones.shape)],
        out_specs=[cur(qb, D_ATTN), cur(qb, LANES)],
        out_shape=[
            jax.ShapeDtypeStruct((bsz, d, sub, D_ATTN), F32),
            jax.ShapeDtypeStruct((bsz, d, sub, LANES), F32),
        ],
        scratch_shapes=[pltpu.VMEM((2, group, blk, 2 * blk), F32),
                        pltpu.VMEM((2, group, blk, 2 * blk), BF16)],
        compiler_params=_params("arbitrary", "arbitrary", "arbitrary"),
        name=f"dilated_attn_{branch}",
    )(*qkv, bias, ones)


def _combine_kernel(q_ref, kp_ref, kc_ref, vp_ref, vc_ref, bias_ref, ones_ref,
                    o1_ref, o2_ref, l1_ref, l2_ref, x_ref, mod_ref, wo_ref, expand_ref,
                    out_ref, o0_ref, l0_ref, s_scr, p_scr, nat_o_ref, nat_l_ref, *, tm, group):
    _attend(q_ref, kp_ref, kc_ref, vp_ref, vc_ref, bias_ref, ones_ref, o0_ref, l0_ref, s_scr, p_scr,
            first=jnp.minimum(pl.program_id(1), 1), n_res=1, n_sub=tm // ATTN_BLOCK, group=group)
    o_refs = (o0_ref, o1_ref, o2_ref)
    l_refs = (l0_ref, l1_ref, l2_ref)
    n_slabs = D_ATTN // LANES
    for g, (_, d) in enumerate(BRANCHES):
        if d == 1:
            continue
        n = tm // d
        for r in range(d):
            nat_l_ref[g - 1, pl.ds(r, n, stride=d), :] = l_refs[g][r]
            for c in range(n_slabs):
                nat_o_ref[g - 1, c, pl.ds(r, n, stride=d), :] = o_refs[g][r, :, c * LANES:(c + 1) * LANES]
    stats = [l0_ref[0], nat_l_ref[0], nat_l_ref[1]]
    m = jnp.maximum(jnp.maximum(stats[0], stats[1]), stats[2])
    es = [jnp.exp(st - m) for st in stats]
    sums = [pltpu.roll(st, LANES - N_HEADS, axis=1) for st in stats]
    inv = 1.0 / (es[0] * sums[0] + es[1] * sums[1] + es[2] * sums[2])
    wts = [e * inv for e in es]
    head_lanes = lax.broadcasted_iota(jnp.int32, (tm, LANES), 1) < N_HEADS
    splits = []
    for w in wts:
        w = jnp.where(head_lanes, w, 0.0)
        hi = w.astype(BF16)
        lo = (w - hi.astype(F32)).astype(BF16)
        splits.append(jnp.concatenate([hi, lo], axis=1))
    cols = []
    wide = 2 * LANES
    for c in range(D_ATTN // wide):
        cs = slice(c * wide, (c + 1) * wide)
        outs = [o0_ref[0, :, cs]] + [
            jnp.concatenate([nat_o_ref[g, 2 * c], nat_o_ref[g, 2 * c + 1]], axis=1) for g in range(2)]
        acc = None
        for g in range(N_BRANCHES):
            term = _dot(splits[g], expand_ref[:, cs]) * outs[g]
            acc = term if acc is None else acc + term
        cols.append(acc.astype(BF16))
    o = jnp.concatenate(cols, axis=-1)
    mod = mod_ref[0]
    out_ref[0] = x_ref[0] + mod[2:3] * _dot(o, wo_ref[...])


def _combine(qkv0, slopes0, outs, stats, x, mods, layer, w_o, w_o_layer, tm=512):
    bsz, s, d_model = x.shape
    dils = [d for _, d in BRANCHES]
    assert dils[0] == 1
    blk = ATTN_BLOCK
    n_sub = tm // blk
    group = _head_group(1)
    res_major = lambda w: [pl.BlockSpec((None, d, tm // d, w), lambda b, i: (b, 0, i, 0)) for d in dils[1:]]
    cur = pl.BlockSpec((None, 1, tm // 2, D_ATTN), lambda b, i: (b, 0, i, 0))
    prev = pl.BlockSpec((None, 1, blk // 2, D_ATTN), lambda b, i: (b, 0, jnp.maximum(n_sub * i - 1, 0), 0))
    bias, ones = _attn_tables(0, slopes0)
    head_rows = np.zeros((LANES, D_ATTN), np.float32)
    head_rows[np.arange(D_ATTN) // HEAD_DIM, np.arange(D_ATTN)] = 1.0
    expand = jnp.asarray(np.concatenate([head_rows, head_rows]), dtype=BF16)
    q0, k0, v0 = qkv0
    return pl.pallas_call(
        functools.partial(_combine_kernel, tm=tm, group=group),
        grid=(bsz, s // tm),
        in_specs=[cur, prev, cur, prev, cur, _resident(bias.shape), _resident(ones.shape)]
        + res_major(D_ATTN) + res_major(LANES) + [
            pl.BlockSpec((1, tm, d_model), lambda b, i: (b, i, 0)),
            _mod_spec(layer, d_model),
            _resident_layer(w_o.shape, w_o_layer),
            _resident(expand.shape),
        ],
        out_specs=pl.BlockSpec((1, tm, d_model), lambda b, i: (b, i, 0)),
        out_shape=jax.ShapeDtypeStruct(x.shape, F32),
        scratch_shapes=[pltpu.VMEM((1, tm, D_ATTN), F32),
                        pltpu.VMEM((1, tm, LANES), F32),
                        pltpu.VMEM((2, group, blk, 2 * blk), F32),
                        pltpu.VMEM((2, group, blk, 2 * blk), BF16),
                        pltpu.VMEM((N_BRANCHES - 1, D_ATTN // LANES, tm, LANES), F32),
                        pltpu.VMEM((N_BRANCHES - 1, tm, LANES), F32)],
        compiler_params=_params("arbitrary", "arbitrary"),
        name="branch_mix_out_proj",
    )(q0, k0, k0, v0, v0, bias, ones, *outs, *stats, x, mods, w_o, expand)


def kernel(x, c, ada_w, ada_b, norm1_g, norm2_g, pool_w_in, pool_w_grp, pool_scale, pool_w_out,
           kv_norm_g, kv_ada_w, kv_ada_b, w_kv, attn_w_q, attn_w_o,
           ffn_w_up, ffn_conv_w, ffn_conv_b, ffn_w_down, final_g):
    bsz, s, d = x.shape
    depth = ada_w.shape[0]
    n_pool = pool_w_in.shape[0]
    slopes = _alibi_slopes(N_BRANCHES * N_HEADS).reshape(N_BRANCHES, N_HEADS)

    mods = _ada(c, ada_w, ada_b).reshape(depth, bsz, 6, d)
    kv_mod = _ada(c, kv_ada_w[None], kv_ada_b[None]).reshape(bsz, 2, 1, d)

    pool_w = [w.astype(BF16) for w in (pool_w_in, pool_w_grp, pool_w_out)]
    w_kv_b, w_q_b, w_o_b = w_kv[None].astype(BF16), attn_w_q.astype(BF16), attn_w_o.astype(BF16)

    ks = vs = None
    for layer in range(depth):
        if layer < n_pool:
            x = _pool_layer(x, mods, layer, norm1_g[:n_pool], pool_w[0], pool_w[1], pool_scale, pool_w[2])
        else:
            if layer == n_pool:
                kv = _proj(x, kv_mod[:, 0], kv_mod[:, 1], kv_norm_g, w_kv_b, 0)
                ks, vs = kv[:N_BRANCHES], kv[N_BRANCHES:]
            jl = layer - n_pool
            mod = mods[layer]
            q = _proj(x, mod[:, 0:1], mod[:, 1:2], norm1_g[layer], w_q_b, jl, out_scale=HEAD_DIM ** -0.5)
            outs, stats = zip(*[_attn_branch(q[g], ks[g], vs[g], g, slopes[g]) for g in range(1, N_BRANCHES)])
            x = _combine((q[0], ks[0], vs[0]), slopes[0], outs, stats, x, mods, layer, w_o_b, jl)
        x = _ffn_layer(x, mods, layer, norm2_g, ffn_w_up, ffn_conv_w, ffn_conv_b, ffn_w_down, final_g,
                       final_norm=(layer == depth - 1))
    return x
```

```python
import functools
import math

import jax
import jax.numpy as jnp
import numpy as np
from jax import lax
from jax.experimental import pallas as pl
from jax.experimental.pallas import tpu as pltpu

D_MODEL = 1024
POOL_WINDOWS = (2, 4, 8, 16)
POOL_GROUP_DIM = D_MODEL // len(POOL_WINDOWS)
BRANCHES = ((128, 1), (512, 4), (2048, 16))
N_BRANCHES = len(BRANCHES)
HEAD_DIM = 64
N_HEADS = D_MODEL // HEAD_DIM
D_ATTN = N_HEADS * HEAD_DIM
ATTN_BLOCK = 128
D_FF = 2816
CONV_WIDTH = 3
EPS = 1e-6

LANES = 128
CARRY_ROWS = 8
UP_STAGE_ROWS = 32
DOWN_STAGE_ROWS = 128
STAGE_DEPTH = 4
VMEM_LIMIT = 56 * 1024 * 1024

BF16 = jnp.bfloat16
F32 = jnp.float32


def _alibi_slopes(n):
    def pow2(m):
        start = 2.0 ** (-(2.0 ** -(math.log2(m) - 3)))
        return [start ** (i + 1) for i in range(m)]
    if math.log2(n).is_integer():
        s = pow2(n)
    else:
        c = 2 ** math.floor(math.log2(n))
        s = pow2(c) + pow2(2 * c)[0::2][: n - c]
    s = np.asarray(s, dtype=np.float32)
    return -np.sort(-s)


def _params(*sem):
    return pltpu.CompilerParams(dimension_semantics=sem, vmem_limit_bytes=VMEM_LIMIT)


def _resident(shape):
    nd = len(shape)
    return pl.BlockSpec(shape, lambda *_: (0,) * nd, pipeline_mode=pl.Buffered(1))


def _resident_layer(shape, layer):
    nd = len(shape) - 1
    return pl.BlockSpec((None,) + tuple(shape[1:]), lambda *_: (layer,) + (0,) * nd,
                        pipeline_mode=pl.Buffered(1))


def _mod_spec(layer, d):
    return pl.BlockSpec((None, 1, 6, d), lambda b, i: (layer, b, 0, 0))


def _norm_mod(x, g, shift, scale):
    y = x * lax.rsqrt(jnp.mean(x * x, axis=-1, keepdims=True) + EPS)
    return (y * g) * (1.0 + scale) + shift


def _dot(a, b):
    return jnp.dot(a, b, preferred_element_type=F32)


def _pack(rows_bf16):
    return pltpu.bitcast(rows_bf16, jnp.uint32)


def _unpack(words):
    return pltpu.bitcast(words, BF16)


def _ada_kernel(c_ref, w_ref, b_ref, o_ref):
    c = c_ref[...]
    cond = c * (1.0 / (1.0 + jnp.exp(-c)))
    o_ref[0] = _dot(cond.astype(BF16), w_ref[0].astype(BF16)) + b_ref[0]


def _ada(c, w, b, tn=1024):
    n_layers, d, n = w.shape
    bsz = c.shape[0]
    return pl.pallas_call(
        _ada_kernel,
        grid=(n_layers, n // tn),
        in_specs=[
            pl.BlockSpec((bsz, d), lambda l, j: (0, 0)),
            pl.BlockSpec((1, d, tn), lambda l, j: (l, 0, j)),
            pl.BlockSpec((1, 1, tn), lambda l, j: (l, 0, j)),
        ],
        out_specs=pl.BlockSpec((1, bsz, tn), lambda l, j: (l, 0, j)),
        out_shape=jax.ShapeDtypeStruct((n_layers, bsz, n), F32),
        compiler_params=_params("arbitrary", "arbitrary"),
        name="ada_mod",
    )(c, w, b.reshape(n_layers, 1, n))


def _pool_kernel(x_ref, mod_ref, g_ref, win_ref, wgrp_ref, scale_ref, wout_ref, o_ref,
                 s1_ref, s2_ref, s4_ref, s8_ref, *, tm, n_split):
    si = pl.program_id(1)
    c0 = CARRY_ROWS
    gd = POOL_GROUP_DIM
    stages = (s1_ref, s2_ref, s4_ref, s8_ref)

    @pl.when(si == 0)
    def _():
        for ref in stages:
            ref[0:c0, :] = jnp.zeros((c0, ref.shape[1]), F32)

    mod = mod_ref[0]
    th = tm // n_split
    for part in range(n_split):
        r0 = part * th
        b0 = c0 + r0
        x = x_ref[0, r0:r0 + th, :]
        h = _norm_mod(x, g_ref[...], mod[0:1], mod[1:2])
        u = _dot(h.astype(BF16), win_ref[...])
        s1_ref[b0:b0 + th, :] = u
        for k in range(1, len(stages)):
            prev, cur = stages[k - 1], stages[k]
            sh = 1 << (k - 1)
            cur[b0:b0 + th, :] = prev[b0:b0 + th, gd:] + prev[b0 - sh:b0 - sh + th, gd:]

        t = si * tm + r0 + lax.broadcasted_iota(jnp.int32, (th, 1), 0)
        ys = []
        for g, w in enumerate(POOL_WINDOWS):
            ref = stages[g]
            sh = w // 2
            wsum = ref[b0:b0 + th, 0:gd] + ref[b0 - sh:b0 - sh + th, 0:gd]
            count = jnp.minimum(t + 1, w).astype(F32)
            pooled = wsum / count - u[:, g * gd:(g + 1) * gd]
            ys.append(_dot(pooled.astype(BF16), wgrp_ref[g]))
        y = jnp.concatenate(ys, axis=-1) * scale_ref[...]
        y = _dot(y.astype(BF16), wout_ref[...])
        o_ref[0, r0:r0 + th, :] = x + mod[2:3] * y

    for ref in stages:
        ref[0:c0, :] = ref[tm:tm + c0, :]


def _pool_layer(x, mods, layer, g, w_in, w_grp, scale, w_out, tm=1024, n_split=4):
    bsz, s, d = x.shape
    gd = POOL_GROUP_DIM
    g, scale = g[:, None, :], scale[:, None, :]
    return pl.pallas_call(
        functools.partial(_pool_kernel, tm=tm, n_split=n_split),
        grid=(bsz, s // tm),
        in_specs=[
            pl.BlockSpec((1, tm, d), lambda b, i: (b, i, 0)),
            _mod_spec(layer, d),
            _resident_layer(g.shape, layer),
            _resident_layer(w_in.shape, layer),
            _resident_layer(w_grp.shape, layer),
            _resident_layer(scale.shape, layer),
            _resident_layer(w_out.shape, layer),
        ],
        out_specs=pl.BlockSpec((1, tm, d), lambda b, i: (b, i, 0)),
        out_shape=jax.ShapeDtypeStruct(x.shape, F32),
        scratch_shapes=[pltpu.VMEM((tm + CARRY_ROWS, d - k * gd), F32) for k in range(4)],
        compiler_params=_params("arbitrary", "arbitrary"),
        name="pool_mixer",
    )(x, mods, g, w_in, w_grp, scale, w_out)


def _stage_bf16(jobs, depth):
    def chunk(job, c):
        w_hbm, layer, _, stage_ref, sem, rows = job
        slot = c % depth
        return pltpu.make_async_copy(w_hbm.at[layer, pl.ds(c * rows, rows), :], stage_ref.at[slot], sem.at[slot])

    counts = [job[2].shape[0] // job[5] for job in jobs]
    for job, n in zip(jobs, counts):
        for c in range(min(depth, n)):
            chunk(job, c).start()
    for c in range(max(counts)):
        for job, n in zip(jobs, counts):
            if c >= n:
                continue
            _, _, dst_ref, stage_ref, _, rows = job
            chunk(job, c).wait()
            dst_ref[c * rows:(c + 1) * rows, :] = stage_ref[c % depth].astype(BF16)
            if c + depth < n:
                chunk(job, c + depth).start()


def _ffn_kernel(x_ref, mod_ref, g_ref, wup_hbm, cw_ref, cb_ref, wdown_hbm, fg_ref, o_ref,
                a_ref, wup_ref, wdown_ref, up_stage, down_stage, up_sem, down_sem,
                *, tm, layer, final_norm):
    si = pl.program_id(1)
    c0 = CARRY_ROWS
    f = D_FF

    @pl.when((pl.program_id(0) == 0) & (si == 0))
    def _():
        _stage_bf16([(wup_hbm, layer, wup_ref, up_stage, up_sem, UP_STAGE_ROWS),
                     (wdown_hbm, layer, wdown_ref, down_stage, down_sem, DOWN_STAGE_ROWS)], STAGE_DEPTH)

    @pl.when(si == 0)
    def _():
        a_ref[0:c0, :] = jnp.zeros((c0, f), F32)

    x = x_ref[0]
    mod = mod_ref[0]
    h = _norm_mod(x, g_ref[...], mod[3:4], mod[4:5]).astype(BF16)
    a = _dot(h, wup_ref[:, 0:f])
    v = _dot(h, wup_ref[:, f:2 * f])
    a_ref[c0:c0 + tm, :] = a
    y = cb_ref[...]
    for k in range(CONV_WIDTH):
        lag = CONV_WIDTH - 1 - k
        tap = a if lag == 0 else a_ref[c0 - lag:c0 - lag + tm, :]
        y = y + tap * cw_ref[k:k + 1, :]
    gated = y * (1.0 / (1.0 + jnp.exp(-y))) * v
    out = x + mod[5:6] * _dot(gated.astype(BF16), wdown_ref[...])
    if final_norm:
        out = (out * lax.rsqrt(jnp.mean(out * out, axis=-1, keepdims=True) + EPS)) * fg_ref[...]
    o_ref[0] = out
    a_ref[0:c0, :] = a_ref[tm:tm + c0, :]


def _ffn_layer(x, mods, layer, g, w_up, conv_w, conv_b, w_down, final_g, final_norm, tm=512):
    bsz, s, d = x.shape
    f = D_FF
    g, conv_b = g[:, None, :], conv_b[:, None, :]
    assert d % UP_STAGE_ROWS == 0 and f % DOWN_STAGE_ROWS == 0
    return pl.pallas_call(
        functools.partial(_ffn_kernel, tm=tm, layer=layer, final_norm=final_norm),
        grid=(bsz, s // tm),
        in_specs=[
            pl.BlockSpec((1, tm, d), lambda b, i: (b, i, 0)),
            _mod_spec(layer, d),
            _resident_layer(g.shape, layer),
            pl.BlockSpec(memory_space=pl.ANY),
            _resident_layer(conv_w.shape, layer),
            _resident_layer(conv_b.shape, layer),
            pl.BlockSpec(memory_space=pl.ANY),
            _resident((1, d)),
        ],
        out_specs=pl.BlockSpec((1, tm, d), lambda b, i: (b, i, 0)),
        out_shape=jax.ShapeDtypeStruct(x.shape, F32),
        scratch_shapes=[pltpu.VMEM((tm + CARRY_ROWS, f), F32),
                        pltpu.VMEM((d, 2 * f), BF16),
                        pltpu.VMEM((f, d), BF16),
                        pltpu.VMEM((STAGE_DEPTH, UP_STAGE_ROWS, 2 * f), F32),
                        pltpu.VMEM((STAGE_DEPTH, DOWN_STAGE_ROWS, d), F32),
                        pltpu.SemaphoreType.DMA((STAGE_DEPTH,)),
                        pltpu.SemaphoreType.DMA((STAGE_DEPTH,))],
        compiler_params=_params("arbitrary", "arbitrary"),
        name="conv_ffn",
    )(x, mods, g, w_up, conv_w, conv_b, w_down, final_g.reshape(1, d))


def _proj_kernel(x_ref, shift_ref, scale_ref, g_ref, w_hbm, *rest, tm, n_tensors, layer, out_scale):
    n_out = n_tensors * N_BRANCHES
    o_refs, (h_ref, w_ref, stage_ref, sem) = rest[:n_out], rest[n_out:]

    @pl.when((pl.program_id(0) == 0) & (pl.program_id(1) == 0))
    def _():
        _stage_bf16([(w_hbm, layer, w_ref, stage_ref, sem, UP_STAGE_ROWS)], STAGE_DEPTH)

    h = _norm_mod(x_ref[0], g_ref[...], shift_ref[0], scale_ref[0])
    n_slabs = D_MODEL // LANES
    for c in range(n_slabs):
        h_ref[c] = h[:, c * LANES:(c + 1) * LANES]
    for g, (_, d) in enumerate(BRANCHES):
        n = tm // d
        if d == 1:
            hp = h.astype(BF16)
        else:
            hp = jnp.concatenate(
                [jnp.concatenate([h_ref[c, pl.ds(r, n, stride=d), :] for c in range(n_slabs)], axis=1)
                 for r in range(d)], axis=0).astype(BF16)
        for t in range(n_tensors):
            col = (t * N_BRANCHES + g) * D_ATTN
            res = _dot(hp, w_ref[:, col:col + D_ATTN])
            if out_scale != 1.0:
                res = res * out_scale
            o_ref = o_refs[t * N_BRANCHES + g]
            for r in range(d):
                o_ref[0, r] = _pack(res[r * n:(r + 1) * n].astype(BF16))


def _proj(x, shift, scale, g, w, layer, out_scale=1.0, tm=512):
    bsz, s, d_model = x.shape
    n_tensors = w.shape[2] // (N_BRANCHES * D_ATTN)
    dils = [d for _, d in BRANCHES] * n_tensors
    return pl.pallas_call(
        functools.partial(_proj_kernel, tm=tm, n_tensors=n_tensors, layer=layer, out_scale=out_scale),
        grid=(bsz, s // tm),
        in_specs=[
            pl.BlockSpec((1, tm, d_model), lambda b, i: (b, i, 0)),
            pl.BlockSpec((1, 1, d_model), lambda b, i: (b, 0, 0)),
            pl.BlockSpec((1, 1, d_model), lambda b, i: (b, 0, 0)),
            _resident((1, d_model)),
            pl.BlockSpec(memory_space=pl.ANY),
        ],
        out_specs=[pl.BlockSpec((1, d, tm // d // 2, D_ATTN), lambda b, i: (b, 0, i, 0)) for d in dils],
        out_shape=[jax.ShapeDtypeStruct((bsz, d, s // d // 2, D_ATTN), jnp.uint32) for d in dils],
        scratch_shapes=[pltpu.VMEM((d_model // LANES, tm, LANES), F32),
                        pltpu.VMEM(w.shape[1:], BF16),
                        pltpu.VMEM((STAGE_DEPTH, UP_STAGE_ROWS, w.shape[2]), F32),
                        pltpu.SemaphoreType.DMA((STAGE_DEPTH,))],
        compiler_params=_params("arbitrary", "arbitrary"),
        name="norm_proj",
    )(x, shift, scale, g.reshape(1, d_model), w)


def _bias_table(slopes, dilation, n_steps):
    blk = ATTN_BLOCK
    row = np.arange(blk)[:, None]
    col = np.arange(2 * blk)[None, :]
    delta = row + blk - col
    valid = (delta >= 0) & (delta <= n_steps)
    dist = (delta * dilation).astype(np.float32)
    bias = -np.asarray(slopes, np.float32)[:, None, None] * dist[None]
    later = np.where(valid[None], bias, -np.inf).astype(np.float32)
    first = np.where((valid & (col >= blk))[None], bias, -np.inf).astype(np.float32)
    return np.stack([first, later])


def _ones_table():
    blk = ATTN_BLOCK
    t = np.zeros((N_HEADS // 2, 4 * blk, LANES), np.float32)
    for p in range(N_HEADS // 2):
        t[p, :2 * blk, N_HEADS + 2 * p] = 1.0
        t[p, 2 * blk:, N_HEADS + 2 * p + 1] = 1.0
    return t


def _attend(q_ref, kp_ref, kc_ref, vp_ref, vc_ref, bias_ref, ones_ref, o_ref, st_ref, s_scr, p_scr,
            *, first, n_res, n_sub, group):
    blk = ATTN_BLOCK
    lane = lax.broadcasted_iota(jnp.int32, (blk, LANES), 1)
    low_half = lane < HEAD_DIM
    half_blk = blk // 2

    def solo(sb):
        return sb == 0 and first is None

    def keys(res, sb, prev_ref, cur_ref, cs):
        if solo(sb):
            return _unpack(cur_ref[res, 0:half_blk, cs])
        if sb == 0:
            return jnp.concatenate(
                [_unpack(prev_ref[res, :, cs]), _unpack(cur_ref[res, 0:half_blk, cs])], axis=0)
        return _unpack(cur_ref[res, (sb - 1) * half_blk:(sb + 1) * half_blk, cs])

    units = [(res, sb, g0) for res in range(n_res) for sb in range(n_sub)
             for g0 in range(0, N_HEADS, group)]
    unit_slot = {u: i % 2 for i, u in enumerate(units)}
    ms = {}
    m_mats = {(res, sb): jnp.zeros((blk, LANES), F32) for res in range(n_res) for sb in range(n_sub)}
    l_mats = dict(m_mats)

    def scores(res, sb, g0):
        variant = first if sb == 0 else 1
        kw = blk if solo(sb) else 2 * blk
        slot = unit_slot[res, sb, g0]
        for pair in range(g0 // 2, (g0 + group) // 2):
            cs = slice(pair * LANES, (pair + 1) * LANES)
            q = _unpack(q_ref[res, sb * half_blk:(sb + 1) * half_blk, cs])
            zq = jnp.zeros_like(q)
            q2 = jnp.concatenate([jnp.where(low_half, q, zq), jnp.where(low_half, zq, q)], axis=0)
            s2 = lax.dot_general(q2, keys(res, sb, kp_ref, kc_ref, cs), (((1,), (1,)), ((), ())),
                                 preferred_element_type=F32)
            for half in range(2):
                head = 2 * pair + half
                bias = bias_ref[1, head, :, blk:] if solo(sb) else bias_ref[variant, head]
                s = s2[half * blk:(half + 1) * blk] + bias
                s_scr[slot, head - g0, :, 0:kw] = s
                ms[res, sb, head] = jnp.max(s, axis=-1, keepdims=True)

    def probs(res, sb, g0):
        slot = unit_slot[res, sb, g0]
        kw = blk if solo(sb) else 2 * blk
        for head in range(g0, g0 + group):
            p_scr[slot, head - g0, :, 0:kw] = jnp.exp(
                s_scr[slot, head - g0, :, 0:kw] - ms[res, sb, head]).astype(BF16)

    def values(res, sb, g0):
        slot = unit_slot[res, sb, g0]
        rows = slice(sb * blk, (sb + 1) * blk)
        for pair in range(g0 // 2, (g0 + group) // 2):
            cs = slice(pair * LANES, (pair + 1) * LANES)
            v = keys(res, sb, vp_ref, vc_ref, cs)
            kw = v.shape[0]
            zv = jnp.zeros_like(v)
            keep = lax.broadcasted_iota(jnp.int32, (kw, LANES), 1) < HEAD_DIM
            v2 = jnp.concatenate([jnp.where(keep, v, zv), jnp.where(keep, zv, v)], axis=0)
            ones = jnp.concatenate([ones_ref[pair, 0:kw], ones_ref[pair, 2 * blk:2 * blk + kw]], axis=0)
            v2 = jnp.concatenate([v2, ones], axis=1)
            p2 = jnp.concatenate([p_scr[slot, 2 * pair - g0, :, 0:kw],
                                  p_scr[slot, 2 * pair + 1 - g0, :, 0:kw]], axis=1)
            u = _dot(p2, v2)
            o_ref[res, rows, cs] = u[:, :LANES]
            l_mats[res, sb] = l_mats[res, sb] + u[:, LANES:]
            for head in (2 * pair, 2 * pair + 1):
                m_mats[res, sb] = jnp.where(lane == head, ms[res, sb, head], m_mats[res, sb])

    scores(*units[0])
    for i, unit in enumerate(units):
        if i + 1 < len(units):
            scores(*units[i + 1])
        probs(*unit)
        values(*unit)
    for (res, sb), m_mat in m_mats.items():
        st_ref[res, sb * blk:(sb + 1) * blk, :] = m_mat + l_mats[res, sb]


def _attn_kernel(*refs, n_res, n_sub, group, single_step):
    if single_step:
        q_ref, kc_ref, vc_ref = refs[:3]
        _attend(q_ref, None, kc_ref, None, vc_ref, *refs[3:], first=None, n_res=n_res, n_sub=n_sub, group=group)
    else:
        _attend(*refs, first=jnp.minimum(pl.program_id(2), 1), n_res=n_res, n_sub=n_sub, group=group)


def _attn_tables(branch, slopes):
    window, d = BRANCHES[branch]
    assert window // d <= ATTN_BLOCK and 2 * N_HEADS <= LANES
    return jnp.asarray(_bias_table(slopes, d, window // d)), jnp.asarray(_ones_table(), dtype=BF16)


def _head_group(n_res):
    return 2 if n_res == 1 else 8


def _attn_branch(q, k, v, branch, slopes, blocks_per_step=8):
    window, d = BRANCHES[branch]
    bsz, _, packed_sub, _ = q.shape
    sub = 2 * packed_sub
    blk = ATTN_BLOCK
    n_sub = min(blocks_per_step, sub // blk)
    n_res = min(blocks_per_step // n_sub, d)
    group = _head_group(n_res)
    qb = n_sub * blk
    assert sub % qb == 0 and d % n_res == 0
    bias, ones = _attn_tables(branch, slopes)
    cur = lambda rows, w: pl.BlockSpec((None, n_res, rows, w), lambda b, r, j: (b, r, j, 0))
    prev = pl.BlockSpec((None, n_res, blk // 2, D_ATTN),
                        lambda b, r, j: (b, r, jnp.maximum(n_sub * j - 1, 0), 0))
    cur_in = cur(qb // 2, D_ATTN)
    single_step = sub == qb
    if single_step:
        qkv_specs, qkv = [cur_in, cur_in, cur_in], (q, k, v)
    else:
        qkv_specs, qkv = [cur_in, prev, cur_in, prev, cur_in], (q, k, k, v, v)
    return pl.pallas_call(
        functools.partial(_attn_kernel, n_res=n_res, n_sub=n_sub, group=group, single_step=single_step),
        grid=(bsz, d // n_res, sub // qb),
        in_specs=qkv_specs + [_resident(bias.shape), _resident(ones.shape)],
        out_specs=[cur(qb, D_ATTN), cur(qb, LANES)],
        out_shape=[
            jax.ShapeDtypeStruct((bsz, d, sub, D_ATTN), F32),
            jax.ShapeDtypeStruct((bsz, d, sub, LANES), F32),
        ],
        scratch_shapes=[pltpu.VMEM((2, group, blk, 2 * blk), F32),
                        pltpu.VMEM((2, group, blk, 2 * blk), BF16)],
        compiler_params=_params("arbitrary", "arbitrary", "arbitrary"),
        name=f"dilated_attn_{branch}",
    )(*qkv, bias, ones)


def _combine_kernel(q_ref, kp_ref, kc_ref, vp_ref, vc_ref, bias_ref, ones_ref,
                    o1_ref, o2_ref, l1_ref, l2_ref, x_ref, mod_ref, wo_ref, expand_ref,
                    out_ref, o0_ref, l0_ref, s_scr, p_scr, nat_o_ref, nat_l_ref, *, tm, group):
    _attend(q_ref, kp_ref, kc_ref, vp_ref, vc_ref, bias_ref, ones_ref, o0_ref, l0_ref, s_scr, p_scr,
            first=jnp.minimum(pl.program_id(1), 1), n_res=1, n_sub=tm // ATTN_BLOCK, group=group)
    o_refs = (o0_ref, o1_ref, o2_ref)
    l_refs = (l0_ref, l1_ref, l2_ref)
    n_slabs = D_ATTN // LANES
    for g, (_, d) in enumerate(BRANCHES):
        if d == 1:
            continue
        n = tm // d
        for r in range(d):
            nat_l_ref[g - 1, pl.ds(r, n, stride=d), :] = l_refs[g][r]
            for c in range(n_slabs):
                nat_o_ref[g - 1, c, pl.ds(r, n, stride=d), :] = o_refs[g][r, :, c * LANES:(c + 1) * LANES]
    stats = [l0_ref[0], nat_l_ref[0], nat_l_ref[1]]
    m = jnp.maximum(jnp.maximum(stats[0], stats[1]), stats[2])
    es = [jnp.exp(st - m) for st in stats]
    sums = [pltpu.roll(st, LANES - N_HEADS, axis=1) for st in stats]
    inv = 1.0 / (es[0] * sums[0] + es[1] * sums[1] + es[2] * sums[2])
    wts = [e * inv for e in es]
    head_lanes = lax.broadcasted_iota(jnp.int32, (tm, LANES), 1) < N_HEADS
    splits = []
    for w in wts:
        w = jnp.where(head_lanes, w, 0.0)
        hi = w.astype(BF16)
        lo = (w - hi.astype(F32)).astype(BF16)
        splits.append(jnp.concatenate([hi, lo], axis=1))
    cols = []
    wide = 2 * LANES
    for c in range(D_ATTN // wide):
        cs = slice(c * wide, (c + 1) * wide)
        outs = [o0_ref[0, :, cs]] + [
            jnp.concatenate([nat_o_ref[g, 2 * c], nat_o_ref[g, 2 * c + 1]], axis=1) for g in range(2)]
        acc = None
        for g in range(N_BRANCHES):
            term = _dot(splits[g], expand_ref[:, cs]) * outs[g]
            acc = term if acc is None else acc + term
        cols.append(acc.astype(BF16))
    o = jnp.concatenate(cols, axis=-1)
    mod = mod_ref[0]
    out_ref[0] = x_ref[0] + mod[2:3] * _dot(o, wo_ref[...])


def _combine(qkv0, slopes0, outs, stats, x, mods, layer, w_o, w_o_layer, tm=512):
    bsz, s, d_model = x.shape
    dils = [d for _, d in BRANCHES]
    assert dils[0] == 1
    blk = ATTN_BLOCK
    n_sub = tm // blk
    group = _head_group(1)
    res_major = lambda w: [pl.BlockSpec((None, d, tm // d, w), lambda b, i: (b, 0, i, 0)) for d in dils[1:]]
    cur = pl.BlockSpec((None, 1, tm // 2, D_ATTN), lambda b, i: (b, 0, i, 0))
    prev = pl.BlockSpec((None, 1, blk // 2, D_ATTN), lambda b, i: (b, 0, jnp.maximum(n_sub * i - 1, 0), 0))
    bias, ones = _attn_tables(0, slopes0)
    head_rows = np.zeros((LANES, D_ATTN), np.float32)
    head_rows[np.arange(D_ATTN) // HEAD_DIM, np.arange(D_ATTN)] = 1.0
    expand = jnp.asarray(np.concatenate([head_rows, head_rows]), dtype=BF16)
    q0, k0, v0 = qkv0
    return pl.pallas_call(
        functools.partial(_combine_kernel, tm=tm, group=group),
        grid=(bsz, s // tm),
        in_specs=[cur, prev, cur, prev, cur, _resident(bias.shape), _resident(ones.shape)]
        + res_major(D_ATTN) + res_major(LANES) + [
            pl.BlockSpec((1, tm, d_model), lambda b, i: (b, i, 0)),
            _mod_spec(layer, d_model),
            _resident_layer(w_o.shape, w_o_layer),
            _resident(expand.shape),
        ],
        out_specs=pl.BlockSpec((1, tm, d_model), lambda b, i: (b, i, 0)),
        out_shape=jax.ShapeDtypeStruct(x.shape, F32),
        scratch_shapes=[pltpu.VMEM((1, tm, D_ATTN), F32),
                        pltpu.VMEM((1, tm, LANES), F32),
                        pltpu.VMEM((2, group, blk, 2 * blk), F32),
                        pltpu.VMEM((2, group, blk, 2 * blk), BF16),
                        pltpu.VMEM((N_BRANCHES - 1, D_ATTN // LANES, tm, LANES), F32),
                        pltpu.VMEM((N_BRANCHES - 1, tm, LANES), F32)],
        compiler_params=_params("arbitrary", "arbitrary"),
        name="branch_mix_out_proj",
    )(q0, k0, k0, v0, v0, bias, ones, *outs, *stats, x, mods, w_o, expand)


def kernel(x, c, ada_w, ada_b, norm1_g, norm2_g, pool_w_in, pool_w_grp, pool_scale, pool_w_out,
           kv_norm_g, kv_ada_w, kv_ada_b, w_kv, attn_w_q, attn_w_o,
           ffn_w_up, ffn_conv_w, ffn_conv_b, ffn_w_down, final_g):
    bsz, s, d = x.shape
    depth = ada_w.shape[0]
    n_pool = pool_w_in.shape[0]
    slopes = _alibi_slopes(N_BRANCHES * N_HEADS).reshape(N_BRANCHES, N_HEADS)

    mods = _ada(c, ada_w, ada_b).reshape(depth, bsz, 6, d)
    kv_mod = _ada(c, kv_ada_w[None], kv_ada_b[None]).reshape(bsz, 2, 1, d)

    pool_w = [w.astype(BF16) for w in (pool_w_in, pool_w_grp, pool_w_out)]
    w_o_b = attn_w_o.astype(BF16)

    ks = vs = None
    for layer in range(depth):
        if layer < n_pool:
            x = _pool_layer(x, mods, layer, norm1_g[:n_pool], pool_w[0], pool_w[1], pool_scale, pool_w[2])
        else:
            if layer == n_pool:
                kv = _proj(x, kv_mod[:, 0], kv_mod[:, 1], kv_norm_g, w_kv[None], 0)
                ks, vs = kv[:N_BRANCHES], kv[N_BRANCHES:]
            jl = layer - n_pool
            mod = mods[layer]
            q = _proj(x, mod[:, 0:1], mod[:, 1:2], norm1_g[layer], attn_w_q, jl, out_scale=HEAD_DIM ** -0.5)
            outs, stats = zip(*[_attn_branch(q[g], ks[g], vs[g], g, slopes[g]) for g in range(1, N_BRANCHES)])
            x = _combine((q[0], ks[0], vs[0]), slopes[0], outs, stats, x, mods, layer, w_o_b, jl)
        x = _ffn_layer(x, mods, layer, norm2_g, ffn_w_up, ffn_conv_w, ffn_conv_b, ffn_w_down, final_g,
                       final_norm=(layer == depth - 1))
    return x
```

```python
import functools
import math

import jax
import jax.numpy as jnp
import numpy as np
from jax import lax
from jax.experimental import pallas as pl
from jax.experimental.pallas import tpu as pltpu

D_MODEL = 1024
POOL_WINDOWS = (2, 4, 8, 16)
POOL_GROUP_DIM = D_MODEL // len(POOL_WINDOWS)
BRANCHES = ((128, 1), (512, 4), (2048, 16))
N_BRANCHES = len(BRANCHES)
HEAD_DIM = 64
N_HEADS = D_MODEL // HEAD_DIM
D_ATTN = N_HEADS * HEAD_DIM
ATTN_BLOCK = 128
D_FF = 2816
CONV_WIDTH = 3
EPS = 1e-6

LANES = 128
CARRY_ROWS = 8
UP_STAGE_ROWS = 32
DOWN_STAGE_ROWS = 128
STAGE_DEPTH = 4
VMEM_LIMIT = 56 * 1024 * 1024

BF16 = jnp.bfloat16
F32 = jnp.float32


def _alibi_slopes(n):
    def pow2(m):
        start = 2.0 ** (-(2.0 ** -(math.log2(m) - 3)))
        return [start ** (i + 1) for i in range(m)]
    if math.log2(n).is_integer():
        s = pow2(n)
    else:
        c = 2 ** math.floor(math.log2(n))
        s = pow2(c) + pow2(2 * c)[0::2][: n - c]
    s = np.asarray(s, dtype=np.float32)
    return -np.sort(-s)


def _params(*sem):
    return pltpu.CompilerParams(dimension_semantics=sem, vmem_limit_bytes=VMEM_LIMIT)


def _resident(shape):
    nd = len(shape)
    return pl.BlockSpec(shape, lambda *_: (0,) * nd, pipeline_mode=pl.Buffered(1))


def _resident_layer(shape, layer):
    nd = len(shape) - 1
    return pl.BlockSpec((None,) + tuple(shape[1:]), lambda *_: (layer,) + (0,) * nd,
                        pipeline_mode=pl.Buffered(1))


def _mod_spec(layer, d):
    return pl.BlockSpec((None, 1, 6, d), lambda b, i: (layer, b, 0, 0))


def _norm_mod(x, g, shift, scale):
    y = x * lax.rsqrt(jnp.mean(x * x, axis=-1, keepdims=True) + EPS)
    return (y * g) * (1.0 + scale) + shift


def _dot(a, b):
    return jnp.dot(a, b, preferred_element_type=F32)


def _pack(rows_bf16):
    return pltpu.bitcast(rows_bf16, jnp.uint32)


def _unpack(words):
    return pltpu.bitcast(words, BF16)


def _ada_kernel(c_ref, w_ref, b_ref, o_ref):
    c = c_ref[...]
    cond = c * (1.0 / (1.0 + jnp.exp(-c)))
    o_ref[0] = _dot(cond.astype(BF16), w_ref[0].astype(BF16)) + b_ref[0]


def _ada(c, w, b, tn=1024):
    n_layers, d, n = w.shape
    bsz = c.shape[0]
    return pl.pallas_call(
        _ada_kernel,
        grid=(n_layers, n // tn),
        in_specs=[
            pl.BlockSpec((bsz, d), lambda l, j: (0, 0)),
            pl.BlockSpec((1, d, tn), lambda l, j: (l, 0, j)),
            pl.BlockSpec((1, 1, tn), lambda l, j: (l, 0, j)),
        ],
        out_specs=pl.BlockSpec((1, bsz, tn), lambda l, j: (l, 0, j)),
        out_shape=jax.ShapeDtypeStruct((n_layers, bsz, n), F32),
        compiler_params=_params("arbitrary", "arbitrary"),
        name="ada_mod",
    )(c, w, b.reshape(n_layers, 1, n))


def _pool_kernel(x_ref, mod_ref, g_ref, win_ref, wgrp_ref, scale_ref, wout_ref, o_ref,
                 s1_ref, s2_ref, s4_ref, s8_ref, *, tm, n_split):
    si = pl.program_id(1)
    c0 = CARRY_ROWS
    gd = POOL_GROUP_DIM
    stages = (s1_ref, s2_ref, s4_ref, s8_ref)

    @pl.when(si == 0)
    def _():
        for ref in stages:
            ref[0:c0, :] = jnp.zeros((c0, ref.shape[1]), F32)

    mod = mod_ref[0]
    th = tm // n_split
    for part in range(n_split):
        r0 = part * th
        b0 = c0 + r0
        x = x_ref[0, r0:r0 + th, :]
        h = _norm_mod(x, g_ref[...], mod[0:1], mod[1:2])
        u = _dot(h.astype(BF16), win_ref[...])
        s1_ref[b0:b0 + th, :] = u
        for k in range(1, len(stages)):
            prev, cur = stages[k - 1], stages[k]
            sh = 1 << (k - 1)
            cur[b0:b0 + th, :] = prev[b0:b0 + th, gd:] + prev[b0 - sh:b0 - sh + th, gd:]

        t = si * tm + r0 + lax.broadcasted_iota(jnp.int32, (th, 1), 0)
        ys = []
        for g, w in enumerate(POOL_WINDOWS):
            ref = stages[g]
            sh = w // 2
            wsum = ref[b0:b0 + th, 0:gd] + ref[b0 - sh:b0 - sh + th, 0:gd]
            count = jnp.minimum(t + 1, w).astype(F32)
            pooled = wsum / count - u[:, g * gd:(g + 1) * gd]
            ys.append(_dot(pooled.astype(BF16), wgrp_ref[g]))
        y = jnp.concatenate(ys, axis=-1) * scale_ref[...]
        y = _dot(y.astype(BF16), wout_ref[...])
        o_ref[0, r0:r0 + th, :] = x + mod[2:3] * y

    for ref in stages:
        ref[0:c0, :] = ref[tm:tm + c0, :]


def _pool_layer(x, mods, layer, g, w_in, w_grp, scale, w_out, tm=1024, n_split=4):
    bsz, s, d = x.shape
    gd = POOL_GROUP_DIM
    g, scale = g[:, None, :], scale[:, None, :]
    return pl.pallas_call(
        functools.partial(_pool_kernel, tm=tm, n_split=n_split),
        grid=(bsz, s // tm),
        in_specs=[
            pl.BlockSpec((1, tm, d), lambda b, i: (b, i, 0)),
            _mod_spec(layer, d),
            _resident_layer(g.shape, layer),
            _resident_layer(w_in.shape, layer),
            _resident_layer(w_grp.shape, layer),
            _resident_layer(scale.shape, layer),
            _resident_layer(w_out.shape, layer),
        ],
        out_specs=pl.BlockSpec((1, tm, d), lambda b, i: (b, i, 0)),
        out_shape=jax.ShapeDtypeStruct(x.shape, F32),
        scratch_shapes=[pltpu.VMEM((tm + CARRY_ROWS, d - k * gd), F32) for k in range(4)],
        compiler_params=_params("arbitrary", "arbitrary"),
        name="pool_mixer",
    )(x, mods, g, w_in, w_grp, scale, w_out)


def _stage_bf16(jobs, depth):
    def chunk(job, c):
        w_hbm, layer, _, stage_ref, sem, rows = job
        slot = c % depth
        return pltpu.make_async_copy(w_hbm.at[layer, pl.ds(c * rows, rows), :], stage_ref.at[slot], sem.at[slot])

    counts = [job[2].shape[0] // job[5] for job in jobs]
    for job, n in zip(jobs, counts):
        for c in range(min(depth, n)):
            chunk(job, c).start()
    for c in range(max(counts)):
        for job, n in zip(jobs, counts):
            if c >= n:
                continue
            _, _, dst_ref, stage_ref, _, rows = job
            chunk(job, c).wait()
            dst_ref[c * rows:(c + 1) * rows, :] = stage_ref[c % depth].astype(BF16)
            if c + depth < n:
                chunk(job, c + depth).start()


def _ffn_kernel(x_ref, mod_ref, g_ref, wup_hbm, cw_ref, cb_ref, wdown_hbm, fg_ref, o_ref,
                a_ref, wup_ref, wdown_ref, up_stage, down_stage, up_sem, down_sem,
                *, tm, layer, final_norm):
    si = pl.program_id(1)
    c0 = CARRY_ROWS
    f = D_FF

    @pl.when((pl.program_id(0) == 0) & (si == 0))
    def _():
        _stage_bf16([(wup_hbm, layer, wup_ref, up_stage, up_sem, UP_STAGE_ROWS),
                     (wdown_hbm, layer, wdown_ref, down_stage, down_sem, DOWN_STAGE_ROWS)], STAGE_DEPTH)

    @pl.when(si == 0)
    def _():
        a_ref[0:c0, :] = jnp.zeros((c0, f), F32)

    x = x_ref[0]
    mod = mod_ref[0]
    h = _norm_mod(x, g_ref[...], mod[3:4], mod[4:5]).astype(BF16)
    a = _dot(h, wup_ref[:, 0:f])
    v = _dot(h, wup_ref[:, f:2 * f])
    a_ref[c0:c0 + tm, :] = a
    y = cb_ref[...]
    for k in range(CONV_WIDTH):
        lag = CONV_WIDTH - 1 - k
        tap = a if lag == 0 else a_ref[c0 - lag:c0 - lag + tm, :]
        y = y + tap * cw_ref[k:k + 1, :]
    gated = y * (1.0 / (1.0 + jnp.exp(-y))) * v
    out = x + mod[5:6] * _dot(gated.astype(BF16), wdown_ref[...])
    if final_norm:
        out = (out * lax.rsqrt(jnp.mean(out * out, axis=-1, keepdims=True) + EPS)) * fg_ref[...]
    o_ref[0] = out
    a_ref[0:c0, :] = a_ref[tm:tm + c0, :]


def _ffn_layer(x, mods, layer, g, w_up, conv_w, conv_b, w_down, final_g, final_norm, tm=512):
    bsz, s, d = x.shape
    f = D_FF
    g, conv_b = g[:, None, :], conv_b[:, None, :]
    assert d % UP_STAGE_ROWS == 0 and f % DOWN_STAGE_ROWS == 0
    return pl.pallas_call(
        functools.partial(_ffn_kernel, tm=tm, layer=layer, final_norm=final_norm),
        grid=(bsz, s // tm),
        in_specs=[
            pl.BlockSpec((1, tm, d), lambda b, i: (b, i, 0)),
            _mod_spec(layer, d),
            _resident_layer(g.shape, layer),
            pl.BlockSpec(memory_space=pl.ANY),
            _resident_layer(conv_w.shape, layer),
            _resident_layer(conv_b.shape, layer),
            pl.BlockSpec(memory_space=pl.ANY),
            _resident((1, d)),
        ],
        out_specs=pl.BlockSpec((1, tm, d), lambda b, i: (b, i, 0)),
        out_shape=jax.ShapeDtypeStruct(x.shape, F32),
        scratch_shapes=[pltpu.VMEM((tm + CARRY_ROWS, f), F32),
                        pltpu.VMEM((d, 2 * f), BF16),
                        pltpu.VMEM((f, d), BF16),
                        pltpu.VMEM((STAGE_DEPTH, UP_STAGE_ROWS, 2 * f), F32),
                        pltpu.VMEM((STAGE_DEPTH, DOWN_STAGE_ROWS, d), F32),
                        pltpu.SemaphoreType.DMA((STAGE_DEPTH,)),
                        pltpu.SemaphoreType.DMA((STAGE_DEPTH,))],
        compiler_params=_params("arbitrary", "arbitrary"),
        name="conv_ffn",
    )(x, mods, g, w_up, conv_w, conv_b, w_down, final_g.reshape(1, d))


def _proj_kernel(x_ref, shift_ref, scale_ref, g_ref, w_ref, *rest, tm, n_tensors, out_scale):
    o_refs, h_ref = rest[:-1], rest[-1]
    h = _norm_mod(x_ref[0], g_ref[...], shift_ref[0], scale_ref[0])
    n_slabs = D_MODEL // LANES
    for c in range(n_slabs):
        h_ref[c] = h[:, c * LANES:(c + 1) * LANES]
    for g, (_, d) in enumerate(BRANCHES):
        n = tm // d
        if d == 1:
            hp = h.astype(BF16)
        else:
            hp = jnp.concatenate(
                [jnp.concatenate([h_ref[c, pl.ds(r, n, stride=d), :] for c in range(n_slabs)], axis=1)
                 for r in range(d)], axis=0).astype(BF16)
        for t in range(n_tensors):
            col = (t * N_BRANCHES + g) * D_ATTN
            res = _dot(hp, w_ref[:, col:col + D_ATTN])
            if out_scale != 1.0:
                res = res * out_scale
            o_ref = o_refs[t * N_BRANCHES + g]
            for r in range(d):
                o_ref[0, r] = _pack(res[r * n:(r + 1) * n].astype(BF16))


def _proj(x, shift, scale, g, w, layer, out_scale=1.0, tm=512):
    bsz, s, d_model = x.shape
    n_tensors = w.shape[2] // (N_BRANCHES * D_ATTN)
    dils = [d for _, d in BRANCHES] * n_tensors
    return pl.pallas_call(
        functools.partial(_proj_kernel, tm=tm, n_tensors=n_tensors, out_scale=out_scale),
        grid=(bsz, s // tm),
        in_specs=[
            pl.BlockSpec((1, tm, d_model), lambda b, i: (b, i, 0)),
            pl.BlockSpec((1, 1, d_model), lambda b, i: (b, 0, 0)),
            pl.BlockSpec((1, 1, d_model), lambda b, i: (b, 0, 0)),
            _resident((1, d_model)),
            _resident_layer(w.shape, layer),
        ],
        out_specs=[pl.BlockSpec((1, d, tm // d // 2, D_ATTN), lambda b, i: (b, 0, i, 0)) for d in dils],
        out_shape=[jax.ShapeDtypeStruct((bsz, d, s // d // 2, D_ATTN), jnp.uint32) for d in dils],
        scratch_shapes=[pltpu.VMEM((d_model // LANES, tm, LANES), F32)],
        compiler_params=_params("arbitrary", "arbitrary"),
        name="norm_proj",
    )(x, shift, scale, g.reshape(1, d_model), w)


def _bias_table(slopes, dilation, n_steps):
    blk = ATTN_BLOCK
    row = np.arange(blk)[:, None]
    col = np.arange(2 * blk)[None, :]
    delta = row + blk - col
    valid = (delta >= 0) & (delta <= n_steps)
    dist = (delta * dilation).astype(np.float32)
    bias = -np.asarray(slopes, np.float32)[:, None, None] * dist[None]
    later = np.where(valid[None], bias, -np.inf).astype(np.float32)
    first = np.where((valid & (col >= blk))[None], bias, -np.inf).astype(np.float32)
    return np.stack([first, later])


def _ones_table():
    blk = ATTN_BLOCK
    t = np.zeros((N_HEADS // 2, 4 * blk, LANES), np.float32)
    for p in range(N_HEADS // 2):
        t[p, :2 * blk, N_HEADS + 2 * p] = 1.0
        t[p, 2 * blk:, N_HEADS + 2 * p + 1] = 1.0
    return t


def _attend(q_ref, kp_ref, kc_ref, vp_ref, vc_ref, bias_ref, ones_ref, o_ref, st_ref, s_scr, p_scr,
            *, first, n_res, n_sub, group):
    blk = ATTN_BLOCK
    lane = lax.broadcasted_iota(jnp.int32, (blk, LANES), 1)
    low_half = lane < HEAD_DIM
    half_blk = blk // 2

    def solo(sb):
        return sb == 0 and first is None

    def keys(res, sb, prev_ref, cur_ref, cs):
        if solo(sb):
            return _unpack(cur_ref[res, 0:half_blk, cs])
        if sb == 0:
            return jnp.concatenate(
                [_unpack(prev_ref[res, :, cs]), _unpack(cur_ref[res, 0:half_blk, cs])], axis=0)
        return _unpack(cur_ref[res, (sb - 1) * half_blk:(sb + 1) * half_blk, cs])

    units = [(res, sb, g0) for res in range(n_res) for sb in range(n_sub)
             for g0 in range(0, N_HEADS, group)]
    unit_slot = {u: i % 2 for i, u in enumerate(units)}
    ms = {}
    m_mats = {(res, sb): jnp.zeros((blk, LANES), F32) for res in range(n_res) for sb in range(n_sub)}
    l_mats = dict(m_mats)

    def scores(res, sb, g0):
        variant = first if sb == 0 else 1
        kw = blk if solo(sb) else 2 * blk
        slot = unit_slot[res, sb, g0]
        for pair in range(g0 // 2, (g0 + group) // 2):
            cs = slice(pair * LANES, (pair + 1) * LANES)
            q = _unpack(q_ref[res, sb * half_blk:(sb + 1) * half_blk, cs])
            zq = jnp.zeros_like(q)
            q2 = jnp.concatenate([jnp.where(low_half, q, zq), jnp.where(low_half, zq, q)], axis=0)
            s2 = lax.dot_general(q2, keys(res, sb, kp_ref, kc_ref, cs), (((1,), (1,)), ((), ())),
                                 preferred_element_type=F32)
            for half in range(2):
                head = 2 * pair + half
                bias = bias_ref[1, head, :, blk:] if solo(sb) else bias_ref[variant, head]
                s = s2[half * blk:(half + 1) * blk] + bias
                s_scr[slot, head - g0, :, 0:kw] = s
                ms[res, sb, head] = jnp.max(s, axis=-1, keepdims=True)

    def probs(res, sb, g0):
        slot = unit_slot[res, sb, g0]
        kw = blk if solo(sb) else 2 * blk
        for head in range(g0, g0 + group):
            p_scr[slot, head - g0, :, 0:kw] = jnp.exp(
                s_scr[slot, head - g0, :, 0:kw] - ms[res, sb, head]).astype(BF16)

    def values(res, sb, g0):
        slot = unit_slot[res, sb, g0]
        rows = slice(sb * blk, (sb + 1) * blk)
        for pair in range(g0 // 2, (g0 + group) // 2):
            cs = slice(pair * LANES, (pair + 1) * LANES)
            v = keys(res, sb, vp_ref, vc_ref, cs)
            kw = v.shape[0]
            zv = jnp.zeros_like(v)
            keep = lax.broadcasted_iota(jnp.int32, (kw, LANES), 1) < HEAD_DIM
            v2 = jnp.concatenate([jnp.where(keep, v, zv), jnp.where(keep, zv, v)], axis=0)
            ones = jnp.concatenate([ones_ref[pair, 0:kw], ones_ref[pair, 2 * blk:2 * blk + kw]], axis=0)
            v2 = jnp.concatenate([v2, ones], axis=1)
            p2 = jnp.concatenate([p_scr[slot, 2 * pair - g0, :, 0:kw],
                                  p_scr[slot, 2 * pair + 1 - g0, :, 0:kw]], axis=1)
            u = _dot(p2, v2)
            o_ref[res, rows, cs] = u[:, :LANES]
            l_mats[res, sb] = l_mats[res, sb] + u[:, LANES:]
            for head in (2 * pair, 2 * pair + 1):
                m_mats[res, sb] = jnp.where(lane == head, ms[res, sb, head], m_mats[res, sb])

    scores(*units[0])
    for i, unit in enumerate(units):
        if i + 1 < len(units):
            scores(*units[i + 1])
        probs(*unit)
        values(*unit)
    for (res, sb), m_mat in m_mats.items():
        st_ref[res, sb * blk:(sb + 1) * blk, :] = m_mat + l_mats[res, sb]


def _attn_kernel(*refs, n_res, n_sub, group, single_step):
    if single_step:
        q_ref, kc_ref, vc_ref = refs[:3]
        _attend(q_ref, None, kc_ref, None, vc_ref, *refs[3:], first=None, n_res=n_res, n_sub=n_sub, group=group)
    else:
        _attend(*refs, first=jnp.minimum(pl.program_id(2), 1), n_res=n_res, n_sub=n_sub, group=group)


def _attn_tables(branch, slopes):
    window, d = BRANCHES[branch]
    assert window // d <= ATTN_BLOCK and 2 * N_HEADS <= LANES
    return jnp.asarray(_bias_table(slopes, d, window // d)), jnp.asarray(_ones_table(), dtype=BF16)


def _head_group(n_sub):
    return 2 if n_sub >= 4 else 8


def _attn_branch(q, k, v, branch, slopes, min_blocks=8, max_blocks=16):
    window, d = BRANCHES[branch]
    bsz, _, packed_sub, _ = q.shape
    sub = 2 * packed_sub
    blk = ATTN_BLOCK
    blocks_per_step = min(max_blocks, max(min_blocks, 2 * (sub // blk)))
    n_sub = min(blocks_per_step, sub // blk)
    n_res = min(blocks_per_step // n_sub, d)
    group = _head_group(n_sub)
    qb = n_sub * blk
    assert sub % qb == 0 and d % n_res == 0
    bias, ones = _attn_tables(branch, slopes)
    cur = lambda rows, w: pl.BlockSpec((None, n_res, rows, w), lambda b, r, j: (b, r, j, 0))
    prev = pl.BlockSpec((None, n_res, blk // 2, D_ATTN),
                        lambda b, r, j: (b, r, jnp.maximum(n_sub * j - 1, 0), 0))
    cur_in = cur(qb // 2, D_ATTN)
    single_step = sub == qb
    if single_step:
        qkv_specs, qkv = [cur_in, cur_in, cur_in], (q, k, v)
    else:
        qkv_specs, qkv = [cur_in, prev, cur_in, prev, cur_in], (q, k, k, v, v)
    return pl.pallas_call(
        functools.partial(_attn_kernel, n_res=n_res, n_sub=n_sub, group=group, single_step=single_step),
        grid=(bsz, d // n_res, sub // qb),
        in_specs=qkv_specs + [_resident(bias.shape), _resident(ones.shape)],
        out_specs=[cur(qb, D_ATTN), cur(qb, LANES)],
        out_shape=[
            jax.ShapeDtypeStruct((bsz, d, sub, D_ATTN), F32),
            jax.ShapeDtypeStruct((bsz, d, sub, LANES), F32),
        ],
        scratch_shapes=[pltpu.VMEM((2, group, blk, 2 * blk), F32),
                        pltpu.VMEM((2, group, blk, 2 * blk), BF16)],
        compiler_params=_params("arbitrary", "arbitrary", "arbitrary"),
        name=f"dilated_attn_{branch}",
    )(*qkv, bias, ones)


def _combine_kernel(q_ref, kp_ref, kc_ref, vp_ref, vc_ref, bias_ref, ones_ref,
                    o1_ref, o2_ref, l1_ref, l2_ref, x_ref, mod_ref, wo_ref, expand_ref,
                    out_ref, o0_ref, l0_ref, s_scr, p_scr, nat_o_ref, nat_l_ref, *, tm, group):
    _attend(q_ref, kp_ref, kc_ref, vp_ref, vc_ref, bias_ref, ones_ref, o0_ref, l0_ref, s_scr, p_scr,
            first=jnp.minimum(pl.program_id(1), 1), n_res=1, n_sub=tm // ATTN_BLOCK, group=group)
    o_refs = (o0_ref, o1_ref, o2_ref)
    l_refs = (l0_ref, l1_ref, l2_ref)
    n_slabs = D_ATTN // LANES
    for g, (_, d) in enumerate(BRANCHES):
        if d == 1:
            continue
        n = tm // d
        for r in range(d):
            nat_l_ref[g - 1, pl.ds(r, n, stride=d), :] = l_refs[g][r]
            for c in range(n_slabs):
                nat_o_ref[g - 1, c, pl.ds(r, n, stride=d), :] = o_refs[g][r, :, c * LANES:(c + 1) * LANES]
    stats = [l0_ref[0], nat_l_ref[0], nat_l_ref[1]]
    m = jnp.maximum(jnp.maximum(stats[0], stats[1]), stats[2])
    es = [jnp.exp(st - m) for st in stats]
    sums = [pltpu.roll(st, LANES - N_HEADS, axis=1) for st in stats]
    inv = 1.0 / (es[0] * sums[0] + es[1] * sums[1] + es[2] * sums[2])
    wts = [e * inv for e in es]
    head_lanes = lax.broadcasted_iota(jnp.int32, (tm, LANES), 1) < N_HEADS
    splits = []
    for w in wts:
        w = jnp.where(head_lanes, w, 0.0)
        hi = w.astype(BF16)
        lo = (w - hi.astype(F32)).astype(BF16)
        splits.append(jnp.concatenate([hi, lo], axis=1))
    cols = []
    wide = 2 * LANES
    for c in range(D_ATTN // wide):
        cs = slice(c * wide, (c + 1) * wide)
        outs = [o0_ref[0, :, cs]] + [
            jnp.concatenate([nat_o_ref[g, 2 * c], nat_o_ref[g, 2 * c + 1]], axis=1) for g in range(2)]
        acc = None
        for g in range(N_BRANCHES):
            term = _dot(splits[g], expand_ref[:, cs]) * outs[g]
            acc = term if acc is None else acc + term
        cols.append(acc.astype(BF16))
    o = jnp.concatenate(cols, axis=-1)
    mod = mod_ref[0]
    out_ref[0] = x_ref[0] + mod[2:3] * _dot(o, wo_ref[...])


def _combine(qkv0, slopes0, outs, stats, x, mods, layer, w_o, w_o_layer, tm=512):
    bsz, s, d_model = x.shape
    dils = [d for _, d in BRANCHES]
    assert dils[0] == 1
    blk = ATTN_BLOCK
    n_sub = tm // blk
    group = _head_group(n_sub)
    res_major = lambda w: [pl.BlockSpec((None, d, tm // d, w), lambda b, i: (b, 0, i, 0)) for d in dils[1:]]
    cur = pl.BlockSpec((None, 1, tm // 2, D_ATTN), lambda b, i: (b, 0, i, 0))
    prev = pl.BlockSpec((None, 1, blk // 2, D_ATTN), lambda b, i: (b, 0, jnp.maximum(n_sub * i - 1, 0), 0))
    bias, ones = _attn_tables(0, slopes0)
    head_rows = np.zeros((LANES, D_ATTN), np.float32)
    head_rows[np.arange(D_ATTN) // HEAD_DIM, np.arange(D_ATTN)] = 1.0
    expand = jnp.asarray(np.concatenate([head_rows, head_rows]), dtype=BF16)
    q0, k0, v0 = qkv0
    return pl.pallas_call(
        functools.partial(_combine_kernel, tm=tm, group=group),
        grid=(bsz, s // tm),
        in_specs=[cur, prev, cur, prev, cur, _resident(bias.shape), _resident(ones.shape)]
        + res_major(D_ATTN) + res_major(LANES) + [
            pl.BlockSpec((1, tm, d_model), lambda b, i: (b, i, 0)),
            _mod_spec(layer, d_model),
            _resident_layer(w_o.shape, w_o_layer),
            _resident(expand.shape),
        ],
        out_specs=pl.BlockSpec((1, tm, d_model), lambda b, i: (b, i, 0)),
        out_shape=jax.ShapeDtypeStruct(x.shape, F32),
        scratch_shapes=[pltpu.VMEM((1, tm, D_ATTN), F32),
                        pltpu.VMEM((1, tm, LANES), F32),
                        pltpu.VMEM((2, group, blk, 2 * blk), F32),
                        pltpu.VMEM((2, group, blk, 2 * blk), BF16),
                        pltpu.VMEM((N_BRANCHES - 1, D_ATTN // LANES, tm, LANES), F32),
                        pltpu.VMEM((N_BRANCHES - 1, tm, LANES), F32)],
        compiler_params=_params("arbitrary", "arbitrary"),
        name="branch_mix_out_proj",
    )(q0, k0, k0, v0, v0, bias, ones, *outs, *stats, x, mods, w_o, expand)


def kernel(x, c, ada_w, ada_b, norm1_g, norm2_g, pool_w_in, pool_w_grp, pool_scale, pool_w_out,
           kv_norm_g, kv_ada_w, kv_ada_b, w_kv, attn_w_q, attn_w_o,
           ffn_w_up, ffn_conv_w, ffn_conv_b, ffn_w_down, final_g):
    bsz, s, d = x.shape
    depth = ada_w.shape[0]
    n_pool = pool_w_in.shape[0]
    slopes = _alibi_slopes(N_BRANCHES * N_HEADS).reshape(N_BRANCHES, N_HEADS)

    mods = _ada(c, ada_w, ada_b).reshape(depth, bsz, 6, d)
    kv_mod = _ada(c, kv_ada_w[None], kv_ada_b[None]).reshape(bsz, 2, 1, d)

    pool_w = [w.astype(BF16) for w in (pool_w_in, pool_w_grp, pool_w_out)]
    w_kv_b, w_q_b, w_o_b = w_kv[None].astype(BF16), attn_w_q.astype(BF16), attn_w_o.astype(BF16)

    ks = vs = None
    for layer in range(depth):
        if layer < n_pool:
            x = _pool_layer(x, mods, layer, norm1_g[:n_pool], pool_w[0], pool_w[1], pool_scale, pool_w[2])
        else:
            if layer == n_pool:
                kv = _proj(x, kv_mod[:, 0], kv_mod[:, 1], kv_norm_g, w_kv_b, 0)
                ks, vs = kv[:N_BRANCHES], kv[N_BRANCHES:]
            jl = layer - n_pool
            mod = mods[layer]
            q = _proj(x, mod[:, 0:1], mod[:, 1:2], norm1_g[layer], w_q_b, jl, out_scale=HEAD_DIM ** -0.5)
            outs, stats = zip(*[_attn_branch(q[g], ks[g], vs[g], g, slopes[g]) for g in range(1, N_BRANCHES)])
            x = _combine((q[0], ks[0], vs[0]), slopes[0], outs, stats, x, mods, layer, w_o_b, jl)
        x = _ffn_layer(x, mods, layer, norm2_g, ffn_w_up, ffn_conv_w, ffn_conv_b, ffn_w_down, final_g,
                       final_norm=(layer == depth - 1))
    return x
```
